```python
import jax, jax.numpy as jnp
from jax import lax
import numpy as np

D_MODEL = 1024
BATCH = 8
SEQ = 2048
DEPTH = 4
DEC_BATCH = 128
DEC_SEQ = 4
PAST_LEN = 16384
PAGE_SIZE = 128

D_SC = D_MODEL // 2
SC_WIDTH = 3
D_RK = D_MODEL
RK_HEAD_DIM = 64
RK_HEADS = D_RK // RK_HEAD_DIM
DECAY_LORA = 64
AAA_LORA = 64
MV_LORA = 32
GATE_LORA = 128
RK_PROJ = 3 * D_RK + DECAY_LORA + AAA_LORA + MV_LORA + GATE_LORA
RK_SPLITS = [D_RK, 2 * D_RK, 3 * D_RK, 3 * D_RK + DECAY_LORA, 3 * D_RK + DECAY_LORA + AAA_LORA,
             3 * D_RK + DECAY_LORA + AAA_LORA + MV_LORA]
D_CF = D_MODEL // 2
CF_WIDTH = 31
N_BRANCH = 3
N_IN = 3 * D_SC + RK_PROJ + 2 * D_CF + N_BRANCH * D_MODEL
IN_SPLITS = [3 * D_SC, 3 * D_SC + RK_PROJ, 3 * D_SC + RK_PROJ + 2 * D_CF]
D_FF = 4 * D_MODEL
D_PLE = 256
RMS_EPS = 1e-6
LN_EPS = 1e-5
GN_EPS = 64e-5

kernel_name = "hybrid_sconv_rwkv7_conformer_decode_step"


def rmsnorm(x, g):
    xf = x.astype(jnp.float32)
    y = xf * lax.rsqrt(jnp.mean(xf * xf, axis=-1, keepdims=True) + RMS_EPS)
    return (y * g.astype(jnp.float32)).astype(x.dtype)


def layernorm(x, g, b, eps):
    xf = x.astype(jnp.float32)
    mu = jnp.mean(xf, axis=-1, keepdims=True)
    xc = xf - mu
    var = jnp.mean(xc * xc, axis=-1, keepdims=True)
    return (xc * lax.rsqrt(var + eps) * g.astype(jnp.float32) + b.astype(jnp.float32)).astype(x.dtype)


def causal_dwconv(x, buf, w):
    kw = w.shape[0]
    xp = jnp.concatenate([buf.astype(x.dtype), x], axis=1)
    y = lax.conv_general_dilated(xp, w[:, None, :].astype(x.dtype), window_strides=(1,), padding='VALID',
                                 dimension_numbers=('NWC', 'WIO', 'NWC'), feature_group_count=x.shape[-1])
    return y, xp[:, xp.shape[1] - (kw - 1):]


def wkv7_scan(r, w, k, v, kk, a, s0):
    def step(S, inp):
        r_t, w_t, k_t, v_t, kk_t, a_t = inp
        sa = jnp.einsum('bhij,bhj->bhi', S, -kk_t)
        S = (S * w_t[:, :, None, :] + sa[..., None] * (kk_t * a_t)[:, :, None, :]
             + v_t[..., None] * k_t[:, :, None, :])
        o = jnp.einsum('bhij,bhj->bhi', S, r_t)
        return S, o
    xs = tuple(jnp.moveaxis(t, 1, 0) for t in (r, w, k, v, kk, a))
    s_fin, o = lax.scan(step, s0, xs)
    return jnp.moveaxis(o, 0, 1), s_fin


def trunk(x, p, sc0, shift0, wkv0, cf0, W):
    f32 = jnp.float32
    bsz, T, _ = x.shape
    h = x
    v_first = None
    new_sc, new_shift, new_wkv, new_cf = [], [], [], []
    for i in range(DEPTH):
        xn = rmsnorm(h, W['norm1_g'][i])
        z = xn @ W['w_in'][i]
        z_sc, z_rk, z_cf, z_gate = jnp.split(z, IN_SPLITS, axis=-1)

        b_sc, c_sc, u_sc = jnp.split(z_sc, 3, axis=-1)
        conv_a, sc_buf = causal_dwconv(c_sc * u_sc, sc0[i], W['sc_conv_w'][i])
        y_a = (b_sc * conv_a) @ W['sc_w_out'][i]
        new_sc.append(sc_buf.astype(sc0.dtype))

        z_prev = jnp.concatenate([shift0[i][:, None].astype(z_rk.dtype), z_rk[:, :-1]], axis=1)
        zm = z_rk + (z_prev - z_rk) * W['rk_mu'][i]
        new_shift.append(z_rk[:, -1].astype(shift0.dtype))
        r, k, v, w_lo, a_lo, v_lo, g_lo = jnp.split(zm, RK_SPLITS, axis=-1)
        w_log = -jax.nn.softplus(-(W['rk_w0'][i] + jnp.tanh(w_lo) @ W['rk_w2'][i])) - 0.5
        decay = jnp.exp(-jnp.exp(w_log.astype(f32)))
        a = jax.nn.sigmoid(W['rk_a0'][i] + a_lo @ W['rk_a2'][i])
        if i == 0:
            v_first = v
        else:
            v = v + (v_first - v) * jax.nn.sigmoid(W['rk_v0'][i] + v_lo @ W['rk_v2'][i])
        g = jax.nn.sigmoid(g_lo) @ W['rk_g2'][i]
        heads = lambda t: t.reshape(bsz, T, RK_HEADS, RK_HEAD_DIM).astype(f32)
        kk = heads(k * W['rk_k_k'][i])
        kk = kk / jnp.maximum(jnp.sqrt(jnp.sum(kk * kk, axis=-1, keepdims=True)), 1e-12)
        k = k * (1.0 + (a - 1.0) * W['rk_k_a'][i])
        rh, kh, vh, ah, wh = heads(r), heads(k), heads(v), heads(a), heads(decay)
        o, s_fin = wkv7_scan(rh, wh, kh, vh, kk, ah, wkv0[i].astype(f32))
        new_wkv.append(s_fin.astype(wkv0.dtype))
        o = layernorm(o, W['rk_ln_g'][i].reshape(RK_HEADS, RK_HEAD_DIM),
                      W['rk_ln_b'][i].reshape(RK_HEADS, RK_HEAD_DIM), GN_EPS)
        o = o + jnp.sum(rh * kh * W['rk_r_k'][i].astype(f32), axis=-1, keepdims=True) * vh
        y_b = (o.reshape(bsz, T, D_RK).astype(x.dtype) * g) @ W['rk_w_out'][i]

        zc = z_cf + W['cf_b_in'][i]
        glu = zc[..., :D_CF] * jax.nn.sigmoid(zc[..., D_CF:])
        conv_c, cf_buf = causal_dwconv(glu, cf0[i], W['cf_dw_w'][i])
        new_cf.append(cf_buf.astype(cf0.dtype))
        c = jax.nn.silu(layernorm(conv_c + W['cf_dw_b'][i], W['cf_ln_g'][i], W['cf_ln_b'][i], LN_EPS))
        y_c = c @ W['cf_w_out'][i] + W['cf_b_out'][i]

        g_a, g_b, g_c = jnp.split(jax.nn.sigmoid(z_gate), N_BRANCH, axis=-1)
        h = h + (g_a * y_a + g_b * y_b + g_c * y_c) @ W['w_o'][i]

        hn = rmsnorm(h, W['norm2_g'][i])
        h = h + jnp.square(jax.nn.relu(hn @ W['mlp_w1'][i])) @ W['mlp_w2'][i]

        gate = jax.nn.sigmoid(rmsnorm(h, W['ple_norm_g'][i]) @ W['ple_gate_w'][i])
        h = h + gate * (p[i] @ W['ple_w'][i])
    y = rmsnorm(h, W['final_norm_g'])
    return y, jnp.stack(new_sc), jnp.stack(new_shift), jnp.stack(new_wkv), jnp.stack(new_cf)


def setup_inputs(seed: int = 0) -> dict:
    key = jax.random.key(seed)
    ks = iter(jax.random.split(key, 64))
    nrm = lambda shape, s: jax.random.normal(next(ks), shape, jnp.float32) * s
    gain = lambda shape: 1.0 + nrm(shape, 0.02)
    L = DEPTH
    return {
        'x_prompt': nrm((BATCH, SEQ, D_MODEL), 1.0),
        'x_sample': nrm((DEC_BATCH, DEC_SEQ, D_MODEL), 1.0),
        'p_prompt': nrm((DEPTH, BATCH, SEQ, D_PLE), 1.0),
        'p_sample': nrm((DEPTH, DEC_BATCH, DEC_SEQ, D_PLE), 1.0),
        'state_sconv': nrm((DEPTH, DEC_BATCH, SC_WIDTH - 1, D_SC), 1.0),
        'state_shift': nrm((DEPTH, DEC_BATCH, RK_PROJ), 1.0),
        'state_wkv': nrm((DEPTH, DEC_BATCH, RK_HEADS, RK_HEAD_DIM, RK_HEAD_DIM), 0.1),
        'state_cconv': nrm((DEPTH, DEC_BATCH, CF_WIDTH - 1, D_CF), 1.0),
        'norm1_g': gain((L, D_MODEL)),
        'w_in': nrm((L, D_MODEL, N_IN), D_MODEL ** -0.5),
        'sc_conv_w': nrm((L, SC_WIDTH, D_SC), SC_WIDTH ** -0.5),
        'sc_w_out': nrm((L, D_SC, D_MODEL), D_SC ** -0.5),
        'rk_mu': jax.random.uniform(next(ks), (L, RK_PROJ), jnp.float32),
        'rk_w0': -1.0 + nrm((L, D_RK), 0.5),
        'rk_w2': nrm((L, DECAY_LORA, D_RK), 0.1 * DECAY_LORA ** -0.5),
        'rk_a0': nrm((L, D_RK), 0.1),
        'rk_a2': nrm((L, AAA_LORA, D_RK), 0.5 * AAA_LORA ** -0.5),
        'rk_v0': 1.0 + nrm((L, D_RK), 0.1),
        'rk_v2': nrm((L, MV_LORA, D_RK), 0.5 * MV_LORA ** -0.5),
        'rk_g2': nrm((L, GATE_LORA, D_RK), GATE_LORA ** -0.5),
        'rk_k_k': 0.85 + nrm((L, D_RK), 0.05),
        'rk_k_a': 1.0 + nrm((L, D_RK), 0.05),
        'rk_r_k': nrm((L, RK_HEADS, RK_HEAD_DIM), 0.1),
        'rk_ln_g': gain((L, D_RK)),
        'rk_ln_b': nrm((L, D_RK), 0.02),
        'rk_w_out': nrm((L, D_RK, D_MODEL), D_RK ** -0.5),
        'cf_b_in': nrm((L, 2 * D_CF), 0.02),
        'cf_dw_w': nrm((L, CF_WIDTH, D_CF), CF_WIDTH ** -0.5),
        'cf_dw_b': nrm((L, D_CF), 0.02),
        'cf_ln_g': gain((L, D_CF)),
        'cf_ln_b': nrm((L, D_CF), 0.02),
        'cf_w_out': nrm((L, D_CF, D_MODEL), D_CF ** -0.5),
        'cf_b_out': nrm((L, D_MODEL), 0.02),
        'w_o': nrm((L, D_MODEL, D_MODEL), (N_BRANCH * D_MODEL) ** -0.5),
        'norm2_g': gain((L, D_MODEL)),
        'mlp_w1': nrm((L, D_MODEL, D_FF), D_MODEL ** -0.5),
        'mlp_w2': nrm((L, D_FF, D_MODEL), D_FF ** -0.5),
        'ple_w': nrm((L, D_PLE, D_MODEL), D_PLE ** -0.5),
        'ple_gate_w': nrm((L, D_MODEL, D_MODEL), D_MODEL ** -0.5),
        'ple_norm_g': gain((L, D_MODEL)),
        'final_norm_g': gain((D_MODEL,)),
    }


def reference(x_prompt, x_sample, p_prompt, p_sample, state_sconv, state_shift, state_wkv, state_cconv,
              norm1_g, w_in, sc_conv_w, sc_w_out, rk_mu, rk_w0, rk_w2, rk_a0, rk_a2, rk_v0, rk_v2, rk_g2,
              rk_k_k, rk_k_a, rk_r_k, rk_ln_g, rk_ln_b, rk_w_out, cf_b_in, cf_dw_w, cf_dw_b, cf_ln_g, cf_ln_b,
              cf_w_out, cf_b_out, w_o, norm2_g, mlp_w1, mlp_w2, ple_w, ple_gate_w, ple_norm_g, final_norm_g):
    W = dict(norm1_g=norm1_g, w_in=w_in, sc_conv_w=sc_conv_w, sc_w_out=sc_w_out, rk_mu=rk_mu, rk_w0=rk_w0,
             rk_w2=rk_w2, rk_a0=rk_a0, rk_a2=rk_a2, rk_v0=rk_v0, rk_v2=rk_v2, rk_g2=rk_g2, rk_k_k=rk_k_k,
             rk_k_a=rk_k_a, rk_r_k=rk_r_k, rk_ln_g=rk_ln_g, rk_ln_b=rk_ln_b, rk_w_out=rk_w_out, cf_b_in=cf_b_in,
             cf_dw_w=cf_dw_w, cf_dw_b=cf_dw_b, cf_ln_g=cf_ln_g, cf_ln_b=cf_ln_b, cf_w_out=cf_w_out,
             cf_b_out=cf_b_out, w_o=w_o, norm2_g=norm2_g, mlp_w1=mlp_w1, mlp_w2=mlp_w2, ple_w=ple_w,
             ple_gate_w=ple_gate_w, ple_norm_g=ple_norm_g, final_norm_g=final_norm_g)
    sc0 = jnp.zeros((DEPTH, BATCH, SC_WIDTH - 1, D_SC), state_sconv.dtype)
    sh0 = jnp.zeros((DEPTH, BATCH, RK_PROJ), state_shift.dtype)
    wkv0 = jnp.zeros((DEPTH, BATCH, RK_HEADS, RK_HEAD_DIM, RK_HEAD_DIM), state_wkv.dtype)
    cf0 = jnp.zeros((DEPTH, BATCH, CF_WIDTH - 1, D_CF), state_cconv.dtype)
    y_prompt, sc_p, sh_p, wkv_p, cf_p = trunk(x_prompt, p_prompt, sc0, sh0, wkv0, cf0, W)
    y_sample, sc_s, sh_s, wkv_s, cf_s = trunk(x_sample, p_sample, state_sconv, state_shift, state_wkv,
                                              state_cconv, W)
    return (y_prompt, y_sample, sc_p, sh_p, wkv_p, cf_p, sc_s, sh_s, wkv_s, cf_s)
```

```python
import functools

import jax
import jax.numpy as jnp
from jax import lax
from jax.experimental import pallas as pl
from jax.experimental.pallas import tpu as pltpu

D_MODEL = 1024
DEPTH = 4
D_SC = 512
D_CF = 512
SC_WIDTH = 3
CF_WIDTH = 31
HEAD_DIM = 64
HEADS = 16
D_FF = 4096
D_PLE = 256
DECAY_LORA, AAA_LORA, MV_LORA, GATE_LORA = 64, 64, 32, 128
N_LORA = DECAY_LORA + AAA_LORA + MV_LORA + GATE_LORA
RK_PROJ = 3 * D_MODEL + N_LORA
RMS_EPS = 1e-6
LN_EPS = 1e-5
GN_EPS = 64e-5

Z_BLK = 3 * D_MODEL
NZ = 3 * Z_BLK
Z2_SC = 0
Z2_CF = 3 * D_SC
Z2_LORA = Z2_CF + 2 * D_CF
LORA_PAD = Z_BLK - Z2_LORA

LANES = 128
VMEM_LIMIT_BYTES = 56 * 1024 * 1024


def _cparams(sem):
    return pltpu.CompilerParams(dimension_semantics=sem, vmem_limit_bytes=VMEM_LIMIT_BYTES)


def _sigmoid(x):
    return 1.0 / (1.0 + jnp.exp(-x))


def _rms(x, g):
    return x * lax.rsqrt(jnp.mean(x * x, axis=-1, keepdims=True) + RMS_EPS) * g


def _bdot(a, b):
    return jnp.dot(a.astype(jnp.bfloat16), b, preferred_element_type=jnp.float32)


def _in_proj_kernel(x_ref, g_ref, w_ref, o_ref):
    o_ref[...] = _bdot(_rms(x_ref[...], g_ref[...]), w_ref[...])


def _in_proj(x, g, w, tm):
    n = x.shape[0]
    return pl.pallas_call(
        _in_proj_kernel,
        grid=(NZ // Z_BLK, n // tm),
        in_specs=[
            pl.BlockSpec((tm, D_MODEL), lambda j, i: (i, 0)),
            pl.BlockSpec((1, D_MODEL), lambda j, i: (0, 0)),
            pl.BlockSpec((D_MODEL, Z_BLK), lambda j, i: (0, j)),
        ],
        out_specs=pl.BlockSpec((tm, Z_BLK), lambda j, i: (i, j)),
        out_shape=jax.ShapeDtypeStruct((n, NZ), jnp.float32),
        compiler_params=_cparams(("arbitrary", "arbitrary")),
        name="in_proj",
    )(x, g, w)


def _seq_kernel(first_layer, tt, *refs):
    it = iter(refs)
    z1_ref, z2_ref, sh1_ref, sh2_ref, sc0_ref, cf0_ref = (next(it) for _ in range(6))
    vf_ref = None if first_layer else next(it)
    (mu1_ref, mu2_ref, w0_ref, w2_ref, a0_ref, a2_ref, v0_ref, v2_ref, g2_ref,
     scw_ref, cfbin_ref, cfw_ref, cfdb_ref, cflg_ref, cflb_ref) = (next(it) for _ in range(15))
    (r_ref, w_ref, k_ref, v_ref, a_ref, g_ref, xa_ref, xc_ref,
     nsh1_ref, nsh2_ref, nsc_ref, ncf_ref) = (next(it) for _ in range(12))
    s1buf, s2buf, cbuf, gbuf = (next(it) for _ in range(4))

    @pl.when(pl.program_id(1) == 0)
    def _():
        s1buf[7:8, :] = sh1_ref[...]
        s2buf[7:8, :] = sh2_ref[...]
        cbuf[6:8, :] = sc0_ref[...]
        gbuf[2:CF_WIDTH + 1, :] = cf0_ref[...]

    z1 = z1_ref[...]
    s1buf[8:8 + tt, :] = z1
    zm1 = z1 + (s1buf[7:7 + tt, :] - z1) * mu1_ref[...]
    z2l = z2_ref[:, Z2_LORA:Z_BLK]
    s2buf[8:8 + tt, :] = z2l
    zm2 = z2l + (s2buf[7:7 + tt, :] - z2l) * mu2_ref[...]
    last1 = s1buf[7 + tt:8 + tt, :]
    last2 = s2buf[7 + tt:8 + tt, :]
    nsh1_ref[...] = last1
    nsh2_ref[...] = last2
    s1buf[7:8, :] = last1
    s2buf[7:8, :] = last2

    x01 = zm2[:, 0:128]
    lw = _bdot(jnp.tanh(x01), w2_ref[...])
    la = _bdot(x01, a2_ref[...])
    xw = w0_ref[...] + lw
    softplus = jnp.maximum(-xw, 0.0) + jnp.log(1.0 + jnp.exp(-jnp.abs(xw)))
    w_ref[...] = jnp.exp(-jnp.exp(-softplus - 0.5))
    a_ref[...] = _sigmoid(a0_ref[...] + la)
    g_ref[...] = _bdot(_sigmoid(zm2[:, 128:384]), g2_ref[...])
    r_ref[...] = zm1[:, 0:D_MODEL]
    k_ref[...] = zm1[:, D_MODEL:2 * D_MODEL]
    v = zm1[:, 2 * D_MODEL:3 * D_MODEL]
    if not first_layer:
        lv = _bdot(zm2[:, 128:256], v2_ref[...])
        v = v + (vf_ref[...] - v) * _sigmoid(v0_ref[...] + lv)
    v_ref[...] = v

    cu = z2_ref[:, Z2_SC + D_SC:Z2_SC + 2 * D_SC] * z2_ref[:, Z2_SC + 2 * D_SC:Z2_SC + 3 * D_SC]
    cbuf[8:8 + tt, :] = cu
    conv_a = (scw_ref[0:1, :] * cbuf[6:6 + tt, :] + scw_ref[1:2, :] * cbuf[7:7 + tt, :]
              + scw_ref[2:3, :] * cu)
    xa_ref[...] = z2_ref[:, Z2_SC:Z2_SC + D_SC] * conv_a
    tail_a = cbuf[6 + tt:8 + tt, :]
    nsc_ref[...] = tail_a
    cbuf[6:8, :] = tail_a

    zc = z2_ref[:, Z2_CF:Z2_CF + 2 * D_CF] + cfbin_ref[...]
    glu = zc[:, 0:D_CF] * _sigmoid(zc[:, D_CF:2 * D_CF])
    gbuf[CF_WIDTH + 1:CF_WIDTH + 1 + tt, :] = glu
    acc = cfw_ref[0:1, :] * gbuf[2:2 + tt, :]
    for kk in range(1, CF_WIDTH):
        acc = acc + cfw_ref[kk:kk + 1, :] * gbuf[2 + kk:2 + kk + tt, :]
    cc = acc + cfdb_ref[...]
    mu = jnp.mean(cc, axis=-1, keepdims=True)
    cd = cc - mu
    var = jnp.mean(cd * cd, axis=-1, keepdims=True)
    ln = cd * lax.rsqrt(var + LN_EPS) * cflg_ref[...] + cflb_ref[...]
    xc_ref[...] = ln * _sigmoid(ln)
    tail_c = gbuf[2 + tt:CF_WIDTH + 1 + tt, :]
    ncf_ref[...] = tail_c
    gbuf[2:CF_WIDTH + 1, :] = tail_c


def _seq(z, sh1, sh2, sc0, cf0, vfirst, lw, tt):
    bsz, tlen, _ = z.shape
    first_layer = vfirst is None
    tok = lambda width, blk: pl.BlockSpec((None, tt, width), lambda b, t, blk=blk: (b, t, blk))
    st = lambda rows, width: pl.BlockSpec((None, rows, width), lambda b, t: (b, 0, 0))
    par = lambda a: pl.BlockSpec(a.shape, lambda b, t: (0,) * a.ndim)
    params = [lw[n] for n in ('mu1', 'mu2', 'w0', 'w2p', 'a0', 'a2p', 'v0', 'v2p', 'g2p',
                              'scw', 'cfbin', 'cfw', 'cfdb', 'cflg', 'cflb')]
    args = [z, z, sh1, sh2, sc0, cf0]
    in_specs = [tok(Z_BLK, 1), tok(Z_BLK, 2), st(1, Z_BLK), st(1, LORA_PAD),
                st(SC_WIDTH - 1, D_SC), st(CF_WIDTH - 1, D_CF)]
    if not first_layer:
        args.append(vfirst)
        in_specs.append(tok(D_MODEL, 0))
    args += params
    in_specs += [par(a) for a in params]
    f32 = jnp.float32
    out_shape = ([jax.ShapeDtypeStruct((bsz, tlen, D_MODEL), f32)] * 6
                 + [jax.ShapeDtypeStruct((bsz, tlen, D_SC), f32), jax.ShapeDtypeStruct((bsz, tlen, D_CF), f32),
                    jax.ShapeDtypeStruct((bsz, 1, Z_BLK), f32), jax.ShapeDtypeStruct((bsz, 1, LORA_PAD), f32),
                    jax.ShapeDtypeStruct((bsz, SC_WIDTH - 1, D_SC), f32),
                    jax.ShapeDtypeStruct((bsz, CF_WIDTH - 1, D_CF), f32)])
    out_specs = ([tok(D_MODEL, 0)] * 6 + [tok(D_SC, 0), tok(D_CF, 0), st(1, Z_BLK), st(1, LORA_PAD),
                                           st(SC_WIDTH - 1, D_SC), st(CF_WIDTH - 1, D_CF)])
    scratch = [pltpu.VMEM((8 + tt, Z_BLK), f32), pltpu.VMEM((8 + tt, LORA_PAD), f32),
               pltpu.VMEM((8 + tt, D_SC), f32), pltpu.VMEM((CF_WIDTH + 1 + tt, D_CF), f32)]
    return pl.pallas_call(
        functools.partial(_seq_kernel, first_layer, tt),
        grid=(bsz, tlen // tt),
        in_specs=in_specs, out_specs=out_specs, out_shape=out_shape, scratch_shapes=scratch,
        compiler_params=_cparams(("arbitrary", "arbitrary")),
        name="seq",
    )(*args)


def _wkv_kernel(tts, r_ref, w_ref, k_ref, v_ref, a_ref, s0_ref, kkp_ref, kap_ref, rkp_ref, lng_ref, lnb_ref,
                y_ref, sout_ref, s_scr, o_scr):
    @pl.when(pl.program_id(1) == 0)
    def _():
        s_scr[...] = s0_ref[...]

    def step(t, carry):
        rt = r_ref[t]
        wt = w_ref[t]
        kraw = k_ref[t]
        vt = v_ref[t]
        at = a_ref[t]
        kkp = kraw * kkp_ref[...]
        nrm = jnp.sqrt(jnp.sum(kkp * kkp, axis=0, keepdims=True))
        kk = kkp / jnp.maximum(nrm, 1e-12)
        kt = kraw * (1.0 + (at - 1.0) * kap_ref[...])
        bt = kk * at
        nkk = -kk
        for i in range(HEAD_DIM):
            si = s_scr[i]
            sa = jnp.sum(si * nkk, axis=0, keepdims=True)
            sn = si * wt + sa * bt + v_ref[t, i:i + 1, :] * kt
            s_scr[i] = sn
            o_scr[i:i + 1, :] = jnp.sum(sn * rt, axis=0, keepdims=True)
        o = o_scr[...]
        mu = jnp.mean(o, axis=0, keepdims=True)
        od = o - mu
        var = jnp.mean(od * od, axis=0, keepdims=True)
        y = od * lax.rsqrt(var + GN_EPS) * lng_ref[...] + lnb_ref[...]
        bonus = jnp.sum(rt * kt * rkp_ref[...], axis=0, keepdims=True)
        y_ref[t] = y + bonus * vt
        return carry

    lax.fori_loop(0, tts, step, 0)
    sout_ref[...] = s_scr[...]


def _wkv(r, w, k, v, a, s0, hp, tts):
    tlen, _, g = r.shape
    tok = pl.BlockSpec((tts, HEAD_DIM, LANES), lambda gi, ti: (ti, 0, gi))
    stt = pl.BlockSpec((HEAD_DIM, HEAD_DIM, LANES), lambda gi, ti: (0, 0, gi))
    par = pl.BlockSpec((HEAD_DIM, LANES), lambda gi, ti: (0, 0))
    return pl.pallas_call(
        functools.partial(_wkv_kernel, tts),
        grid=(g // LANES, tlen // tts),
        in_specs=[tok] * 5 + [stt] + [par] * 5,
        out_specs=[tok, stt],
        out_shape=[jax.ShapeDtypeStruct((tlen, HEAD_DIM, g), jnp.float32),
                   jax.ShapeDtypeStruct((HEAD_DIM, HEAD_DIM, g), jnp.float32)],
        scratch_shapes=[pltpu.VMEM((HEAD_DIM, HEAD_DIM, LANES), jnp.float32),
                        pltpu.VMEM((HEAD_DIM, LANES), jnp.float32)],
        compiler_params=_cparams(("arbitrary", "arbitrary")),
        name="wkv",
    )(r, w, k, v, a, s0, hp['kk'], hp['ka'], hp['rk'], hp['lng'], hp['lnb'])


def _merge_kernel(h_ref, xa_ref, yw_ref, g_ref, xc_ref, zg_ref, wa_ref, wb_ref, wc_ref, bc_ref, wo_ref, o_ref):
    ya = _bdot(xa_ref[...], wa_ref[...])
    yb = _bdot(yw_ref[...] * g_ref[...], wb_ref[...])
    yc = _bdot(xc_ref[...], wc_ref[...]) + bc_ref[...]
    m = (_sigmoid(zg_ref[:, 0:D_MODEL]) * ya + _sigmoid(zg_ref[:, D_MODEL:2 * D_MODEL]) * yb
         + _sigmoid(zg_ref[:, 2 * D_MODEL:3 * D_MODEL]) * yc)
    o_ref[...] = h_ref[...] + _bdot(m, wo_ref[...])


def _merge(h, xa, yw, g, xc, z, lw, tm):
    n = h.shape[0]
    tok = lambda width: pl.BlockSpec((tm, width), lambda i: (i, 0))
    par = lambda a: pl.BlockSpec(a.shape, lambda i: (0,) * a.ndim)
    ws = [lw[n_] for n_ in ('sc_w_out', 'rk_w_out', 'cf_w_out', 'cf_b_out', 'w_o')]
    return pl.pallas_call(
        _merge_kernel,
        grid=(n // tm,),
        in_specs=[tok(D_MODEL), tok(D_SC), tok(D_MODEL), tok(D_MODEL), tok(D_CF), tok(Z_BLK)] + [par(a) for a in ws],
        out_specs=tok(D_MODEL),
        out_shape=jax.ShapeDtypeStruct((n, D_MODEL), jnp.float32),
        compiler_params=_cparams(("arbitrary",)),
        name="merge",
    )(h, xa, yw, g, xc, z, *ws)


def _mlp_ple_kernel(nk, h_ref, p_ref, g2_ref, w1_ref, w2_ref, gp_ref, wg_ref, wp_ref, o_ref, hn_scr, acc_scr):
    kstep = pl.program_id(1)

    @pl.when(kstep == 0)
    def _():
        hn_scr[...] = _rms(h_ref[...], g2_ref[...]).astype(jnp.bfloat16)
        acc_scr[...] = jnp.zeros_like(acc_scr)

    u = jnp.dot(hn_scr[...], w1_ref[...], preferred_element_type=jnp.float32)
    u = jnp.maximum(u, 0.0)
    acc_scr[...] += _bdot(u * u, w2_ref[...])

    @pl.when(kstep == nk - 1)
    def _():
        h1 = h_ref[...] + acc_scr[...]
        gate = _sigmoid(_bdot(_rms(h1, gp_ref[...]), wg_ref[...]))
        o_ref[...] = h1 + gate * _bdot(p_ref[...], wp_ref[...])


def _mlp_ple(h, p, lw, tm, tf):
    n = h.shape[0]
    nk = D_FF // tf
    par = lambda a: pl.BlockSpec(a.shape, lambda i, k: (0,) * a.ndim)
    return pl.pallas_call(
        functools.partial(_mlp_ple_kernel, nk),
        grid=(n // tm, nk),
        in_specs=[pl.BlockSpec((tm, D_MODEL), lambda i, k: (i, 0)),
                  pl.BlockSpec((tm, D_PLE), lambda i, k: (i, 0)),
                  par(lw['norm2_g']),
                  pl.BlockSpec((D_MODEL, tf), lambda i, k: (0, k)),
                  pl.BlockSpec((tf, D_MODEL), lambda i, k: (k, 0)),
                  par(lw['ple_norm_g']), par(lw['ple_gate_w']), par(lw['ple_w'])],
        out_specs=pl.BlockSpec((tm, D_MODEL), lambda i, k: (i, 0)),
        out_shape=jax.ShapeDtypeStruct((n, D_MODEL), jnp.float32),
        scratch_shapes=[pltpu.VMEM((tm, D_MODEL), jnp.bfloat16), pltpu.VMEM((tm, D_MODEL), jnp.float32)],
        compiler_params=_cparams(("arbitrary", "arbitrary")),
        name="mlp_ple",
    )(h, p, lw['norm2_g'], lw['mlp_w1'], lw['mlp_w2'], lw['ple_norm_g'], lw['ple_gate_w'], lw['ple_w'])


def _final_norm_kernel(x_ref, g_ref, o_ref):
    o_ref[...] = _rms(x_ref[...], g_ref[...])


def _final_norm(x, g, tm):
    n = x.shape[0]
    return pl.pallas_call(
        _final_norm_kernel,
        grid=(n // tm,),
        in_specs=[pl.BlockSpec((tm, D_MODEL), lambda i: (i, 0)), pl.BlockSpec((1, D_MODEL), lambda i: (0, 0))],
        out_specs=pl.BlockSpec((tm, D_MODEL), lambda i: (i, 0)),
        out_shape=jax.ShapeDtypeStruct((n, D_MODEL), jnp.float32),
        compiler_params=_cparams(("arbitrary",)),
        name="final_norm",
    )(x, g)


def _prep_layer(W, i):
    bf = jnp.bfloat16
    row = lambda a: a.reshape(1, -1)
    w_in = W['w_in'][i]
    n_sc, n_rk, n_cf = 3 * D_SC, RK_PROJ, 2 * D_CF
    o_rk, o_cf, o_gate = n_sc, n_sc + n_rk, n_sc + n_rk + n_cf
    w_perm = jnp.concatenate([
        w_in[:, o_gate:o_gate + 3 * D_MODEL], w_in[:, o_rk:o_rk + 3 * D_MODEL], w_in[:, 0:n_sc],
        w_in[:, o_cf:o_cf + n_cf], w_in[:, o_rk + 3 * D_MODEL:o_rk + RK_PROJ],
        jnp.zeros((D_MODEL, LORA_PAD - N_LORA), w_in.dtype)], axis=1).astype(bf)
    mu = W['rk_mu'][i]
    zrows = lambda n: jnp.zeros((n, D_MODEL), jnp.float32)
    o_a, o_v, o_g = DECAY_LORA, DECAY_LORA + AAA_LORA, DECAY_LORA + AAA_LORA + MV_LORA
    heads_t = lambda a: jnp.tile(a.reshape(HEADS, HEAD_DIM).T, (1, LANES // HEADS))
    return {
        'norm1_g': row(W['norm1_g'][i]), 'w_in': w_perm,
        'mu1': row(mu[:3 * D_MODEL]),
        'mu2': row(jnp.concatenate([mu[3 * D_MODEL:], jnp.zeros((LORA_PAD - N_LORA,), mu.dtype)])),
        'w0': row(W['rk_w0'][i]), 'a0': row(W['rk_a0'][i]), 'v0': row(W['rk_v0'][i]),
        'w2p': jnp.concatenate([W['rk_w2'][i], zrows(128 - o_a)]).astype(bf),
        'a2p': jnp.concatenate([zrows(o_a), W['rk_a2'][i]]).astype(bf),
        'v2p': jnp.concatenate([W['rk_v2'][i], zrows(128 - MV_LORA)]).astype(bf),
        'g2p': jnp.concatenate([zrows(o_g - 128), W['rk_g2'][i], zrows(384 - N_LORA)]).astype(bf),
        'scw': W['sc_conv_w'][i], 'cfbin': row(W['cf_b_in'][i]), 'cfw': W['cf_dw_w'][i],
        'cfdb': row(W['cf_dw_b'][i]), 'cflg': row(W['cf_ln_g'][i]), 'cflb': row(W['cf_ln_b'][i]),
        'head': {'kk': heads_t(W['rk_k_k'][i]), 'ka': heads_t(W['rk_k_a'][i]), 'rk': heads_t(W['rk_r_k'][i]),
                 'lng': heads_t(W['rk_ln_g'][i]), 'lnb': heads_t(W['rk_ln_b'][i])},
        'sc_w_out': W['sc_w_out'][i].astype(bf), 'rk_w_out': W['rk_w_out'][i].astype(bf),
        'cf_w_out': W['cf_w_out'][i].astype(bf), 'cf_b_out': row(W['cf_b_out'][i]), 'w_o': W['w_o'][i].astype(bf),
        'norm2_g': row(W['norm2_g'][i]), 'mlp_w1': W['mlp_w1'][i].astype(bf), 'mlp_w2': W['mlp_w2'][i].astype(bf),
        'ple_norm_g': row(W['ple_norm_g'][i]), 'ple_gate_w': W['ple_gate_w'][i].astype(bf),
        'ple_w': W['ple_w'][i].astype(bf),
    }


def _to_lanes(x):
    bsz, tlen, _ = x.shape
    return x.reshape(bsz, tlen, HEADS, HEAD_DIM).transpose(1, 3, 0, 2).reshape(tlen, HEAD_DIM, bsz * HEADS)


def _from_lanes(y, bsz):
    tlen = y.shape[0]
    return y.reshape(tlen, HEAD_DIM, bsz, HEADS).transpose(2, 0, 3, 1).reshape(bsz, tlen, D_MODEL)


def _pick(n, pref):
    return pref if n % pref == 0 else n


def _trunk(x, p, sc0, shift0, wkv0, cf0, layers, final_g, cfg):
    bsz, tlen, _ = x.shape
    n = bsz * tlen
    tm = _pick(n, cfg['tm'])
    tt = _pick(tlen, cfg['tt'])
    tts = _pick(tlen, cfg['tts'])
    h = x.reshape(n, D_MODEL)
    v_first = None
    new_sc, new_shift, new_wkv, new_cf = [], [], [], []
    for i, lw in enumerate(layers):
        z = _in_proj(h, lw['norm1_g'], lw['w_in'], tm)
        sh = shift0[i]
        sh1 = sh[:, None, :3 * D_MODEL]
        sh2 = jnp.pad(sh[:, 3 * D_MODEL:], ((0, 0), (0, LORA_PAD - N_LORA)))[:, None, :]
        (r, w, k, v, a, g, xa, xc, nsh1, nsh2, nsc, ncf) = _seq(
            z.reshape(bsz, tlen, NZ), sh1, sh2, sc0[i], cf0[i], v_first, lw, tt)
        if i == 0:
            v_first = v
        s0 = wkv0[i].transpose(2, 3, 0, 1).reshape(HEAD_DIM, HEAD_DIM, bsz * HEADS)
        y, s_out = _wkv(_to_lanes(r), _to_lanes(w), _to_lanes(k), _to_lanes(v), _to_lanes(a), s0, lw['head'], tts)
        yw = _from_lanes(y, bsz).reshape(n, D_MODEL)
        h = _merge(h, xa.reshape(n, D_SC), yw, g.reshape(n, D_MODEL), xc.reshape(n, D_CF), z, lw, tm)
        h = _mlp_ple(h, p[i].reshape(n, D_PLE), lw, tm, cfg['tf'])
        new_sc.append(nsc)
        new_shift.append(jnp.concatenate([nsh1[:, 0, :], nsh2[:, 0, :N_LORA]], axis=-1))
        new_wkv.append(s_out.reshape(HEAD_DIM, HEAD_DIM, bsz, HEADS).transpose(2, 3, 0, 1))
        new_cf.append(ncf)
    y = _final_norm(h, final_g, tm).reshape(bsz, tlen, D_MODEL)
    return y, jnp.stack(new_sc), jnp.stack(new_shift), jnp.stack(new_wkv), jnp.stack(new_cf)


PROMPT_CFG = {'tm': 512, 'tt': 128, 'tts': 16, 'tf': 1024}
SAMPLE_CFG = {'tm': 512, 'tt': 4, 'tts': 4, 'tf': 1024}


def kernel(x_prompt, x_sample, p_prompt, p_sample, state_sconv, state_shift, state_wkv, state_cconv,
           norm1_g, w_in, sc_conv_w, sc_w_out, rk_mu, rk_w0, rk_w2, rk_a0, rk_a2, rk_v0, rk_v2, rk_g2,
           rk_k_k, rk_k_a, rk_r_k, rk_ln_g, rk_ln_b, rk_w_out, cf_b_in, cf_dw_w, cf_dw_b, cf_ln_g, cf_ln_b,
           cf_w_out, cf_b_out, w_o, norm2_g, mlp_w1, mlp_w2, ple_w, ple_gate_w, ple_norm_g, final_norm_g):
    W = dict(norm1_g=norm1_g, w_in=w_in, sc_conv_w=sc_conv_w, sc_w_out=sc_w_out, rk_mu=rk_mu, rk_w0=rk_w0,
             rk_w2=rk_w2, rk_a0=rk_a0, rk_a2=rk_a2, rk_v0=rk_v0, rk_v2=rk_v2, rk_g2=rk_g2, rk_k_k=rk_k_k,
             rk_k_a=rk_k_a, rk_r_k=rk_r_k, rk_ln_g=rk_ln_g, rk_ln_b=rk_ln_b, rk_w_out=rk_w_out, cf_b_in=cf_b_in,
             cf_dw_w=cf_dw_w, cf_dw_b=cf_dw_b, cf_ln_g=cf_ln_g, cf_ln_b=cf_ln_b, cf_w_out=cf_w_out,
             cf_b_out=cf_b_out, w_o=w_o, norm2_g=norm2_g, mlp_w1=mlp_w1, mlp_w2=mlp_w2, ple_w=ple_w,
             ple_gate_w=ple_gate_w, ple_norm_g=ple_norm_g)
    layers = [_prep_layer(W, i) for i in range(DEPTH)]
    final_g = final_norm_g.reshape(1, D_MODEL)
    bp = x_prompt.shape[0]
    f32 = jnp.float32
    zsc = jnp.zeros((DEPTH, bp, SC_WIDTH - 1, D_SC), f32)
    zsh = jnp.zeros((DEPTH, bp, RK_PROJ), f32)
    zwkv = jnp.zeros((DEPTH, bp, HEADS, HEAD_DIM, HEAD_DIM), f32)
    zcf = jnp.zeros((DEPTH, bp, CF_WIDTH - 1, D_CF), f32)
    y_p, sc_p, sh_p, wkv_p, cf_p = _trunk(x_prompt, p_prompt, zsc, zsh, zwkv, zcf, layers, final_g, PROMPT_CFG)
    y_s, sc_s, sh_s, wkv_s, cf_s = _trunk(x_sample, p_sample, state_sconv, state_shift, state_wkv, state_cconv,
                                          layers, final_g, SAMPLE_CFG)
    return (y_p, y_s, sc_p, sh_p, wkv_p, cf_p, sc_s, sh_s, wkv_s, cf_s)
```

```python
import functools

import jax
import jax.numpy as jnp
from jax import lax
from jax.experimental import pallas as pl
from jax.experimental.pallas import tpu as pltpu

D_MODEL = 1024
DEPTH = 4
D_SC = 512
D_CF = 512
SC_WIDTH = 3
CF_WIDTH = 31
HEAD_DIM = 64
HEADS = 16
D_FF = 4096
D_PLE = 256
DECAY_LORA, AAA_LORA, MV_LORA, GATE_LORA = 64, 64, 32, 128
N_LORA = DECAY_LORA + AAA_LORA + MV_LORA + GATE_LORA
RK_PROJ = 3 * D_MODEL + N_LORA
RMS_EPS = 1e-6
LN_EPS = 1e-5
GN_EPS = 64e-5

Z_BLK = 3 * D_MODEL
NZ = 3 * Z_BLK
Z2_SC = 0
Z2_CF = 3 * D_SC
Z2_LORA = Z2_CF + 2 * D_CF
LORA_PAD = Z_BLK - Z2_LORA

VMEM_LIMIT_BYTES = 56 * 1024 * 1024


def _cparams(sem):
    return pltpu.CompilerParams(dimension_semantics=sem, vmem_limit_bytes=VMEM_LIMIT_BYTES)


def _sigmoid(x):
    return 1.0 / (1.0 + jnp.exp(-x))


def _rms(x, g):
    return x * lax.rsqrt(jnp.mean(x * x, axis=-1, keepdims=True) + RMS_EPS) * g


def _bdot(a, b):
    return jnp.dot(a.astype(jnp.bfloat16), b, preferred_element_type=jnp.float32)


def _in_proj_kernel(x_ref, g_ref, w_ref, o_ref):
    o_ref[...] = _bdot(_rms(x_ref[...], g_ref[...]), w_ref[...])


def _in_proj(x, g, w, tm):
    n = x.shape[0]
    return pl.pallas_call(
        _in_proj_kernel,
        grid=(NZ // Z_BLK, n // tm),
        in_specs=[
            pl.BlockSpec((tm, D_MODEL), lambda j, i: (i, 0)),
            pl.BlockSpec((1, D_MODEL), lambda j, i: (0, 0)),
            pl.BlockSpec((D_MODEL, Z_BLK), lambda j, i: (0, j)),
        ],
        out_specs=pl.BlockSpec((tm, Z_BLK), lambda j, i: (i, j)),
        out_shape=jax.ShapeDtypeStruct((n, NZ), jnp.float32),
        compiler_params=_cparams(("arbitrary", "arbitrary")),
        name="in_proj",
    )(x, g, w)


def _seq_kernel(first_layer, tt, *refs):
    it = iter(refs)
    z1_ref, z2_ref, sh1_ref, sh2_ref, sc0_ref, cf0_ref = (next(it) for _ in range(6))
    vf_ref = None if first_layer else next(it)
    (mu1_ref, mu2_ref, w0_ref, w2_ref, a0_ref, a2_ref, v0_ref, v2_ref, g2_ref,
     scw_ref, cfbin_ref, cfw_ref, cfdb_ref, cflg_ref, cflb_ref) = (next(it) for _ in range(15))
    (r_ref, w_ref, k_ref, v_ref, a_ref, g_ref, xa_ref, xc_ref,
     nsh1_ref, nsh2_ref, nsc_ref, ncf_ref) = (next(it) for _ in range(12))
    s1buf, s2buf, cbuf, gbuf = (next(it) for _ in range(4))

    @pl.when(pl.program_id(1) == 0)
    def _():
        s1buf[7:8, :] = sh1_ref[...]
        s2buf[7:8, :] = sh2_ref[...]
        cbuf[6:8, :] = sc0_ref[...]
        gbuf[2:CF_WIDTH + 1, :] = cf0_ref[...]

    z1 = z1_ref[...]
    s1buf[8:8 + tt, :] = z1
    zm1 = z1 + (s1buf[7:7 + tt, :] - z1) * mu1_ref[...]
    z2l = z2_ref[:, Z2_LORA:Z_BLK]
    s2buf[8:8 + tt, :] = z2l
    zm2 = z2l + (s2buf[7:7 + tt, :] - z2l) * mu2_ref[...]
    last1 = s1buf[7 + tt:8 + tt, :]
    last2 = s2buf[7 + tt:8 + tt, :]
    nsh1_ref[...] = last1
    nsh2_ref[...] = last2
    s1buf[7:8, :] = last1
    s2buf[7:8, :] = last2

    x01 = zm2[:, 0:128]
    lw = _bdot(jnp.tanh(x01), w2_ref[...])
    la = _bdot(x01, a2_ref[...])
    xw = w0_ref[...] + lw
    softplus = jnp.maximum(-xw, 0.0) + jnp.log(1.0 + jnp.exp(-jnp.abs(xw)))
    w_ref[...] = -jnp.exp(-softplus - 0.5)
    a_ref[...] = _sigmoid(a0_ref[...] + la)
    g_ref[...] = _bdot(_sigmoid(zm2[:, 128:384]), g2_ref[...])
    r_ref[...] = zm1[:, 0:D_MODEL]
    k_ref[...] = zm1[:, D_MODEL:2 * D_MODEL]
    v = zm1[:, 2 * D_MODEL:3 * D_MODEL]
    if not first_layer:
        lv = _bdot(zm2[:, 128:256], v2_ref[...])
        v = v + (vf_ref[...] - v) * _sigmoid(v0_ref[...] + lv)
    v_ref[...] = v

    cu = z2_ref[:, Z2_SC + D_SC:Z2_SC + 2 * D_SC] * z2_ref[:, Z2_SC + 2 * D_SC:Z2_SC + 3 * D_SC]
    cbuf[8:8 + tt, :] = cu
    conv_a = (scw_ref[0:1, :] * cbuf[6:6 + tt, :] + scw_ref[1:2, :] * cbuf[7:7 + tt, :]
              + scw_ref[2:3, :] * cu)
    xa_ref[...] = z2_ref[:, Z2_SC:Z2_SC + D_SC] * conv_a
    tail_a = cbuf[6 + tt:8 + tt, :]
    nsc_ref[...] = tail_a
    cbuf[6:8, :] = tail_a

    zc = z2_ref[:, Z2_CF:Z2_CF + 2 * D_CF] + cfbin_ref[...]
    glu = zc[:, 0:D_CF] * _sigmoid(zc[:, D_CF:2 * D_CF])
    gbuf[CF_WIDTH + 1:CF_WIDTH + 1 + tt, :] = glu
    acc = cfw_ref[0:1, :] * gbuf[2:2 + tt, :]
    for kk in range(1, CF_WIDTH):
        acc = acc + cfw_ref[kk:kk + 1, :] * gbuf[2 + kk:2 + kk + tt, :]
    cc = acc + cfdb_ref[...]
    mu = jnp.mean(cc, axis=-1, keepdims=True)
    cd = cc - mu
    var = jnp.mean(cd * cd, axis=-1, keepdims=True)
    ln = cd * lax.rsqrt(var + LN_EPS) * cflg_ref[...] + cflb_ref[...]
    xc_ref[...] = ln * _sigmoid(ln)
    tail_c = gbuf[2 + tt:CF_WIDTH + 1 + tt, :]
    ncf_ref[...] = tail_c
    gbuf[2:CF_WIDTH + 1, :] = tail_c


def _seq(z, sh1, sh2, sc0, cf0, vfirst, lw, tt):
    bsz, tlen, _ = z.shape
    first_layer = vfirst is None
    tok = lambda width, blk: pl.BlockSpec((None, tt, width), lambda b, t, blk=blk: (b, t, blk))
    st = lambda rows, width: pl.BlockSpec((None, rows, width), lambda b, t: (b, 0, 0))
    par = lambda a: pl.BlockSpec(a.shape, lambda b, t: (0,) * a.ndim)
    params = [lw[n] for n in ('mu1', 'mu2', 'w0', 'w2p', 'a0', 'a2p', 'v0', 'v2p', 'g2p',
                              'scw', 'cfbin', 'cfw', 'cfdb', 'cflg', 'cflb')]
    args = [z, z, sh1, sh2, sc0, cf0]
    in_specs = [tok(Z_BLK, 1), tok(Z_BLK, 2), st(1, Z_BLK), st(1, LORA_PAD),
                st(SC_WIDTH - 1, D_SC), st(CF_WIDTH - 1, D_CF)]
    if not first_layer:
        args.append(vfirst)
        in_specs.append(tok(D_MODEL, 0))
    args += params
    in_specs += [par(a) for a in params]
    f32 = jnp.float32
    out_shape = ([jax.ShapeDtypeStruct((bsz, tlen, D_MODEL), f32)] * 6
                 + [jax.ShapeDtypeStruct((bsz, tlen, D_SC), f32), jax.ShapeDtypeStruct((bsz, tlen, D_CF), f32),
                    jax.ShapeDtypeStruct((bsz, 1, Z_BLK), f32), jax.ShapeDtypeStruct((bsz, 1, LORA_PAD), f32),
                    jax.ShapeDtypeStruct((bsz, SC_WIDTH - 1, D_SC), f32),
                    jax.ShapeDtypeStruct((bsz, CF_WIDTH - 1, D_CF), f32)])
    out_specs = ([tok(D_MODEL, 0)] * 6 + [tok(D_SC, 0), tok(D_CF, 0), st(1, Z_BLK), st(1, LORA_PAD),
                                           st(SC_WIDTH - 1, D_SC), st(CF_WIDTH - 1, D_CF)])
    scratch = [pltpu.VMEM((8 + tt, Z_BLK), f32), pltpu.VMEM((8 + tt, LORA_PAD), f32),
               pltpu.VMEM((8 + tt, D_SC), f32), pltpu.VMEM((CF_WIDTH + 1 + tt, D_CF), f32)]
    return pl.pallas_call(
        functools.partial(_seq_kernel, first_layer, tt),
        grid=(bsz, tlen // tt),
        in_specs=in_specs, out_specs=out_specs, out_shape=out_shape, scratch_shapes=scratch,
        compiler_params=_cparams(("arbitrary", "arbitrary")),
        name="seq",
    )(*args)


CHUNK = 8
PAIR = 2 * HEAD_DIM
N_PAIR = HEADS // 2
SEG = 256
P_ROWS = 4 * CHUNK * CHUNK


def _segsum(x, ones):
    hi = x.astype(jnp.bfloat16)
    lo = (x - hi.astype(jnp.float32)).astype(jnp.bfloat16)
    outs = []
    for q in range(D_MODEL // SEG):
        sl = slice(q * SEG, (q + 1) * SEG)
        outs.append(jnp.dot(hi[:, sl], ones, preferred_element_type=jnp.float32)
                    + jnp.dot(lo[:, sl], ones, preferred_element_type=jnp.float32))
    return jnp.concatenate(outs, axis=1)


def _wkv_kernel(nb, tt, r_ref, ld_ref, k_ref, v_ref, a_ref, s0_ref, kkp_ref, kap_ref, rkp_ref, lng_ref, lnb_ref,
                ones_ref, y_ref, sout_ref, s_scr, kk_scr, km_scr, o_scr, p_scr, e_scr):
    f32, bf16 = jnp.float32, jnp.bfloat16
    ones = ones_ref[...]
    zero64 = jnp.zeros((HEAD_DIM, HEAD_DIM), f32)

    @pl.when(pl.program_id(1) == 0)
    def _():
        for bb in range(nb):
            for p in range(N_PAIR):
                top = jnp.concatenate([s0_ref[bb, 2 * p], zero64], axis=1)
                bot = jnp.concatenate([zero64, s0_ref[bb, 2 * p + 1]], axis=1)
                s_scr[bb, p] = jnp.concatenate([top, bot], axis=0)

    for bb in range(nb):
        kraw = k_ref[bb]
        kkp = kraw * kkp_ref[...]
        nrm = jnp.sqrt(_segsum(kkp * kkp, ones))
        kk_scr[bb] = kkp / jnp.maximum(nrm, 1e-12)
        km_scr[bb] = kraw * (1.0 + (a_ref[bb] - 1.0) * kap_ref[...])

    row = lax.broadcasted_iota(jnp.int32, (CHUNK, D_MODEL), 0)
    prow = lax.broadcasted_iota(jnp.int32, (PAIR, PAIR), 0) // HEAD_DIM
    pcol = lax.broadcasted_iota(jnp.int32, (PAIR, PAIR), 1) // HEAD_DIM
    same_head = prow == pcol
    bcast = lambda x, s: jnp.broadcast_to(x[s:s + 1, :], (CHUNK, D_MODEL))

    def chunk(c, carry):
        rows = pl.ds(pl.multiple_of(c * CHUNK, CHUNK), CHUNK)
        for bb in range(nb):
            r = r_ref[bb, rows, :]
            ld = ld_ref[bb, rows, :]
            v = v_ref[bb, rows, :]
            kk = kk_scr[bb, rows, :]
            cum = ld
            for sh in (1, 2, 4):
                cum = cum + jnp.where(row >= sh, pltpu.roll(cum, sh, axis=0), 0.0)
            w_cum = jnp.exp(cum)
            w_inv = jnp.exp(-cum)
            at = -kk * jnp.exp(cum - ld)
            rt = r * w_cum
            bt = kk * a_ref[bb, rows, :] * w_inv
            kt = km_scr[bb, rows, :] * w_inv
            wc = bcast(w_cum, CHUNK - 1)
            lhs = jnp.concatenate([at, rt], axis=0).astype(bf16)
            xy = []
            for p in range(N_PAIR):
                sl = slice(p * PAIR, (p + 1) * PAIR)
                xy.append(lax.dot_general(lhs[:, sl], s_scr[bb, p].astype(bf16), (((1,), (1,)), ((), ())),
                                          preferred_element_type=f32))
            xy = jnp.concatenate(xy, axis=1)
            for s in range(CHUNK):
                bs, ks = bcast(bt, s), bcast(kt, s)
                am = jnp.where(row > s, at, 0.0)
                rm = jnp.where(row >= s, rt, 0.0)
                p_scr[bb, 32 * s:32 * s + 16, :] = jnp.concatenate([bs * am, ks * am], axis=0).astype(bf16)
                p_scr[bb, 32 * s + 16:32 * s + 32, :] = jnp.concatenate([bs * rm, ks * rm], axis=0).astype(bf16)
            for q in range(D_MODEL // SEG):
                sl = slice(q * SEG, (q + 1) * SEG)
                e_scr[bb, :, sl] = jnp.dot(p_scr[bb, :, sl], ones, preferred_element_type=f32)
            acc_u = xy[0:CHUNK]
            acc_o = xy[CHUNK:2 * CHUNK]
            for s in range(CHUNK):
                vs = bcast(v, s)
                acc_u = acc_u + e_scr[bb, 32 * s + 8:32 * s + 16, :] * vs
                acc_o = acc_o + e_scr[bb, 32 * s + 24:32 * s + 32, :] * vs
            for s in range(CHUNK):
                us = bcast(acc_u, s)
                acc_u = acc_u + e_scr[bb, 32 * s:32 * s + 8, :] * us
                acc_o = acc_o + e_scr[bb, 32 * s + 16:32 * s + 24, :] * us
            o_scr[bb, rows, :] = acc_o
            uv = jnp.concatenate([acc_u, v], axis=0).astype(bf16)
            bk = jnp.concatenate([bt * wc, kt * wc], axis=0).astype(bf16)
            for p in range(N_PAIR):
                sl = slice(p * PAIR, (p + 1) * PAIR)
                upd = lax.dot_general(uv[:, sl], bk[:, sl], (((0,), (0,)), ((), ())), preferred_element_type=f32)
                s_scr[bb, p] = s_scr[bb, p] * wc[0:1, sl] + jnp.where(same_head, upd, 0.0)
        return carry

    lax.fori_loop(0, tt // CHUNK, chunk, 0)

    for bb in range(nb):
        o = o_scr[bb]
        mu = _segsum(o, ones) * (1.0 / HEAD_DIM)
        od = o - mu
        var = _segsum(od * od, ones) * (1.0 / HEAD_DIM)
        y = od * lax.rsqrt(var + GN_EPS) * lng_ref[...] + lnb_ref[...]
        bonus = _segsum(r_ref[bb] * km_scr[bb] * rkp_ref[...], ones)
        y_ref[bb] = y + bonus * v_ref[bb]

    @pl.when(pl.program_id(1) == pl.num_programs(1) - 1)
    def _():
        for bb in range(nb):
            for p in range(N_PAIR):
                sp = s_scr[bb, p]
                sout_ref[bb, 2 * p] = sp[0:HEAD_DIM, 0:HEAD_DIM]
                sout_ref[bb, 2 * p + 1] = sp[HEAD_DIM:PAIR, HEAD_DIM:PAIR]


def _wkv(r, ld, k, v, a, s0, lw, nb, tt):
    bsz, tlen, _ = r.shape
    f32 = jnp.float32
    tok = pl.BlockSpec((nb, tt, D_MODEL), lambda b, t: (b, t, 0))
    stt = pl.BlockSpec((nb, HEADS, HEAD_DIM, HEAD_DIM), lambda b, t: (b, 0, 0, 0))
    par = pl.BlockSpec((1, D_MODEL), lambda b, t: (0, 0))
    return pl.pallas_call(
        functools.partial(_wkv_kernel, nb, tt),
        grid=(bsz // nb, tlen // tt),
        in_specs=[tok] * 5 + [stt] + [par] * 5 + [pl.BlockSpec((SEG, SEG), lambda b, t: (0, 0))],
        out_specs=[tok, stt],
        out_shape=[jax.ShapeDtypeStruct((bsz, tlen, D_MODEL), f32),
                   jax.ShapeDtypeStruct((bsz, HEADS, HEAD_DIM, HEAD_DIM), f32)],
        scratch_shapes=[pltpu.VMEM((nb, N_PAIR, PAIR, PAIR), f32),
                        pltpu.VMEM((nb, tt, D_MODEL), f32), pltpu.VMEM((nb, tt, D_MODEL), f32),
                        pltpu.VMEM((nb, tt, D_MODEL), f32),
                        pltpu.VMEM((nb, P_ROWS, D_MODEL), jnp.bfloat16),
                        pltpu.VMEM((nb, P_ROWS, D_MODEL), f32)],
        compiler_params=_cparams(("arbitrary", "arbitrary")),
        name="wkv",
    )(r, ld, k, v, a, s0, lw['kk'], lw['ka'], lw['rk'], lw['lng'], lw['lnb'], lw['ones'])


def _merge_kernel(h_ref, xa_ref, yw_ref, g_ref, xc_ref, zg_ref, wa_ref, wb_ref, wc_ref, bc_ref, wo_ref, o_ref):
    ya = _bdot(xa_ref[...], wa_ref[...])
    yb = _bdot(yw_ref[...] * g_ref[...], wb_ref[...])
    yc = _bdot(xc_ref[...], wc_ref[...]) + bc_ref[...]
    m = (_sigmoid(zg_ref[:, 0:D_MODEL]) * ya + _sigmoid(zg_ref[:, D_MODEL:2 * D_MODEL]) * yb
         + _sigmoid(zg_ref[:, 2 * D_MODEL:3 * D_MODEL]) * yc)
    o_ref[...] = h_ref[...] + _bdot(m, wo_ref[...])


def _merge(h, xa, yw, g, xc, z, lw, tm):
    n = h.shape[0]
    tok = lambda width: pl.BlockSpec((tm, width), lambda i: (i, 0))
    par = lambda a: pl.BlockSpec(a.shape, lambda i: (0,) * a.ndim)
    ws = [lw[n_] for n_ in ('sc_w_out', 'rk_w_out', 'cf_w_out', 'cf_b_out', 'w_o')]
    return pl.pallas_call(
        _merge_kernel,
        grid=(n // tm,),
        in_specs=[tok(D_MODEL), tok(D_SC), tok(D_MODEL), tok(D_MODEL), tok(D_CF), tok(Z_BLK)] + [par(a) for a in ws],
        out_specs=tok(D_MODEL),
        out_shape=jax.ShapeDtypeStruct((n, D_MODEL), jnp.float32),
        compiler_params=_cparams(("arbitrary",)),
        name="merge",
    )(h, xa, yw, g, xc, z, *ws)


def _mlp_ple_kernel(nk, h_ref, p_ref, g2_ref, w1_ref, w2_ref, gp_ref, wg_ref, wp_ref, o_ref, hn_scr, acc_scr):
    kstep = pl.program_id(1)

    @pl.when(kstep == 0)
    def _():
        hn_scr[...] = _rms(h_ref[...], g2_ref[...]).astype(jnp.bfloat16)
        acc_scr[...] = jnp.zeros_like(acc_scr)

    u = jnp.dot(hn_scr[...], w1_ref[...], preferred_element_type=jnp.float32)
    u = jnp.maximum(u, 0.0)
    acc_scr[...] += _bdot(u * u, w2_ref[...])

    @pl.when(kstep == nk - 1)
    def _():
        h1 = h_ref[...] + acc_scr[...]
        gate = _sigmoid(_bdot(_rms(h1, gp_ref[...]), wg_ref[...]))
        o_ref[...] = h1 + gate * _bdot(p_ref[...], wp_ref[...])


def _mlp_ple(h, p, lw, tm, tf):
    n = h.shape[0]
    nk = D_FF // tf
    par = lambda a: pl.BlockSpec(a.shape, lambda i, k: (0,) * a.ndim)
    return pl.pallas_call(
        functools.partial(_mlp_ple_kernel, nk),
        grid=(n // tm, nk),
        in_specs=[pl.BlockSpec((tm, D_MODEL), lambda i, k: (i, 0)),
                  pl.BlockSpec((tm, D_PLE), lambda i, k: (i, 0)),
                  par(lw['norm2_g']),
                  pl.BlockSpec((D_MODEL, tf), lambda i, k: (0, k)),
                  pl.BlockSpec((tf, D_MODEL), lambda i, k: (k, 0)),
                  par(lw['ple_norm_g']), par(lw['ple_gate_w']), par(lw['ple_w'])],
        out_specs=pl.BlockSpec((tm, D_MODEL), lambda i, k: (i, 0)),
        out_shape=jax.ShapeDtypeStruct((n, D_MODEL), jnp.float32),
        scratch_shapes=[pltpu.VMEM((tm, D_MODEL), jnp.bfloat16), pltpu.VMEM((tm, D_MODEL), jnp.float32)],
        compiler_params=_cparams(("arbitrary", "arbitrary")),
        name="mlp_ple",
    )(h, p, lw['norm2_g'], lw['mlp_w1'], lw['mlp_w2'], lw['ple_norm_g'], lw['ple_gate_w'], lw['ple_w'])


def _final_norm_kernel(x_ref, g_ref, o_ref):
    o_ref[...] = _rms(x_ref[...], g_ref[...])


def _final_norm(x, g, tm):
    n = x.shape[0]
    return pl.pallas_call(
        _final_norm_kernel,
        grid=(n // tm,),
        in_specs=[pl.BlockSpec((tm, D_MODEL), lambda i: (i, 0)), pl.BlockSpec((1, D_MODEL), lambda i: (0, 0))],
        out_specs=pl.BlockSpec((tm, D_MODEL), lambda i: (i, 0)),
        out_shape=jax.ShapeDtypeStruct((n, D_MODEL), jnp.float32),
        compiler_params=_cparams(("arbitrary",)),
        name="final_norm",
    )(x, g)


def _prep_layer(W, i):
    bf = jnp.bfloat16
    row = lambda a: a.reshape(1, -1)
    w_in = W['w_in'][i]
    n_sc, n_rk, n_cf = 3 * D_SC, RK_PROJ, 2 * D_CF
    o_rk, o_cf, o_gate = n_sc, n_sc + n_rk, n_sc + n_rk + n_cf
    w_perm = jnp.concatenate([
        w_in[:, o_gate:o_gate + 3 * D_MODEL], w_in[:, o_rk:o_rk + 3 * D_MODEL], w_in[:, 0:n_sc],
        w_in[:, o_cf:o_cf + n_cf], w_in[:, o_rk + 3 * D_MODEL:o_rk + RK_PROJ],
        jnp.zeros((D_MODEL, LORA_PAD - N_LORA), w_in.dtype)], axis=1).astype(bf)
    mu = W['rk_mu'][i]
    zrows = lambda n: jnp.zeros((n, D_MODEL), jnp.float32)
    o_a, o_v, o_g = DECAY_LORA, DECAY_LORA + AAA_LORA, DECAY_LORA + AAA_LORA + MV_LORA
    seg_head = jnp.arange(SEG) // HEAD_DIM
    return {
        'ones': (seg_head[:, None] == seg_head[None, :]).astype(bf),
        'kk': row(W['rk_k_k'][i]), 'ka': row(W['rk_k_a'][i]), 'rk': row(W['rk_r_k'][i]),
        'lng': row(W['rk_ln_g'][i]), 'lnb': row(W['rk_ln_b'][i]),
        'norm1_g': row(W['norm1_g'][i]), 'w_in': w_perm,
        'mu1': row(mu[:3 * D_MODEL]),
        'mu2': row(jnp.concatenate([mu[3 * D_MODEL:], jnp.zeros((LORA_PAD - N_LORA,), mu.dtype)])),
        'w0': row(W['rk_w0'][i]), 'a0': row(W['rk_a0'][i]), 'v0': row(W['rk_v0'][i]),
        'w2p': jnp.concatenate([W['rk_w2'][i], zrows(128 - o_a)]).astype(bf),
        'a2p': jnp.concatenate([zrows(o_a), W['rk_a2'][i]]).astype(bf),
        'v2p': jnp.concatenate([W['rk_v2'][i], zrows(128 - MV_LORA)]).astype(bf),
        'g2p': jnp.concatenate([zrows(o_g - 128), W['rk_g2'][i], zrows(384 - N_LORA)]).astype(bf),
        'scw': W['sc_conv_w'][i], 'cfbin': row(W['cf_b_in'][i]), 'cfw': W['cf_dw_w'][i],
        'cfdb': row(W['cf_dw_b'][i]), 'cflg': row(W['cf_ln_g'][i]), 'cflb': row(W['cf_ln_b'][i]),
        'sc_w_out': W['sc_w_out'][i].astype(bf), 'rk_w_out': W['rk_w_out'][i].astype(bf),
        'cf_w_out': W['cf_w_out'][i].astype(bf), 'cf_b_out': row(W['cf_b_out'][i]), 'w_o': W['w_o'][i].astype(bf),
        'norm2_g': row(W['norm2_g'][i]), 'mlp_w1': W['mlp_w1'][i].astype(bf), 'mlp_w2': W['mlp_w2'][i].astype(bf),
        'ple_norm_g': row(W['ple_norm_g'][i]), 'ple_gate_w': W['ple_gate_w'][i].astype(bf),
        'ple_w': W['ple_w'][i].astype(bf),
    }


def _pick(n, pref):
    return pref if n % pref == 0 else n


def _trunk(x, p, sc0, shift0, wkv0, cf0, layers, final_g, cfg):
    bsz, tlen, _ = x.shape
    n = bsz * tlen
    tm = _pick(n, cfg['tm'])
    tt = _pick(tlen, cfg['tt'])
    tpad = -tlen % CHUNK
    tw = _pick(tlen + tpad, cfg['tw'])
    pad_t = lambda a: jnp.pad(a, ((0, 0), (0, tpad), (0, 0))) if tpad else a
    h = x.reshape(n, D_MODEL)
    v_first = None
    new_sc, new_shift, new_wkv, new_cf = [], [], [], []
    for i, lw in enumerate(layers):
        z = _in_proj(h, lw['norm1_g'], lw['w_in'], tm)
        sh = shift0[i]
        sh1 = sh[:, None, :3 * D_MODEL]
        sh2 = jnp.pad(sh[:, 3 * D_MODEL:], ((0, 0), (0, LORA_PAD - N_LORA)))[:, None, :]
        (r, w, k, v, a, g, xa, xc, nsh1, nsh2, nsc, ncf) = _seq(
            z.reshape(bsz, tlen, NZ), sh1, sh2, sc0[i], cf0[i], v_first, lw, tt)
        if i == 0:
            v_first = v
        y, s_out = _wkv(pad_t(r), pad_t(w), pad_t(k), pad_t(v), pad_t(a), wkv0[i], lw, cfg['nb'], tw)
        yw = y[:, :tlen].reshape(n, D_MODEL)
        h = _merge(h, xa.reshape(n, D_SC), yw, g.reshape(n, D_MODEL), xc.reshape(n, D_CF), z, lw, tm)
        h = _mlp_ple(h, p[i].reshape(n, D_PLE), lw, tm, cfg['tf'])
        new_sc.append(nsc)
        new_shift.append(jnp.concatenate([nsh1[:, 0, :], nsh2[:, 0, :N_LORA]], axis=-1))
        new_wkv.append(s_out)
        new_cf.append(ncf)
    y = _final_norm(h, final_g, tm).reshape(bsz, tlen, D_MODEL)
    return y, jnp.stack(new_sc), jnp.stack(new_shift), jnp.stack(new_wkv), jnp.stack(new_cf)


PROMPT_CFG = {'tm': 512, 'tt': 128, 'tw': 128, 'nb': 2, 'tf': 1024}
SAMPLE_CFG = {'tm': 512, 'tt': 4, 'tw': 8, 'nb': 2, 'tf': 1024}


def kernel(x_prompt, x_sample, p_prompt, p_sample, state_sconv, state_shift, state_wkv, state_cconv,
           norm1_g, w_in, sc_conv_w, sc_w_out, rk_mu, rk_w0, rk_w2, rk_a0, rk_a2, rk_v0, rk_v2, rk_g2,
           rk_k_k, rk_k_a, rk_r_k, rk_ln_g, rk_ln_b, rk_w_out, cf_b_in, cf_dw_w, cf_dw_b, cf_ln_g, cf_ln_b,
           cf_w_out, cf_b_out, w_o, norm2_g, mlp_w1, mlp_w2, ple_w, ple_gate_w, ple_norm_g, final_norm_g):
    W = dict(norm1_g=norm1_g, w_in=w_in, sc_conv_w=sc_conv_w, sc_w_out=sc_w_out, rk_mu=rk_mu, rk_w0=rk_w0,
             rk_w2=rk_w2, rk_a0=rk_a0, rk_a2=rk_a2, rk_v0=rk_v0, rk_v2=rk_v2, rk_g2=rk_g2, rk_k_k=rk_k_k,
             rk_k_a=rk_k_a, rk_r_k=rk_r_k, rk_ln_g=rk_ln_g, rk_ln_b=rk_ln_b, rk_w_out=rk_w_out, cf_b_in=cf_b_in,
             cf_dw_w=cf_dw_w, cf_dw_b=cf_dw_b, cf_ln_g=cf_ln_g, cf_ln_b=cf_ln_b, cf_w_out=cf_w_out,
             cf_b_out=cf_b_out, w_o=w_o, norm2_g=norm2_g, mlp_w1=mlp_w1, mlp_w2=mlp_w2, ple_w=ple_w,
             ple_gate_w=ple_gate_w, ple_norm_g=ple_norm_g)
    layers = [_prep_layer(W, i) for i in range(DEPTH)]
    final_g = final_norm_g.reshape(1, D_MODEL)
    bp = x_prompt.shape[0]
    f32 = jnp.float32
    zsc = jnp.zeros((DEPTH, bp, SC_WIDTH - 1, D_SC), f32)
    zsh = jnp.zeros((DEPTH, bp, RK_PROJ), f32)
    zwkv = jnp.zeros((DEPTH, bp, HEADS, HEAD_DIM, HEAD_DIM), f32)
    zcf = jnp.zeros((DEPTH, bp, CF_WIDTH - 1, D_CF), f32)
    y_p, sc_p, sh_p, wkv_p, cf_p = _trunk(x_prompt, p_prompt, zsc, zsh, zwkv, zcf, layers, final_g, PROMPT_CFG)
    y_s, sc_s, sh_s, wkv_s, cf_s = _trunk(x_sample, p_sample, state_sconv, state_shift, state_wkv, state_cconv,
                                          layers, final_g, SAMPLE_CFG)
    return (y_p, y_s, sc_p, sh_p, wkv_p, cf_p, sc_s, sh_s, wkv_s, cf_s)
```

```python
import functools

import jax
import jax.numpy as jnp
from jax import lax
from jax.experimental import pallas as pl
from jax.experimental.pallas import tpu as pltpu

D_MODEL = 1024
DEPTH = 4
D_SC = 512
D_CF = 512
SC_WIDTH = 3
CF_WIDTH = 31
HEAD_DIM = 64
HEADS = 16
D_FF = 4096
D_PLE = 256
DECAY_LORA, AAA_LORA, MV_LORA, GATE_LORA = 64, 64, 32, 128
N_LORA = DECAY_LORA + AAA_LORA + MV_LORA + GATE_LORA
RK_PROJ = 3 * D_MODEL + N_LORA
RMS_EPS = 1e-6
LN_EPS = 1e-5
GN_EPS = 64e-5

Z_BLK = 3 * D_MODEL
NZ = 3 * Z_BLK
Z2_SC = 0
Z2_CF = 3 * D_SC
Z2_LORA = Z2_CF + 2 * D_CF
LORA_PAD = Z_BLK - Z2_LORA

VMEM_LIMIT_BYTES = 56 * 1024 * 1024


def _cparams(sem):
    return pltpu.CompilerParams(dimension_semantics=sem, vmem_limit_bytes=VMEM_LIMIT_BYTES)


def _sigmoid(x):
    return 1.0 / (1.0 + jnp.exp(-x))


def _rms(x, g):
    return x * lax.rsqrt(jnp.mean(x * x, axis=-1, keepdims=True) + RMS_EPS) * g


def _bdot(a, b):
    return jnp.dot(a.astype(jnp.bfloat16), b, preferred_element_type=jnp.float32)


def _in_proj_kernel(x_ref, g_ref, w_ref, o_ref):
    o_ref[...] = _bdot(_rms(x_ref[...], g_ref[...]), w_ref[...])


def _in_proj(x, g, w, tm):
    n = x.shape[0]
    return pl.pallas_call(
        _in_proj_kernel,
        grid=(NZ // Z_BLK, n // tm),
        in_specs=[
            pl.BlockSpec((tm, D_MODEL), lambda j, i: (i, 0)),
            pl.BlockSpec((1, D_MODEL), lambda j, i: (0, 0)),
            pl.BlockSpec((D_MODEL, Z_BLK), lambda j, i: (0, j)),
        ],
        out_specs=pl.BlockSpec((tm, Z_BLK), lambda j, i: (i, j)),
        out_shape=jax.ShapeDtypeStruct((n, NZ), jnp.float32),
        compiler_params=_cparams(("arbitrary", "arbitrary")),
        name="in_proj",
    )(x, g, w)


def _seq_kernel(first_layer, tt, *refs):
    it = iter(refs)
    z1_ref, z2_ref, sh1_ref, sh2_ref, sc0_ref, cf0_ref = (next(it) for _ in range(6))
    vf_ref = None if first_layer else next(it)
    (mu1_ref, mu2_ref, w0_ref, w2_ref, a0_ref, a2_ref, v0_ref, v2_ref, g2_ref,
     scw_ref, cfbin_ref, cfw_ref, cfdb_ref, cflg_ref, cflb_ref) = (next(it) for _ in range(15))
    (r_ref, w_ref, k_ref, v_ref, a_ref, g_ref, xa_ref, xc_ref,
     nsh1_ref, nsh2_ref, nsc_ref, ncf_ref) = (next(it) for _ in range(12))
    s1buf, s2buf, cbuf, gbuf = (next(it) for _ in range(4))

    @pl.when(pl.program_id(1) == 0)
    def _():
        s1buf[7:8, :] = sh1_ref[...]
        s2buf[7:8, :] = sh2_ref[...]
        cbuf[6:8, :] = sc0_ref[...]
        gbuf[2:CF_WIDTH + 1, :] = cf0_ref[...]

    z1 = z1_ref[...]
    s1buf[8:8 + tt, :] = z1
    zm1 = z1 + (s1buf[7:7 + tt, :] - z1) * mu1_ref[...]
    z2l = z2_ref[:, Z2_LORA:Z_BLK]
    s2buf[8:8 + tt, :] = z2l
    zm2 = z2l + (s2buf[7:7 + tt, :] - z2l) * mu2_ref[...]
    last1 = s1buf[7 + tt:8 + tt, :]
    last2 = s2buf[7 + tt:8 + tt, :]
    nsh1_ref[...] = last1
    nsh2_ref[...] = last2
    s1buf[7:8, :] = last1
    s2buf[7:8, :] = last2

    x01 = zm2[:, 0:128]
    lw = _bdot(jnp.tanh(x01), w2_ref[...])
    la = _bdot(x01, a2_ref[...])
    xw = w0_ref[...] + lw
    softplus = jnp.maximum(-xw, 0.0) + jnp.log(1.0 + jnp.exp(-jnp.abs(xw)))
    w_ref[...] = -jnp.exp(-softplus - 0.5)
    a_ref[...] = _sigmoid(a0_ref[...] + la)
    g_ref[...] = _bdot(_sigmoid(zm2[:, 128:384]), g2_ref[...])
    r_ref[...] = zm1[:, 0:D_MODEL]
    k_ref[...] = zm1[:, D_MODEL:2 * D_MODEL]
    v = zm1[:, 2 * D_MODEL:3 * D_MODEL]
    if not first_layer:
        lv = _bdot(zm2[:, 128:256], v2_ref[...])
        v = v + (vf_ref[...] - v) * _sigmoid(v0_ref[...] + lv)
    v_ref[...] = v

    cu = z2_ref[:, Z2_SC + D_SC:Z2_SC + 2 * D_SC] * z2_ref[:, Z2_SC + 2 * D_SC:Z2_SC + 3 * D_SC]
    cbuf[8:8 + tt, :] = cu
    conv_a = (scw_ref[0:1, :] * cbuf[6:6 + tt, :] + scw_ref[1:2, :] * cbuf[7:7 + tt, :]
              + scw_ref[2:3, :] * cu)
    xa_ref[...] = z2_ref[:, Z2_SC:Z2_SC + D_SC] * conv_a
    tail_a = cbuf[6 + tt:8 + tt, :]
    nsc_ref[...] = tail_a
    cbuf[6:8, :] = tail_a

    zc = z2_ref[:, Z2_CF:Z2_CF + 2 * D_CF] + cfbin_ref[...]
    glu = zc[:, 0:D_CF] * _sigmoid(zc[:, D_CF:2 * D_CF])
    gbuf[CF_WIDTH + 1:CF_WIDTH + 1 + tt, :] = glu
    acc = cfw_ref[0:1, :] * gbuf[2:2 + tt, :]
    for kk in range(1, CF_WIDTH):
        acc = acc + cfw_ref[kk:kk + 1, :] * gbuf[2 + kk:2 + kk + tt, :]
    cc = acc + cfdb_ref[...]
    mu = jnp.mean(cc, axis=-1, keepdims=True)
    cd = cc - mu
    var = jnp.mean(cd * cd, axis=-1, keepdims=True)
    ln = cd * lax.rsqrt(var + LN_EPS) * cflg_ref[...] + cflb_ref[...]
    xc_ref[...] = ln * _sigmoid(ln)
    tail_c = gbuf[2 + tt:CF_WIDTH + 1 + tt, :]
    ncf_ref[...] = tail_c
    gbuf[2:CF_WIDTH + 1, :] = tail_c


def _seq(z, sh1, sh2, sc0, cf0, vfirst, lw, tt):
    bsz, tlen, _ = z.shape
    first_layer = vfirst is None
    tok = lambda width, blk: pl.BlockSpec((None, tt, width), lambda b, t, blk=blk: (b, t, blk))
    st = lambda rows, width: pl.BlockSpec((None, rows, width), lambda b, t: (b, 0, 0))
    par = lambda a: pl.BlockSpec(a.shape, lambda b, t: (0,) * a.ndim)
    params = [lw[n] for n in ('mu1', 'mu2', 'w0', 'w2p', 'a0', 'a2p', 'v0', 'v2p', 'g2p',
                              'scw', 'cfbin', 'cfw', 'cfdb', 'cflg', 'cflb')]
    args = [z, z, sh1, sh2, sc0, cf0]
    in_specs = [tok(Z_BLK, 1), tok(Z_BLK, 2), st(1, Z_BLK), st(1, LORA_PAD),
                st(SC_WIDTH - 1, D_SC), st(CF_WIDTH - 1, D_CF)]
    if not first_layer:
        args.append(vfirst)
        in_specs.append(tok(D_MODEL, 0))
    args += params
    in_specs += [par(a) for a in params]
    f32 = jnp.float32
    out_shape = ([jax.ShapeDtypeStruct((bsz, tlen, D_MODEL), f32)] * 6
                 + [jax.ShapeDtypeStruct((bsz, tlen, D_SC), f32), jax.ShapeDtypeStruct((bsz, tlen, D_CF), f32),
                    jax.ShapeDtypeStruct((bsz, 1, Z_BLK), f32), jax.ShapeDtypeStruct((bsz, 1, LORA_PAD), f32),
                    jax.ShapeDtypeStruct((bsz, SC_WIDTH - 1, D_SC), f32),
                    jax.ShapeDtypeStruct((bsz, CF_WIDTH - 1, D_CF), f32)])
    out_specs = ([tok(D_MODEL, 0)] * 6 + [tok(D_SC, 0), tok(D_CF, 0), st(1, Z_BLK), st(1, LORA_PAD),
                                           st(SC_WIDTH - 1, D_SC), st(CF_WIDTH - 1, D_CF)])
    scratch = [pltpu.VMEM((8 + tt, Z_BLK), f32), pltpu.VMEM((8 + tt, LORA_PAD), f32),
               pltpu.VMEM((8 + tt, D_SC), f32), pltpu.VMEM((CF_WIDTH + 1 + tt, D_CF), f32)]
    return pl.pallas_call(
        functools.partial(_seq_kernel, first_layer, tt),
        grid=(bsz, tlen // tt),
        in_specs=in_specs, out_specs=out_specs, out_shape=out_shape, scratch_shapes=scratch,
        compiler_params=_cparams(("arbitrary", "arbitrary")),
        name="seq",
    )(*args)


CHUNK = 8
PAIR = 2 * HEAD_DIM
N_PAIR = HEADS // 2
SEG = 256
P_ROWS = 4 * CHUNK * CHUNK


def _segsum(x, ones):
    xb = x.astype(jnp.bfloat16)
    return jnp.concatenate([jnp.dot(xb[:, q * SEG:(q + 1) * SEG], ones, preferred_element_type=jnp.float32)
                            for q in range(D_MODEL // SEG)], axis=1)


def _wkv_kernel(nb, tt, r_ref, ld_ref, k_ref, v_ref, a_ref, s0_ref, kkp_ref, kap_ref, rkp_ref, lng_ref, lnb_ref,
                ones_ref, y_ref, sout_ref, s_scr, kk_scr, km_scr, o_scr, p_scr, e_scr):
    f32, bf16 = jnp.float32, jnp.bfloat16
    ones = ones_ref[...]
    zero64 = jnp.zeros((HEAD_DIM, HEAD_DIM), f32)

    @pl.when(pl.program_id(1) == 0)
    def _():
        for bb in range(nb):
            for p in range(N_PAIR):
                top = jnp.concatenate([s0_ref[bb, 2 * p], zero64], axis=1)
                bot = jnp.concatenate([zero64, s0_ref[bb, 2 * p + 1]], axis=1)
                s_scr[bb, p] = jnp.concatenate([top, bot], axis=0)

    for bb in range(nb):
        kraw = k_ref[bb]
        kkp = kraw * kkp_ref[...]
        nrm = jnp.sqrt(_segsum(kkp * kkp, ones))
        kk_scr[bb] = kkp / jnp.maximum(nrm, 1e-12)
        km_scr[bb] = kraw * (1.0 + (a_ref[bb] - 1.0) * kap_ref[...])

    row = lax.broadcasted_iota(jnp.int32, (CHUNK, D_MODEL), 0)
    head0 = (lax.broadcasted_iota(jnp.int32, (2 * CHUNK, D_MODEL), 1) // HEAD_DIM) % 2 == 0
    bcast = lambda x, s: jnp.broadcast_to(x[s:s + 1, :], (CHUNK, D_MODEL))

    def chunk(c, carry):
        rows = pl.ds(pl.multiple_of(c * CHUNK, CHUNK), CHUNK)
        st = []
        for bb in range(nb):
            r = r_ref[bb, rows, :]
            ld = ld_ref[bb, rows, :]
            v = v_ref[bb, rows, :]
            kk = kk_scr[bb, rows, :]
            cum = ld
            for sh in (1, 2, 4):
                cum = cum + jnp.where(row >= sh, pltpu.roll(cum, sh, axis=0), 0.0)
            w_cum = jnp.exp(cum)
            w_inv = jnp.exp(-cum)
            at = -kk * jnp.exp(cum - ld)
            rt = r * w_cum
            bt = kk * a_ref[bb, rows, :] * w_inv
            kt = km_scr[bb, rows, :] * w_inv
            wc = bcast(w_cum, CHUNK - 1)
            lhs = jnp.concatenate([at, rt], axis=0).astype(bf16)
            xy = []
            for p in range(N_PAIR):
                sl = slice(p * PAIR, (p + 1) * PAIR)
                xy.append(lax.dot_general(lhs[:, sl], s_scr[bb, p].astype(bf16), (((1,), (1,)), ((), ())),
                                          preferred_element_type=f32))
            xy = jnp.concatenate(xy, axis=1)
            st.append((v, at, rt, bt, kt, wc, xy))
        for bb in range(nb):
            v, at, rt, bt, kt, wc, xy = st[bb]
            for s in range(CHUNK):
                bs, ks = bcast(bt, s), bcast(kt, s)
                am = jnp.where(row > s, at, 0.0)
                rm = jnp.where(row >= s, rt, 0.0)
                p_scr[bb, 32 * s:32 * s + 16, :] = jnp.concatenate([bs * am, ks * am], axis=0).astype(bf16)
                p_scr[bb, 32 * s + 16:32 * s + 32, :] = jnp.concatenate([bs * rm, ks * rm], axis=0).astype(bf16)
            for q in range(D_MODEL // SEG):
                sl = slice(q * SEG, (q + 1) * SEG)
                e_scr[bb, :, sl] = jnp.dot(p_scr[bb, :, sl], ones, preferred_element_type=f32)
        accs = []
        for bb in range(nb):
            v, at, rt, bt, kt, wc, xy = st[bb]
            pre_u = pre_o = None
            for s in range(CHUNK):
                vs = bcast(v, s)
                tu = e_scr[bb, 32 * s + 8:32 * s + 16, :] * vs
                to = e_scr[bb, 32 * s + 24:32 * s + 32, :] * vs
                pre_u = tu if pre_u is None else pre_u + tu
                pre_o = to if pre_o is None else pre_o + to
            acc_u = xy[0:CHUNK] + pre_u
            acc_o = xy[CHUNK:2 * CHUNK] + pre_o
            for s in range(CHUNK):
                us = bcast(acc_u, s)
                acc_u = acc_u + e_scr[bb, 32 * s:32 * s + 8, :] * us
                acc_o = acc_o + e_scr[bb, 32 * s + 16:32 * s + 24, :] * us
            o_scr[bb, rows, :] = acc_o
            accs.append(acc_u)
        for bb in range(nb):
            v, at, rt, bt, kt, wc, xy = st[bb]
            uv = jnp.concatenate([accs[bb], v], axis=0)
            uv = jnp.concatenate([jnp.where(head0, uv, 0.0), jnp.where(head0, 0.0, uv)], axis=0).astype(bf16)
            bk = jnp.concatenate([bt * wc, kt * wc], axis=0)
            bk = jnp.concatenate([jnp.where(head0, bk, 0.0), jnp.where(head0, 0.0, bk)], axis=0).astype(bf16)
            for p in range(N_PAIR):
                sl = slice(p * PAIR, (p + 1) * PAIR)
                upd = lax.dot_general(uv[:, sl], bk[:, sl], (((0,), (0,)), ((), ())), preferred_element_type=f32)
                s_scr[bb, p] = s_scr[bb, p] * wc[0:1, sl] + upd
        return carry

    lax.fori_loop(0, tt // CHUNK, chunk, 0)

    for bb in range(nb):
        o = o_scr[bb]
        mu = _segsum(o, ones) * (1.0 / HEAD_DIM)
        od = o - mu
        var = _segsum(od * od, ones) * (1.0 / HEAD_DIM)
        y = od * lax.rsqrt(var + GN_EPS) * lng_ref[...] + lnb_ref[...]
        bonus = _segsum(r_ref[bb] * km_scr[bb] * rkp_ref[...], ones)
        y_ref[bb] = y + bonus * v_ref[bb]

    @pl.when(pl.program_id(1) == pl.num_programs(1) - 1)
    def _():
        for bb in range(nb):
            for p in range(N_PAIR):
                sp = s_scr[bb, p]
                sout_ref[bb, 2 * p] = sp[0:HEAD_DIM, 0:HEAD_DIM]
                sout_ref[bb, 2 * p + 1] = sp[HEAD_DIM:PAIR, HEAD_DIM:PAIR]


def _wkv(r, ld, k, v, a, s0, lw, nb, tt):
    bsz, tlen, _ = r.shape
    f32 = jnp.float32
    tok = pl.BlockSpec((nb, tt, D_MODEL), lambda b, t: (b, t, 0))
    stt = pl.BlockSpec((nb, HEADS, HEAD_DIM, HEAD_DIM), lambda b, t: (b, 0, 0, 0))
    par = pl.BlockSpec((1, D_MODEL), lambda b, t: (0, 0))
    return pl.pallas_call(
        functools.partial(_wkv_kernel, nb, tt),
        grid=(bsz // nb, tlen // tt),
        in_specs=[tok] * 5 + [stt] + [par] * 5 + [pl.BlockSpec((SEG, SEG), lambda b, t: (0, 0))],
        out_specs=[tok, stt],
        out_shape=[jax.ShapeDtypeStruct((bsz, tlen, D_MODEL), f32),
                   jax.ShapeDtypeStruct((bsz, HEADS, HEAD_DIM, HEAD_DIM), f32)],
        scratch_shapes=[pltpu.VMEM((nb, N_PAIR, PAIR, PAIR), f32),
                        pltpu.VMEM((nb, tt, D_MODEL), f32), pltpu.VMEM((nb, tt, D_MODEL), f32),
                        pltpu.VMEM((nb, tt, D_MODEL), f32),
                        pltpu.VMEM((nb, P_ROWS, D_MODEL), jnp.bfloat16),
                        pltpu.VMEM((nb, P_ROWS, D_MODEL), f32)],
        compiler_params=_cparams(("arbitrary", "arbitrary")),
        name="wkv",
    )(r, ld, k, v, a, s0, lw['kk'], lw['ka'], lw['rk'], lw['lng'], lw['lnb'], lw['ones'])


def _merge_kernel(h_ref, xa_ref, yw_ref, g_ref, xc_ref, zg_ref, wa_ref, wb_ref, wc_ref, bc_ref, wo_ref, o_ref):
    ya = _bdot(xa_ref[...], wa_ref[...])
    yb = _bdot(yw_ref[...] * g_ref[...], wb_ref[...])
    yc = _bdot(xc_ref[...], wc_ref[...]) + bc_ref[...]
    m = (_sigmoid(zg_ref[:, 0:D_MODEL]) * ya + _sigmoid(zg_ref[:, D_MODEL:2 * D_MODEL]) * yb
         + _sigmoid(zg_ref[:, 2 * D_MODEL:3 * D_MODEL]) * yc)
    o_ref[...] = h_ref[...] + _bdot(m, wo_ref[...])


def _merge(h, xa, yw, g, xc, z, lw, tm):
    n = h.shape[0]
    tok = lambda width: pl.BlockSpec((tm, width), lambda i: (i, 0))
    par = lambda a: pl.BlockSpec(a.shape, lambda i: (0,) * a.ndim)
    ws = [lw[n_] for n_ in ('sc_w_out', 'rk_w_out', 'cf_w_out', 'cf_b_out', 'w_o')]
    return pl.pallas_call(
        _merge_kernel,
        grid=(n // tm,),
        in_specs=[tok(D_MODEL), tok(D_SC), tok(D_MODEL), tok(D_MODEL), tok(D_CF), tok(Z_BLK)] + [par(a) for a in ws],
        out_specs=tok(D_MODEL),
        out_shape=jax.ShapeDtypeStruct((n, D_MODEL), jnp.float32),
        compiler_params=_cparams(("arbitrary",)),
        name="merge",
    )(h, xa, yw, g, xc, z, *ws)


def _mlp_ple_kernel(nk, h_ref, p_ref, g2_ref, w1_ref, w2_ref, gp_ref, wg_ref, wp_ref, o_ref, hn_scr, acc_scr):
    kstep = pl.program_id(1)

    @pl.when(kstep == 0)
    def _():
        hn_scr[...] = _rms(h_ref[...], g2_ref[...]).astype(jnp.bfloat16)
        acc_scr[...] = jnp.zeros_like(acc_scr)

    u = jnp.dot(hn_scr[...], w1_ref[...], preferred_element_type=jnp.float32)
    u = jnp.maximum(u, 0.0)
    acc_scr[...] += _bdot(u * u, w2_ref[...])

    @pl.when(kstep == nk - 1)
    def _():
        h1 = h_ref[...] + acc_scr[...]
        gate = _sigmoid(_bdot(_rms(h1, gp_ref[...]), wg_ref[...]))
        o_ref[...] = h1 + gate * _bdot(p_ref[...], wp_ref[...])


def _mlp_ple(h, p, lw, tm, tf):
    n = h.shape[0]
    nk = D_FF // tf
    par = lambda a: pl.BlockSpec(a.shape, lambda i, k: (0,) * a.ndim)
    return pl.pallas_call(
        functools.partial(_mlp_ple_kernel, nk),
        grid=(n // tm, nk),
        in_specs=[pl.BlockSpec((tm, D_MODEL), lambda i, k: (i, 0)),
                  pl.BlockSpec((tm, D_PLE), lambda i, k: (i, 0)),
                  par(lw['norm2_g']),
                  pl.BlockSpec((D_MODEL, tf), lambda i, k: (0, k)),
                  pl.BlockSpec((tf, D_MODEL), lambda i, k: (k, 0)),
                  par(lw['ple_norm_g']), par(lw['ple_gate_w']), par(lw['ple_w'])],
        out_specs=pl.BlockSpec((tm, D_MODEL), lambda i, k: (i, 0)),
        out_shape=jax.ShapeDtypeStruct((n, D_MODEL), jnp.float32),
        scratch_shapes=[pltpu.VMEM((tm, D_MODEL), jnp.bfloat16), pltpu.VMEM((tm, D_MODEL), jnp.float32)],
        compiler_params=_cparams(("arbitrary", "arbitrary")),
        name="mlp_ple",
    )(h, p, lw['norm2_g'], lw['mlp_w1'], lw['mlp_w2'], lw['ple_norm_g'], lw['ple_gate_w'], lw['ple_w'])


def _final_norm_kernel(x_ref, g_ref, o_ref):
    o_ref[...] = _rms(x_ref[...], g_ref[...])


def _final_norm(x, g, tm):
    n = x.shape[0]
    return pl.pallas_call(
        _final_norm_kernel,
        grid=(n // tm,),
        in_specs=[pl.BlockSpec((tm, D_MODEL), lambda i: (i, 0)), pl.BlockSpec((1, D_MODEL), lambda i: (0, 0))],
        out_specs=pl.BlockSpec((tm, D_MODEL), lambda i: (i, 0)),
        out_shape=jax.ShapeDtypeStruct((n, D_MODEL), jnp.float32),
        compiler_params=_cparams(("arbitrary",)),
        name="final_norm",
    )(x, g)


def _prep_layer(W, i):
    bf = jnp.bfloat16
    row = lambda a: a.reshape(1, -1)
    w_in = W['w_in'][i]
    n_sc, n_rk, n_cf = 3 * D_SC, RK_PROJ, 2 * D_CF
    o_rk, o_cf, o_gate = n_sc, n_sc + n_rk, n_sc + n_rk + n_cf
    w_perm = jnp.concatenate([
        w_in[:, o_gate:o_gate + 3 * D_MODEL], w_in[:, o_rk:o_rk + 3 * D_MODEL], w_in[:, 0:n_sc],
        w_in[:, o_cf:o_cf + n_cf], w_in[:, o_rk + 3 * D_MODEL:o_rk + RK_PROJ],
        jnp.zeros((D_MODEL, LORA_PAD - N_LORA), w_in.dtype)], axis=1).astype(bf)
    mu = W['rk_mu'][i]
    zrows = lambda n: jnp.zeros((n, D_MODEL), jnp.float32)
    o_a, o_v, o_g = DECAY_LORA, DECAY_LORA + AAA_LORA, DECAY_LORA + AAA_LORA + MV_LORA
    seg_head = jnp.arange(SEG) // HEAD_DIM
    return {
        'ones': (seg_head[:, None] == seg_head[None, :]).astype(bf),
        'kk': row(W['rk_k_k'][i]), 'ka': row(W['rk_k_a'][i]), 'rk': row(W['rk_r_k'][i]),
        'lng': row(W['rk_ln_g'][i]), 'lnb': row(W['rk_ln_b'][i]),
        'norm1_g': row(W['norm1_g'][i]), 'w_in': w_perm,
        'mu1': row(mu[:3 * D_MODEL]),
        'mu2': row(jnp.concatenate([mu[3 * D_MODEL:], jnp.zeros((LORA_PAD - N_LORA,), mu.dtype)])),
        'w0': row(W['rk_w0'][i]), 'a0': row(W['rk_a0'][i]), 'v0': row(W['rk_v0'][i]),
        'w2p': jnp.concatenate([W['rk_w2'][i], zrows(128 - o_a)]).astype(bf),
        'a2p': jnp.concatenate([zrows(o_a), W['rk_a2'][i]]).astype(bf),
        'v2p': jnp.concatenate([W['rk_v2'][i], zrows(128 - MV_LORA)]).astype(bf),
        'g2p': jnp.concatenate([zrows(o_g - 128), W['rk_g2'][i], zrows(384 - N_LORA)]).astype(bf),
        'scw': W['sc_conv_w'][i], 'cfbin': row(W['cf_b_in'][i]), 'cfw': W['cf_dw_w'][i],
        'cfdb': row(W['cf_dw_b'][i]), 'cflg': row(W['cf_ln_g'][i]), 'cflb': row(W['cf_ln_b'][i]),
        'sc_w_out': W['sc_w_out'][i].astype(bf), 'rk_w_out': W['rk_w_out'][i].astype(bf),
        'cf_w_out': W['cf_w_out'][i].astype(bf), 'cf_b_out': row(W['cf_b_out'][i]), 'w_o': W['w_o'][i].astype(bf),
        'norm2_g': row(W['norm2_g'][i]), 'mlp_w1': W['mlp_w1'][i].astype(bf), 'mlp_w2': W['mlp_w2'][i].astype(bf),
        'ple_norm_g': row(W['ple_norm_g'][i]), 'ple_gate_w': W['ple_gate_w'][i].astype(bf),
        'ple_w': W['ple_w'][i].astype(bf),
    }


def _pick(n, pref):
    return pref if n % pref == 0 else n


def _trunk(x, p, sc0, shift0, wkv0, cf0, layers, final_g, cfg):
    bsz, tlen, _ = x.shape
    n = bsz * tlen
    tm = _pick(n, cfg['tm'])
    tt = _pick(tlen, cfg['tt'])
    tpad = -tlen % CHUNK
    tw = _pick(tlen + tpad, cfg['tw'])
    pad_t = lambda a: jnp.pad(a, ((0, 0), (0, tpad), (0, 0))) if tpad else a
    h = x.reshape(n, D_MODEL)
    v_first = None
    new_sc, new_shift, new_wkv, new_cf = [], [], [], []
    for i, lw in enumerate(layers):
        z = _in_proj(h, lw['norm1_g'], lw['w_in'], tm)
        sh = shift0[i]
        sh1 = sh[:, None, :3 * D_MODEL]
        sh2 = jnp.pad(sh[:, 3 * D_MODEL:], ((0, 0), (0, LORA_PAD - N_LORA)))[:, None, :]
        (r, w, k, v, a, g, xa, xc, nsh1, nsh2, nsc, ncf) = _seq(
            z.reshape(bsz, tlen, NZ), sh1, sh2, sc0[i], cf0[i], v_first, lw, tt)
        if i == 0:
            v_first = v
        y, s_out = _wkv(pad_t(r), pad_t(w), pad_t(k), pad_t(v), pad_t(a), wkv0[i], lw, cfg['nb'], tw)
        yw = y[:, :tlen].reshape(n, D_MODEL)
        h = _merge(h, xa.reshape(n, D_SC), yw, g.reshape(n, D_MODEL), xc.reshape(n, D_CF), z, lw, tm)
        h = _mlp_ple(h, p[i].reshape(n, D_PLE), lw, tm, cfg['tf'])
        new_sc.append(nsc)
        new_shift.append(jnp.concatenate([nsh1[:, 0, :], nsh2[:, 0, :N_LORA]], axis=-1))
        new_wkv.append(s_out)
        new_cf.append(ncf)
    y = _final_norm(h, final_g, tm).reshape(bsz, tlen, D_MODEL)
    return y, jnp.stack(new_sc), jnp.stack(new_shift), jnp.stack(new_wkv), jnp.stack(new_cf)


PROMPT_CFG = {'tm': 512, 'tt': 128, 'tw': 128, 'nb': 4, 'tf': 1024}
SAMPLE_CFG = {'tm': 512, 'tt': 4, 'tw': 8, 'nb': 8, 'tf': 1024}


def kernel(x_prompt, x_sample, p_prompt, p_sample, state_sconv, state_shift, state_wkv, state_cconv,
           norm1_g, w_in, sc_conv_w, sc_w_out, rk_mu, rk_w0, rk_w2, rk_a0, rk_a2, rk_v0, rk_v2, rk_g2,
           rk_k_k, rk_k_a, rk_r_k, rk_ln_g, rk_ln_b, rk_w_out, cf_b_in, cf_dw_w, cf_dw_b, cf_ln_g, cf_ln_b,
           cf_w_out, cf_b_out, w_o, norm2_g, mlp_w1, mlp_w2, ple_w, ple_gate_w, ple_norm_g, final_norm_g):
    W = dict(norm1_g=norm1_g, w_in=w_in, sc_conv_w=sc_conv_w, sc_w_out=sc_w_out, rk_mu=rk_mu, rk_w0=rk_w0,
             rk_w2=rk_w2, rk_a0=rk_a0, rk_a2=rk_a2, rk_v0=rk_v0, rk_v2=rk_v2, rk_g2=rk_g2, rk_k_k=rk_k_k,
             rk_k_a=rk_k_a, rk_r_k=rk_r_k, rk_ln_g=rk_ln_g, rk_ln_b=rk_ln_b, rk_w_out=rk_w_out, cf_b_in=cf_b_in,
             cf_dw_w=cf_dw_w, cf_dw_b=cf_dw_b, cf_ln_g=cf_ln_g, cf_ln_b=cf_ln_b, cf_w_out=cf_w_out,
             cf_b_out=cf_b_out, w_o=w_o, norm2_g=norm2_g, mlp_w1=mlp_w1, mlp_w2=mlp_w2, ple_w=ple_w,
             ple_gate_w=ple_gate_w, ple_norm_g=ple_norm_g)
    layers = [_prep_layer(W, i) for i in range(DEPTH)]
    final_g = final_norm_g.reshape(1, D_MODEL)
    bp = x_prompt.shape[0]
    f32 = jnp.float32
    zsc = jnp.zeros((DEPTH, bp, SC_WIDTH - 1, D_SC), f32)
    zsh = jnp.zeros((DEPTH, bp, RK_PROJ), f32)
    zwkv = jnp.zeros((DEPTH, bp, HEADS, HEAD_DIM, HEAD_DIM), f32)
    zcf = jnp.zeros((DEPTH, bp, CF_WIDTH - 1, D_CF), f32)
    y_p, sc_p, sh_p, wkv_p, cf_p = _trunk(x_prompt, p_prompt, zsc, zsh, zwkv, zcf, layers, final_g, PROMPT_CFG)
    y_s, sc_s, sh_s, wkv_s, cf_s = _trunk(x_sample, p_sample, state_sconv, state_shift, state_wkv, state_cconv,
                                          layers, final_g, SAMPLE_CFG)
    return (y_p, y_s, sc_p, sh_p, wkv_p, cf_p, sc_s, sh_s, wkv_s, cf_s)
```

```python
import functools

import jax
import jax.numpy as jnp
from jax import lax
from jax.experimental import pallas as pl
from jax.experimental.pallas import tpu as pltpu

D_MODEL = 1024
DEPTH = 4
D_SC = 512
D_CF = 512
SC_WIDTH = 3
CF_WIDTH = 31
HEAD_DIM = 64
HEADS = 16
D_FF = 4096
D_PLE = 256
DECAY_LORA, AAA_LORA, MV_LORA, GATE_LORA = 64, 64, 32, 128
N_LORA = DECAY_LORA + AAA_LORA + MV_LORA + GATE_LORA
RK_PROJ = 3 * D_MODEL + N_LORA
RMS_EPS = 1e-6
LN_EPS = 1e-5
GN_EPS = 64e-5

Z_BLK = 3 * D_MODEL
NZ = 3 * Z_BLK
Z2_SC = 0
Z2_CF = 3 * D_SC
Z2_LORA = Z2_CF + 2 * D_CF
LORA_PAD = Z_BLK - Z2_LORA

VMEM_LIMIT_BYTES = 56 * 1024 * 1024


def _cparams(sem):
    return pltpu.CompilerParams(dimension_semantics=sem, vmem_limit_bytes=VMEM_LIMIT_BYTES)


def _sigmoid(x):
    return 1.0 / (1.0 + jnp.exp(-x))


def _rms(x, g):
    return x * lax.rsqrt(jnp.mean(x * x, axis=-1, keepdims=True) + RMS_EPS) * g


def _bdot(a, b):
    return jnp.dot(a.astype(jnp.bfloat16), b, preferred_element_type=jnp.float32)


def _in_proj_kernel(x_ref, g_ref, w_ref, o_ref):
    o_ref[...] = _bdot(_rms(x_ref[...], g_ref[...]), w_ref[...])


def _lpar(a, li):
    return pl.BlockSpec((None,) + a.shape[1:], lambda *_: (li,) + (0,) * (a.ndim - 1))


def _in_proj(x, P, li, tm):
    n = x.shape[0]
    g, w = P['norm1_g'], P['w_in']
    return pl.pallas_call(
        _in_proj_kernel,
        grid=(NZ // Z_BLK, n // tm),
        in_specs=[
            pl.BlockSpec((tm, D_MODEL), lambda j, i: (i, 0)),
            _lpar(g, li),
            pl.BlockSpec((None, D_MODEL, Z_BLK), lambda j, i: (li, 0, j)),
        ],
        out_specs=pl.BlockSpec((tm, Z_BLK), lambda j, i: (i, j)),
        out_shape=jax.ShapeDtypeStruct((n, NZ), jnp.float32),
        compiler_params=_cparams(("arbitrary", "arbitrary")),
        name="in_proj",
    )(x, g, w)


def _seq_kernel(first_layer, tt, *refs):
    it = iter(refs)
    z1_ref, z2_ref, sh1_ref, sh2_ref, sc0_ref, cf0_ref = (next(it) for _ in range(6))
    vf_ref = None if first_layer else next(it)
    (mu1_ref, mu2_ref, w0_ref, w2_ref, a0_ref, a2_ref, v0_ref, v2_ref, g2_ref,
     scw_ref, cfbin_ref, cfw_ref, cfdb_ref, cflg_ref, cflb_ref) = (next(it) for _ in range(15))
    (r_ref, w_ref, k_ref, v_ref, a_ref, g_ref, xa_ref, xc_ref,
     nsh1_ref, nsh2_ref, nsc_ref, ncf_ref) = (next(it) for _ in range(12))
    s1buf, s2buf, cbuf, gbuf, pbuf = (next(it) for _ in range(5))

    @pl.when(pl.program_id(1) == 0)
    def _():
        s1buf[7:8, :] = sh1_ref[...]
        s2buf[7:8, :] = sh2_ref[...]
        cbuf[6:8, :] = sc0_ref[...]
        gbuf[2:CF_WIDTH + 1, :] = cf0_ref[...]
        gbuf[0:2, :] = jnp.zeros((2, D_CF), jnp.float32)
        gbuf[CF_WIDTH + 1 + tt:CF_WIDTH + 1 + tt + 8, :] = jnp.zeros((8, D_CF), jnp.float32)

    z1 = z1_ref[...]
    s1buf[8:8 + tt, :] = z1
    zm1 = z1 + (s1buf[7:7 + tt, :] - z1) * mu1_ref[...]
    z2l = z2_ref[:, Z2_LORA:Z_BLK]
    s2buf[8:8 + tt, :] = z2l
    zm2 = z2l + (s2buf[7:7 + tt, :] - z2l) * mu2_ref[...]
    last1 = s1buf[7 + tt:8 + tt, :]
    last2 = s2buf[7 + tt:8 + tt, :]
    nsh1_ref[...] = last1
    nsh2_ref[...] = last2
    s1buf[7:8, :] = last1
    s2buf[7:8, :] = last2

    x01 = zm2[:, 0:128]
    lw = _bdot(jnp.tanh(x01), w2_ref[...])
    la = _bdot(x01, a2_ref[...])
    xw = w0_ref[...] + lw
    softplus = jnp.maximum(-xw, 0.0) + jnp.log(1.0 + jnp.exp(-jnp.abs(xw)))
    w_ref[...] = -jnp.exp(-softplus - 0.5)
    a_ref[...] = _sigmoid(a0_ref[...] + la)
    g_ref[...] = _bdot(_sigmoid(zm2[:, 128:384]), g2_ref[...])
    r_ref[...] = zm1[:, 0:D_MODEL]
    k_ref[...] = zm1[:, D_MODEL:2 * D_MODEL]
    v = zm1[:, 2 * D_MODEL:3 * D_MODEL]
    if not first_layer:
        lv = _bdot(zm2[:, 128:256], v2_ref[...])
        v = v + (vf_ref[...] - v) * _sigmoid(v0_ref[...] + lv)
    v_ref[...] = v

    cu = z2_ref[:, Z2_SC + D_SC:Z2_SC + 2 * D_SC] * z2_ref[:, Z2_SC + 2 * D_SC:Z2_SC + 3 * D_SC]
    cbuf[8:8 + tt, :] = cu
    conv_a = (scw_ref[0:1, :] * cbuf[6:6 + tt, :] + scw_ref[1:2, :] * cbuf[7:7 + tt, :]
              + scw_ref[2:3, :] * cu)
    xa_ref[...] = z2_ref[:, Z2_SC:Z2_SC + D_SC] * conv_a
    tail_a = cbuf[6 + tt:8 + tt, :]
    nsc_ref[...] = tail_a
    cbuf[6:8, :] = tail_a

    zc = z2_ref[:, Z2_CF:Z2_CF + 2 * D_CF] + cfbin_ref[...]
    glu = zc[:, 0:D_CF] * _sigmoid(zc[:, D_CF:2 * D_CF])
    gbuf[CF_WIDTH + 1:CF_WIDTH + 1 + tt, :] = glu
    if tt % 8 == 0:
        acc = None
        for ph in range(8):
            part = None
            for j in range(ph, CF_WIDTH + 2, 8):
                if j < 2:
                    continue
                term = cfw_ref[j - 2:j - 1, :] * gbuf[j - ph:j - ph + tt + 8, :]
                part = term if part is None else part + term
            pbuf[...] = part
            sh = pbuf[ph:ph + tt, :]
            acc = sh if acc is None else acc + sh
    else:
        acc = cfw_ref[0:1, :] * gbuf[2:2 + tt, :]
        for kk in range(1, CF_WIDTH):
            acc = acc + cfw_ref[kk:kk + 1, :] * gbuf[2 + kk:2 + kk + tt, :]
    cc = acc + cfdb_ref[...]
    mu = jnp.mean(cc, axis=-1, keepdims=True)
    cd = cc - mu
    var = jnp.mean(cd * cd, axis=-1, keepdims=True)
    ln = cd * lax.rsqrt(var + LN_EPS) * cflg_ref[...] + cflb_ref[...]
    xc_ref[...] = ln * _sigmoid(ln)
    tail_c = gbuf[2 + tt:CF_WIDTH + 1 + tt, :]
    ncf_ref[...] = tail_c
    gbuf[2:CF_WIDTH + 1, :] = tail_c


def _seq(z, sh1, sh2, sc0, cf0, vfirst, P, li, tt):
    bsz, tlen, _ = z.shape
    first_layer = vfirst is None
    tok = lambda width, blk: pl.BlockSpec((None, tt, width), lambda b, t, blk=blk: (b, t, blk))
    st = lambda rows, width: pl.BlockSpec((None, rows, width), lambda b, t: (b, 0, 0))
    lst = lambda rows, width: pl.BlockSpec((None, None, rows, width), lambda b, t: (li, b, 0, 0))
    par = lambda a: _lpar(a, li)
    params = [P[n] for n in ('mu1', 'mu2', 'w0', 'w2p', 'a0', 'a2p', 'v0', 'v2p', 'g2p',
                              'scw', 'cfbin', 'cfw', 'cfdb', 'cflg', 'cflb')]
    args = [z, z, sh1, sh2, sc0, cf0]
    in_specs = [tok(Z_BLK, 1), tok(Z_BLK, 2), lst(1, Z_BLK), lst(1, LORA_PAD),
                lst(SC_WIDTH - 1, D_SC), lst(CF_WIDTH - 1, D_CF)]
    if not first_layer:
        args.append(vfirst)
        in_specs.append(tok(D_MODEL, 0))
    args += params
    in_specs += [par(a) for a in params]
    f32 = jnp.float32
    out_shape = ([jax.ShapeDtypeStruct((bsz, tlen, D_MODEL), f32)] * 6
                 + [jax.ShapeDtypeStruct((bsz, tlen, D_SC), f32), jax.ShapeDtypeStruct((bsz, tlen, D_CF), f32),
                    jax.ShapeDtypeStruct((bsz, 1, Z_BLK), f32), jax.ShapeDtypeStruct((bsz, 1, LORA_PAD), f32),
                    jax.ShapeDtypeStruct((bsz, SC_WIDTH - 1, D_SC), f32),
                    jax.ShapeDtypeStruct((bsz, CF_WIDTH - 1, D_CF), f32)])
    out_specs = ([tok(D_MODEL, 0)] * 6 + [tok(D_SC, 0), tok(D_CF, 0), st(1, Z_BLK), st(1, LORA_PAD),
                                           st(SC_WIDTH - 1, D_SC), st(CF_WIDTH - 1, D_CF)])
    scratch = [pltpu.VMEM((8 + tt, Z_BLK), f32), pltpu.VMEM((8 + tt, LORA_PAD), f32),
               pltpu.VMEM((8 + tt, D_SC), f32), pltpu.VMEM((CF_WIDTH + 1 + tt + 8, D_CF), f32),
               pltpu.VMEM((tt + 8, D_CF), f32)]
    return pl.pallas_call(
        functools.partial(_seq_kernel, first_layer, tt),
        grid=(bsz, tlen // tt),
        in_specs=in_specs, out_specs=out_specs, out_shape=out_shape, scratch_shapes=scratch,
        compiler_params=_cparams(("arbitrary", "arbitrary")),
        name="seq",
    )(*args)


CHUNK = 8
PAIR = 2 * HEAD_DIM
N_PAIR = HEADS // 2
SEG = 256
P_ROWS = 4 * CHUNK * CHUNK


def _segsum(x, ones):
    xb = x.astype(jnp.bfloat16)
    return jnp.concatenate([jnp.dot(xb[:, q * SEG:(q + 1) * SEG], ones, preferred_element_type=jnp.float32)
                            for q in range(D_MODEL // SEG)], axis=1)


def _wkv_kernel(nb, tt, r_ref, ld_ref, k_ref, v_ref, a_ref, s0_ref, kkp_ref, kap_ref, rkp_ref, lng_ref, lnb_ref,
                ones_ref, y_ref, sout_ref, s_scr, kk_scr, km_scr, o_scr, p_scr, e_scr):
    f32, bf16 = jnp.float32, jnp.bfloat16
    ones = ones_ref[...]
    zero64 = jnp.zeros((HEAD_DIM, HEAD_DIM), f32)

    @pl.when(pl.program_id(1) == 0)
    def _():
        for bb in range(nb):
            for p in range(N_PAIR):
                top = jnp.concatenate([s0_ref[bb, 2 * p], zero64], axis=1)
                bot = jnp.concatenate([zero64, s0_ref[bb, 2 * p + 1]], axis=1)
                s_scr[bb, p] = jnp.concatenate([top, bot], axis=0)

    for bb in range(nb):
        kraw = k_ref[bb]
        kkp = kraw * kkp_ref[...]
        nrm = jnp.sqrt(_segsum(kkp * kkp, ones))
        kk_scr[bb] = kkp / jnp.maximum(nrm, 1e-12)
        km_scr[bb] = kraw * (1.0 + (a_ref[bb] - 1.0) * kap_ref[...])

    row = lax.broadcasted_iota(jnp.int32, (CHUNK, D_MODEL), 0)
    head0 = (lax.broadcasted_iota(jnp.int32, (2 * CHUNK, D_MODEL), 1) // HEAD_DIM) % 2 == 0
    bcast = lambda x, s: jnp.broadcast_to(x[s:s + 1, :], (CHUNK, D_MODEL))

    def chunk(c, carry):
        rows = pl.ds(pl.multiple_of(c * CHUNK, CHUNK), CHUNK)
        st = []
        for bb in range(nb):
            r = r_ref[bb, rows, :]
            ld = ld_ref[bb, rows, :]
            v = v_ref[bb, rows, :]
            kk = kk_scr[bb, rows, :]
            cum = ld
            for sh in (1, 2, 4):
                cum = cum + jnp.where(row >= sh, pltpu.roll(cum, sh, axis=0), 0.0)
            w_cum = jnp.exp(cum)
            w_inv = jnp.exp(-cum)
            at = -kk * jnp.exp(cum - ld)
            rt = r * w_cum
            bt = kk * a_ref[bb, rows, :] * w_inv
            kt = km_scr[bb, rows, :] * w_inv
            wc = bcast(w_cum, CHUNK - 1)
            lhs = jnp.concatenate([at, rt], axis=0).astype(bf16)
            st.append([v, at, rt, bt, kt, wc, lhs])
        for bb in range(nb):
            v, at, rt, bt, kt, wc, lhs = st[bb]
            xy = []
            for p in range(N_PAIR):
                sl = slice(p * PAIR, (p + 1) * PAIR)
                xy.append(lax.dot_general(lhs[:, sl], s_scr[bb, p].astype(bf16), (((1,), (1,)), ((), ())),
                                          preferred_element_type=f32))
            st[bb][6] = jnp.concatenate(xy, axis=1)
            for s in range(CHUNK):
                bs, ks = bcast(bt, s), bcast(kt, s)
                am = jnp.where(row > s, at, 0.0)
                rm = jnp.where(row >= s, rt, 0.0)
                p_scr[bb, 32 * s:32 * s + 16, :] = jnp.concatenate([bs * am, ks * am], axis=0).astype(bf16)
                p_scr[bb, 32 * s + 16:32 * s + 32, :] = jnp.concatenate([bs * rm, ks * rm], axis=0).astype(bf16)
            for q in range(D_MODEL // SEG):
                sl = slice(q * SEG, (q + 1) * SEG)
                e_scr[bb, :, sl] = jnp.dot(p_scr[bb, :, sl], ones, preferred_element_type=f32)
        accs = []
        for bb in range(nb):
            v, at, rt, bt, kt, wc, xy = st[bb]
            pre_u = pre_o = None
            for s in range(CHUNK):
                vs = bcast(v, s)
                tu = e_scr[bb, 32 * s + 8:32 * s + 16, :] * vs
                to = e_scr[bb, 32 * s + 24:32 * s + 32, :] * vs
                pre_u = tu if pre_u is None else pre_u + tu
                pre_o = to if pre_o is None else pre_o + to
            acc_u = xy[0:CHUNK] + pre_u
            acc_o = xy[CHUNK:2 * CHUNK] + pre_o
            for s in range(CHUNK):
                us = bcast(acc_u, s)
                acc_u = acc_u + e_scr[bb, 32 * s:32 * s + 8, :] * us
                acc_o = acc_o + e_scr[bb, 32 * s + 16:32 * s + 24, :] * us
            o_scr[bb, rows, :] = acc_o
            accs.append(acc_u)
        for bb in range(nb):
            v, at, rt, bt, kt, wc, xy = st[bb]
            uv = jnp.concatenate([accs[bb], v], axis=0)
            uv = jnp.concatenate([jnp.where(head0, uv, 0.0), jnp.where(head0, 0.0, uv)], axis=0).astype(bf16)
            bk = jnp.concatenate([bt * wc, kt * wc], axis=0)
            bk = jnp.concatenate([jnp.where(head0, bk, 0.0), jnp.where(head0, 0.0, bk)], axis=0).astype(bf16)
            for p in range(N_PAIR):
                sl = slice(p * PAIR, (p + 1) * PAIR)
                upd = lax.dot_general(uv[:, sl], bk[:, sl], (((0,), (0,)), ((), ())), preferred_element_type=f32)
                s_scr[bb, p] = s_scr[bb, p] * wc[0:1, sl] + upd
        return carry

    lax.fori_loop(0, tt // CHUNK, chunk, 0)

    for bb in range(nb):
        o = o_scr[bb]
        mu = _segsum(o, ones) * (1.0 / HEAD_DIM)
        od = o - mu
        var = _segsum(od * od, ones) * (1.0 / HEAD_DIM)
        y = od * lax.rsqrt(var + GN_EPS) * lng_ref[...] + lnb_ref[...]
        bonus = _segsum(r_ref[bb] * km_scr[bb] * rkp_ref[...], ones)
        y_ref[bb] = y + bonus * v_ref[bb]

    @pl.when(pl.program_id(1) == pl.num_programs(1) - 1)
    def _():
        for bb in range(nb):
            for p in range(N_PAIR):
                sp = s_scr[bb, p]
                sout_ref[bb, 2 * p] = sp[0:HEAD_DIM, 0:HEAD_DIM]
                sout_ref[bb, 2 * p + 1] = sp[HEAD_DIM:PAIR, HEAD_DIM:PAIR]


def _wkv(r, ld, k, v, a, s0, P, li, nb, tt):
    bsz, tlen, _ = r.shape
    f32 = jnp.float32
    tok = pl.BlockSpec((nb, tt, D_MODEL), lambda b, t: (b, t, 0))
    stt = pl.BlockSpec((nb, HEADS, HEAD_DIM, HEAD_DIM), lambda b, t: (b, 0, 0, 0))
    lstt = pl.BlockSpec((None, nb, HEADS, HEAD_DIM, HEAD_DIM), lambda b, t: (li, b, 0, 0, 0))
    par = _lpar(P['kk'], li)
    return pl.pallas_call(
        functools.partial(_wkv_kernel, nb, tt),
        grid=(bsz // nb, tlen // tt),
        in_specs=[tok] * 5 + [lstt] + [par] * 5 + [pl.BlockSpec((SEG, SEG), lambda b, t: (0, 0))],
        out_specs=[tok, stt],
        out_shape=[jax.ShapeDtypeStruct((bsz, tlen, D_MODEL), f32),
                   jax.ShapeDtypeStruct((bsz, HEADS, HEAD_DIM, HEAD_DIM), f32)],
        scratch_shapes=[pltpu.VMEM((nb, N_PAIR, PAIR, PAIR), f32),
                        pltpu.VMEM((nb, tt, D_MODEL), f32), pltpu.VMEM((nb, tt, D_MODEL), f32),
                        pltpu.VMEM((nb, tt, D_MODEL), f32),
                        pltpu.VMEM((nb, P_ROWS, D_MODEL), jnp.bfloat16),
                        pltpu.VMEM((nb, P_ROWS, D_MODEL), f32)],
        compiler_params=_cparams(("arbitrary", "arbitrary")),
        name="wkv",
    )(r, ld, k, v, a, s0, P['kk'], P['ka'], P['rk'], P['lng'], P['lnb'], P['ones'])


def _merge_kernel(h_ref, xa_ref, yw_ref, g_ref, xc_ref, zg_ref, wa_ref, wb_ref, wc_ref, bc_ref, wo_ref, o_ref):
    ya = _bdot(xa_ref[...], wa_ref[...])
    yb = _bdot(yw_ref[...] * g_ref[...], wb_ref[...])
    yc = _bdot(xc_ref[...], wc_ref[...]) + bc_ref[...]
    m = (_sigmoid(zg_ref[:, 0:D_MODEL]) * ya + _sigmoid(zg_ref[:, D_MODEL:2 * D_MODEL]) * yb
         + _sigmoid(zg_ref[:, 2 * D_MODEL:3 * D_MODEL]) * yc)
    o_ref[...] = h_ref[...] + _bdot(m, wo_ref[...])


def _merge(h, xa, yw, g, xc, z, P, li, tm):
    n = h.shape[0]
    tok = lambda width: pl.BlockSpec((tm, width), lambda i: (i, 0))
    par = lambda a: _lpar(a, li)
    ws = [P[n_] for n_ in ('sc_w_out', 'rk_w_out', 'cf_w_out', 'cf_b_out', 'w_o')]
    return pl.pallas_call(
        _merge_kernel,
        grid=(n // tm,),
        in_specs=[tok(D_MODEL), tok(D_SC), tok(D_MODEL), tok(D_MODEL), tok(D_CF), tok(Z_BLK)] + [par(a) for a in ws],
        out_specs=tok(D_MODEL),
        out_shape=jax.ShapeDtypeStruct((n, D_MODEL), jnp.float32),
        compiler_params=_cparams(("arbitrary",)),
        name="merge",
    )(h, xa, yw, g, xc, z, *ws)


def _mlp_ple_kernel(nk, h_ref, p_ref, g2_ref, w1_ref, w2_ref, gp_ref, wg_ref, wp_ref, o_ref, hn_scr, acc_scr):
    kstep = pl.program_id(1)

    @pl.when(kstep == 0)
    def _():
        hn_scr[...] = _rms(h_ref[...], g2_ref[...]).astype(jnp.bfloat16)
        acc_scr[...] = jnp.zeros_like(acc_scr)

    u = jnp.dot(hn_scr[...], w1_ref[...], preferred_element_type=jnp.float32)
    u = jnp.maximum(u, 0.0)
    acc_scr[...] += _bdot(u * u, w2_ref[...])

    @pl.when(kstep == nk - 1)
    def _():
        h1 = h_ref[...] + acc_scr[...]
        gate = _sigmoid(_bdot(_rms(h1, gp_ref[...]), wg_ref[...]))
        o_ref[...] = h1 + gate * _bdot(p_ref[...], wp_ref[...])


def _mlp_ple(h, p, P, li, tm, tf):
    n = h.shape[0]
    nk = D_FF // tf
    par = lambda a: _lpar(a, li)
    return pl.pallas_call(
        functools.partial(_mlp_ple_kernel, nk),
        grid=(n // tm, nk),
        in_specs=[pl.BlockSpec((tm, D_MODEL), lambda i, k: (i, 0)),
                  pl.BlockSpec((None, tm, D_PLE), lambda i, k: (li, i, 0)),
                  par(P['norm2_g']),
                  pl.BlockSpec((None, D_MODEL, tf), lambda i, k: (li, 0, k)),
                  pl.BlockSpec((None, tf, D_MODEL), lambda i, k: (li, k, 0)),
                  par(P['ple_norm_g']), par(P['ple_gate_w']), par(P['ple_w'])],
        out_specs=pl.BlockSpec((tm, D_MODEL), lambda i, k: (i, 0)),
        out_shape=jax.ShapeDtypeStruct((n, D_MODEL), jnp.float32),
        scratch_shapes=[pltpu.VMEM((tm, D_MODEL), jnp.bfloat16), pltpu.VMEM((tm, D_MODEL), jnp.float32)],
        compiler_params=_cparams(("arbitrary", "arbitrary")),
        name="mlp_ple",
    )(h, p, P['norm2_g'], P['mlp_w1'], P['mlp_w2'], P['ple_norm_g'], P['ple_gate_w'], P['ple_w'])


def _final_norm_kernel(x_ref, g_ref, o_ref):
    o_ref[...] = _rms(x_ref[...], g_ref[...])


def _final_norm(x, g, tm):
    n = x.shape[0]
    return pl.pallas_call(
        _final_norm_kernel,
        grid=(n // tm,),
        in_specs=[pl.BlockSpec((tm, D_MODEL), lambda i: (i, 0)), pl.BlockSpec((1, D_MODEL), lambda i: (0, 0))],
        out_specs=pl.BlockSpec((tm, D_MODEL), lambda i: (i, 0)),
        out_shape=jax.ShapeDtypeStruct((n, D_MODEL), jnp.float32),
        compiler_params=_cparams(("arbitrary",)),
        name="final_norm",
    )(x, g)


def _prep(W):
    bf = jnp.bfloat16
    nl = W['w_in'].shape[0]
    row = lambda a: a.reshape(nl, 1, -1)
    w_in = W['w_in']
    n_sc, n_rk, n_cf = 3 * D_SC, RK_PROJ, 2 * D_CF
    o_rk, o_cf, o_gate = n_sc, n_sc + n_rk, n_sc + n_rk + n_cf
    w_perm = jnp.concatenate([
        w_in[:, :, o_gate:o_gate + 3 * D_MODEL], w_in[:, :, o_rk:o_rk + 3 * D_MODEL], w_in[:, :, 0:n_sc],
        w_in[:, :, o_cf:o_cf + n_cf], w_in[:, :, o_rk + 3 * D_MODEL:o_rk + RK_PROJ],
        jnp.zeros((nl, D_MODEL, LORA_PAD - N_LORA), w_in.dtype)], axis=2).astype(bf)
    mu = W['rk_mu']
    zrows = lambda n: jnp.zeros((nl, n, D_MODEL), jnp.float32)
    o_a, o_g = DECAY_LORA, DECAY_LORA + AAA_LORA + MV_LORA
    seg_head = jnp.arange(SEG) // HEAD_DIM
    return {
        'ones': (seg_head[:, None] == seg_head[None, :]).astype(bf),
        'kk': row(W['rk_k_k']), 'ka': row(W['rk_k_a']), 'rk': row(W['rk_r_k']),
        'lng': row(W['rk_ln_g']), 'lnb': row(W['rk_ln_b']),
        'norm1_g': row(W['norm1_g']), 'w_in': w_perm,
        'mu1': row(mu[:, :3 * D_MODEL]),
        'mu2': row(jnp.pad(mu[:, 3 * D_MODEL:], ((0, 0), (0, LORA_PAD - N_LORA)))),
        'w0': row(W['rk_w0']), 'a0': row(W['rk_a0']), 'v0': row(W['rk_v0']),
        'w2p': jnp.concatenate([W['rk_w2'], zrows(128 - o_a)], axis=1).astype(bf),
        'a2p': jnp.concatenate([zrows(o_a), W['rk_a2']], axis=1).astype(bf),
        'v2p': jnp.concatenate([W['rk_v2'], zrows(128 - MV_LORA)], axis=1).astype(bf),
        'g2p': jnp.concatenate([zrows(o_g - 128), W['rk_g2'], zrows(384 - N_LORA)], axis=1).astype(bf),
        'scw': W['sc_conv_w'], 'cfbin': row(W['cf_b_in']), 'cfw': W['cf_dw_w'],
        'cfdb': row(W['cf_dw_b']), 'cflg': row(W['cf_ln_g']), 'cflb': row(W['cf_ln_b']),
        'sc_w_out': W['sc_w_out'].astype(bf), 'rk_w_out': W['rk_w_out'].astype(bf),
        'cf_w_out': W['cf_w_out'].astype(bf), 'cf_b_out': row(W['cf_b_out']), 'w_o': W['w_o'].astype(bf),
        'norm2_g': row(W['norm2_g']), 'mlp_w1': W['mlp_w1'].astype(bf), 'mlp_w2': W['mlp_w2'].astype(bf),
        'ple_norm_g': row(W['ple_norm_g']), 'ple_gate_w': W['ple_gate_w'].astype(bf),
        'ple_w': W['ple_w'].astype(bf),
    }


def _pick(n, pref):
    return pref if n % pref == 0 else n


def _trunk(x, p, sc0, shift0, wkv0, cf0, P, final_g, cfg):
    bsz, tlen, _ = x.shape
    n = bsz * tlen
    tm = _pick(n, cfg['tm'])
    tt = _pick(tlen, cfg['tt'])
    tpad = -tlen % CHUNK
    tw = _pick(tlen + tpad, cfg['tw'])
    pad_t = lambda a: jnp.pad(a, ((0, 0), (0, tpad), (0, 0))) if tpad else a
    h = x.reshape(n, D_MODEL)
    pf = p.reshape(DEPTH, n, D_PLE)
    sh1 = shift0[:, :, None, :3 * D_MODEL]
    sh2 = jnp.pad(shift0[:, :, 3 * D_MODEL:], ((0, 0), (0, 0), (0, LORA_PAD - N_LORA)))[:, :, None, :]
    v_first = None
    new_sc, new_shift, new_wkv, new_cf = [], [], [], []
    for i in range(DEPTH):
        z = _in_proj(h, P, i, tm)
        (r, w, k, v, a, g, xa, xc, nsh1, nsh2, nsc, ncf) = _seq(
            z.reshape(bsz, tlen, NZ), sh1, sh2, sc0, cf0, v_first, P, i, tt)
        if i == 0:
            v_first = v
        y, s_out = _wkv(pad_t(r), pad_t(w), pad_t(k), pad_t(v), pad_t(a), wkv0, P, i, cfg['nb'], tw)
        yw = y[:, :tlen].reshape(n, D_MODEL)
        h = _merge(h, xa.reshape(n, D_SC), yw, g.reshape(n, D_MODEL), xc.reshape(n, D_CF), z, P, i, tm)
        h = _mlp_ple(h, pf, P, i, tm, cfg['tf'])
        new_sc.append(nsc)
        new_shift.append(jnp.concatenate([nsh1[:, 0, :], nsh2[:, 0, :N_LORA]], axis=-1))
        new_wkv.append(s_out)
        new_cf.append(ncf)
    y = _final_norm(h, final_g, tm).reshape(bsz, tlen, D_MODEL)
    return y, jnp.stack(new_sc), jnp.stack(new_shift), jnp.stack(new_wkv), jnp.stack(new_cf)


PROMPT_CFG = {'tm': 512, 'tt': 128, 'tw': 128, 'nb': 4, 'tf': 1024}
SAMPLE_CFG = {'tm': 512, 'tt': 4, 'tw': 8, 'nb': 8, 'tf': 1024}


def kernel(x_prompt, x_sample, p_prompt, p_sample, state_sconv, state_shift, state_wkv, state_cconv,
           norm1_g, w_in, sc_conv_w, sc_w_out, rk_mu, rk_w0, rk_w2, rk_a0, rk_a2, rk_v0, rk_v2, rk_g2,
           rk_k_k, rk_k_a, rk_r_k, rk_ln_g, rk_ln_b, rk_w_out, cf_b_in, cf_dw_w, cf_dw_b, cf_ln_g, cf_ln_b,
           cf_w_out, cf_b_out, w_o, norm2_g, mlp_w1, mlp_w2, ple_w, ple_gate_w, ple_norm_g, final_norm_g):
    W = dict(norm1_g=norm1_g, w_in=w_in, sc_conv_w=sc_conv_w, sc_w_out=sc_w_out, rk_mu=rk_mu, rk_w0=rk_w0,
             rk_w2=rk_w2, rk_a0=rk_a0, rk_a2=rk_a2, rk_v0=rk_v0, rk_v2=rk_v2, rk_g2=rk_g2, rk_k_k=rk_k_k,
             rk_k_a=rk_k_a, rk_r_k=rk_r_k, rk_ln_g=rk_ln_g, rk_ln_b=rk_ln_b, rk_w_out=rk_w_out, cf_b_in=cf_b_in,
             cf_dw_w=cf_dw_w, cf_dw_b=cf_dw_b, cf_ln_g=cf_ln_g, cf_ln_b=cf_ln_b, cf_w_out=cf_w_out,
             cf_b_out=cf_b_out, w_o=w_o, norm2_g=norm2_g, mlp_w1=mlp_w1, mlp_w2=mlp_w2, ple_w=ple_w,
             ple_gate_w=ple_gate_w, ple_norm_g=ple_norm_g)
    P = _prep(W)
    final_g = final_norm_g.reshape(1, D_MODEL)
    bp = x_prompt.shape[0]
    f32 = jnp.float32
    zsc = jnp.zeros((DEPTH, bp, SC_WIDTH - 1, D_SC), f32)
    zsh = jnp.zeros((DEPTH, bp, RK_PROJ), f32)
    zwkv = jnp.zeros((DEPTH, bp, HEADS, HEAD_DIM, HEAD_DIM), f32)
    zcf = jnp.zeros((DEPTH, bp, CF_WIDTH - 1, D_CF), f32)
    y_p, sc_p, sh_p, wkv_p, cf_p = _trunk(x_prompt, p_prompt, zsc, zsh, zwkv, zcf, P, final_g, PROMPT_CFG)
    y_s, sc_s, sh_s, wkv_s, cf_s = _trunk(x_sample, p_sample, state_sconv, state_shift, state_wkv, state_cconv,
                                          P, final_g, SAMPLE_CFG)
    return (y_p, y_s, sc_p, sh_p, wkv_p, cf_p, sc_s, sh_s, wkv_s, cf_s)
```

```python
import functools

import jax
import jax.numpy as jnp
from jax import lax
from jax.experimental import pallas as pl
from jax.experimental.pallas import tpu as pltpu

D_MODEL = 1024
DEPTH = 4
D_SC = 512
D_CF = 512
SC_WIDTH = 3
CF_WIDTH = 31
HEAD_DIM = 64
HEADS = 16
D_FF = 4096
D_PLE = 256
DECAY_LORA, AAA_LORA, MV_LORA, GATE_LORA = 64, 64, 32, 128
N_LORA = DECAY_LORA + AAA_LORA + MV_LORA + GATE_LORA
RK_PROJ = 3 * D_MODEL + N_LORA
RMS_EPS = 1e-6
LN_EPS = 1e-5
GN_EPS = 64e-5

Z_BLK = 3 * D_MODEL
NZ = 3 * Z_BLK
Z2_SC = 0
Z2_CF = 3 * D_SC
Z2_LORA = Z2_CF + 2 * D_CF
LORA_PAD = Z_BLK - Z2_LORA

VMEM_LIMIT_BYTES = 56 * 1024 * 1024


def _cparams(sem):
    return pltpu.CompilerParams(dimension_semantics=sem, vmem_limit_bytes=VMEM_LIMIT_BYTES)


def _sigmoid(x):
    return 1.0 / (1.0 + jnp.exp(-x))


def _rms(x, g):
    return x * lax.rsqrt(jnp.mean(x * x, axis=-1, keepdims=True) + RMS_EPS) * g


def _bdot(a, b):
    return jnp.dot(a.astype(jnp.bfloat16), b, preferred_element_type=jnp.float32)


def _in_proj_kernel(x_ref, g_ref, w_ref, o_ref):
    o_ref[...] = _bdot(_rms(x_ref[...], g_ref[...]), w_ref[...])


def _lpar(a, li):
    return pl.BlockSpec((None,) + a.shape[1:], lambda *_: (li,) + (0,) * (a.ndim - 1))


def _in_proj(x, P, li, tm):
    n = x.shape[0]
    g, w = P['norm1_g'], P['w_in']
    return pl.pallas_call(
        _in_proj_kernel,
        grid=(NZ // Z_BLK, n // tm),
        in_specs=[
            pl.BlockSpec((tm, D_MODEL), lambda j, i: (i, 0)),
            _lpar(g, li),
            pl.BlockSpec((None, D_MODEL, Z_BLK), lambda j, i: (li, 0, j)),
        ],
        out_specs=pl.BlockSpec((tm, Z_BLK), lambda j, i: (i, j)),
        out_shape=jax.ShapeDtypeStruct((n, NZ), jnp.float32),
        compiler_params=_cparams(("arbitrary", "arbitrary")),
        name="in_proj",
    )(x, g, w)


def _seq_kernel(first_layer, tt, *refs):
    it = iter(refs)
    z1_ref, z2_ref, sh1_ref, sh2_ref, sc0_ref, cf0_ref = (next(it) for _ in range(6))
    vf_ref = None if first_layer else next(it)
    (mu1_ref, mu2_ref, w0_ref, w2_ref, a0_ref, a2_ref, v0_ref, v2_ref, g2_ref,
     scw_ref, cfbin_ref, cfw_ref, cfdb_ref, cflg_ref, cflb_ref) = (next(it) for _ in range(15))
    (r_ref, w_ref, k_ref, v_ref, a_ref, g_ref, xa_ref, xc_ref,
     nsh1_ref, nsh2_ref, nsc_ref, ncf_ref) = (next(it) for _ in range(12))
    s1buf, s2buf, cbuf, gbuf, pbuf = (next(it) for _ in range(5))

    @pl.when(pl.program_id(1) == 0)
    def _():
        s1buf[7:8, :] = sh1_ref[...]
        s2buf[7:8, :] = sh2_ref[...]
        cbuf[6:8, :] = sc0_ref[...]
        gbuf[2:CF_WIDTH + 1, :] = cf0_ref[...]
        gbuf[0:2, :] = jnp.zeros((2, D_CF), jnp.float32)
        gbuf[CF_WIDTH + 1 + tt:CF_WIDTH + 1 + tt + 8, :] = jnp.zeros((8, D_CF), jnp.float32)

    z1 = z1_ref[...]
    s1buf[8:8 + tt, :] = z1
    zm1 = z1 + (s1buf[7:7 + tt, :] - z1) * mu1_ref[...]
    z2l = z2_ref[:, Z2_LORA:Z_BLK]
    s2buf[8:8 + tt, :] = z2l
    zm2 = z2l + (s2buf[7:7 + tt, :] - z2l) * mu2_ref[...]
    last1 = s1buf[7 + tt:8 + tt, :]
    last2 = s2buf[7 + tt:8 + tt, :]
    nsh1_ref[...] = last1
    nsh2_ref[...] = last2
    s1buf[7:8, :] = last1
    s2buf[7:8, :] = last2

    x01 = zm2[:, 0:128]
    lw = _bdot(jnp.tanh(x01), w2_ref[...])
    la = _bdot(x01, a2_ref[...])
    xw = w0_ref[...] + lw
    softplus = jnp.maximum(-xw, 0.0) + jnp.log(1.0 + jnp.exp(-jnp.abs(xw)))
    w_ref[...] = -jnp.exp(-softplus - 0.5)
    a_ref[...] = _sigmoid(a0_ref[...] + la)
    g_ref[...] = _bdot(_sigmoid(zm2[:, 128:384]), g2_ref[...])
    r_ref[...] = zm1[:, 0:D_MODEL]
    k_ref[...] = zm1[:, D_MODEL:2 * D_MODEL]
    v = zm1[:, 2 * D_MODEL:3 * D_MODEL]
    if not first_layer:
        lv = _bdot(zm2[:, 128:256], v2_ref[...])
        v = v + (vf_ref[...] - v) * _sigmoid(v0_ref[...] + lv)
    v_ref[...] = v

    cu = z2_ref[:, Z2_SC + D_SC:Z2_SC + 2 * D_SC] * z2_ref[:, Z2_SC + 2 * D_SC:Z2_SC + 3 * D_SC]
    cbuf[8:8 + tt, :] = cu
    conv_a = (scw_ref[0:1, :] * cbuf[6:6 + tt, :] + scw_ref[1:2, :] * cbuf[7:7 + tt, :]
              + scw_ref[2:3, :] * cu)
    xa_ref[...] = z2_ref[:, Z2_SC:Z2_SC + D_SC] * conv_a
    tail_a = cbuf[6 + tt:8 + tt, :]
    nsc_ref[...] = tail_a
    cbuf[6:8, :] = tail_a

    zc = z2_ref[:, Z2_CF:Z2_CF + 2 * D_CF] + cfbin_ref[...]
    glu = zc[:, 0:D_CF] * _sigmoid(zc[:, D_CF:2 * D_CF])
    gbuf[CF_WIDTH + 1:CF_WIDTH + 1 + tt, :] = glu
    if tt % 8 == 0:
        acc = None
        for ph in range(8):
            part = None
            for j in range(ph, CF_WIDTH + 2, 8):
                if j < 2:
                    continue
                term = cfw_ref[j - 2:j - 1, :] * gbuf[j - ph:j - ph + tt + 8, :]
                part = term if part is None else part + term
            pbuf[...] = part
            sh = pbuf[ph:ph + tt, :]
            acc = sh if acc is None else acc + sh
    else:
        acc = cfw_ref[0:1, :] * gbuf[2:2 + tt, :]
        for kk in range(1, CF_WIDTH):
            acc = acc + cfw_ref[kk:kk + 1, :] * gbuf[2 + kk:2 + kk + tt, :]
    cc = acc + cfdb_ref[...]
    mu = jnp.mean(cc, axis=-1, keepdims=True)
    cd = cc - mu
    var = jnp.mean(cd * cd, axis=-1, keepdims=True)
    ln = cd * lax.rsqrt(var + LN_EPS) * cflg_ref[...] + cflb_ref[...]
    xc_ref[...] = ln * _sigmoid(ln)
    tail_c = gbuf[2 + tt:CF_WIDTH + 1 + tt, :]
    ncf_ref[...] = tail_c
    gbuf[2:CF_WIDTH + 1, :] = tail_c


def _seq(z, sh1, sh2, sc0, cf0, vfirst, P, li, tt):
    bsz, tlen, _ = z.shape
    first_layer = vfirst is None
    tok = lambda width, blk: pl.BlockSpec((None, tt, width), lambda b, t, blk=blk: (b, t, blk))
    st = lambda rows, width: pl.BlockSpec((None, rows, width), lambda b, t: (b, 0, 0))
    lst = lambda rows, width: pl.BlockSpec((None, None, rows, width), lambda b, t: (li, b, 0, 0))
    par = lambda a: _lpar(a, li)
    params = [P[n] for n in ('mu1', 'mu2', 'w0', 'w2p', 'a0', 'a2p', 'v0', 'v2p', 'g2p',
                              'scw', 'cfbin', 'cfw', 'cfdb', 'cflg', 'cflb')]
    args = [z, z, sh1, sh2, sc0, cf0]
    in_specs = [tok(Z_BLK, 1), tok(Z_BLK, 2), lst(1, Z_BLK), lst(1, LORA_PAD),
                lst(SC_WIDTH - 1, D_SC), lst(CF_WIDTH - 1, D_CF)]
    if not first_layer:
        args.append(vfirst)
        in_specs.append(tok(D_MODEL, 0))
    args += params
    in_specs += [par(a) for a in params]
    f32 = jnp.float32
    out_shape = ([jax.ShapeDtypeStruct((bsz, tlen, D_MODEL), f32)] * 6
                 + [jax.ShapeDtypeStruct((bsz, tlen, D_SC), f32), jax.ShapeDtypeStruct((bsz, tlen, D_CF), f32),
                    jax.ShapeDtypeStruct((bsz, 1, Z_BLK), f32), jax.ShapeDtypeStruct((bsz, 1, LORA_PAD), f32),
                    jax.ShapeDtypeStruct((bsz, SC_WIDTH - 1, D_SC), f32),
                    jax.ShapeDtypeStruct((bsz, CF_WIDTH - 1, D_CF), f32)])
    out_specs = ([tok(D_MODEL, 0)] * 6 + [tok(D_SC, 0), tok(D_CF, 0), st(1, Z_BLK), st(1, LORA_PAD),
                                           st(SC_WIDTH - 1, D_SC), st(CF_WIDTH - 1, D_CF)])
    scratch = [pltpu.VMEM((8 + tt, Z_BLK), f32), pltpu.VMEM((8 + tt, LORA_PAD), f32),
               pltpu.VMEM((8 + tt, D_SC), f32), pltpu.VMEM((CF_WIDTH + 1 + tt + 8, D_CF), f32),
               pltpu.VMEM((tt + 8, D_CF), f32)]
    return pl.pallas_call(
        functools.partial(_seq_kernel, first_layer, tt),
        grid=(bsz, tlen // tt),
        in_specs=in_specs, out_specs=out_specs, out_shape=out_shape, scratch_shapes=scratch,
        compiler_params=_cparams(("arbitrary", "arbitrary")),
        name="seq",
    )(*args)


CHUNK = 8
PAIR = 2 * HEAD_DIM
N_PAIR = HEADS // 2
SEG = 256
P_ROWS = 4 * CHUNK * CHUNK - 2 * CHUNK
R_OFF = [32 * s + 16 if s < CHUNK - 1 else 32 * s for s in range(CHUNK)]


def _segsum(x, ones):
    xb = x.astype(jnp.bfloat16)
    return jnp.concatenate([jnp.dot(xb[:, q * SEG:(q + 1) * SEG], ones, preferred_element_type=jnp.float32)
                            for q in range(D_MODEL // SEG)], axis=1)


def _wkv_kernel(nb, tt, r_ref, ld_ref, k_ref, v_ref, a_ref, s0_ref, kkp_ref, kap_ref, rkp_ref, lng_ref, lnb_ref,
                ones_ref, y_ref, sout_ref, s_scr, kk_scr, km_scr, o_scr, p_scr, e_scr):
    f32, bf16 = jnp.float32, jnp.bfloat16
    ones = ones_ref[...]
    zero64 = jnp.zeros((HEAD_DIM, HEAD_DIM), f32)

    @pl.when(pl.program_id(1) == 0)
    def _():
        for bb in range(nb):
            for p in range(N_PAIR):
                top = jnp.concatenate([s0_ref[bb, 2 * p], zero64], axis=1)
                bot = jnp.concatenate([zero64, s0_ref[bb, 2 * p + 1]], axis=1)
                s_scr[bb, p] = jnp.concatenate([top, bot], axis=0)

    for bb in range(nb):
        kraw = k_ref[bb]
        kkp = kraw * kkp_ref[...]
        nrm = jnp.sqrt(_segsum(kkp * kkp, ones))
        kk_scr[bb] = kkp / jnp.maximum(nrm, 1e-12)
        km_scr[bb] = kraw * (1.0 + (a_ref[bb] - 1.0) * kap_ref[...])

    row = lax.broadcasted_iota(jnp.int32, (CHUNK, D_MODEL), 0)
    head0 = (lax.broadcasted_iota(jnp.int32, (2 * CHUNK, D_MODEL), 1) // HEAD_DIM) % 2 == 0
    bcast = lambda x, s: jnp.broadcast_to(x[s:s + 1, :], (CHUNK, D_MODEL))

    def chunk(c, carry):
        rows = pl.ds(pl.multiple_of(c * CHUNK, CHUNK), CHUNK)
        st = []
        for bb in range(nb):
            r = r_ref[bb, rows, :]
            ld = ld_ref[bb, rows, :]
            v = v_ref[bb, rows, :]
            kk = kk_scr[bb, rows, :]
            cum = ld
            for sh in (1, 2, 4):
                cum = cum + jnp.where(row >= sh, pltpu.roll(cum, sh, axis=0), 0.0)
            w_cum = jnp.exp(cum)
            w_inv = jnp.exp(-cum)
            at = -kk * jnp.exp(cum - ld)
            rt = r * w_cum
            bt = kk * a_ref[bb, rows, :] * w_inv
            kt = km_scr[bb, rows, :] * w_inv
            wc = bcast(w_cum, CHUNK - 1)
            lhs = jnp.concatenate([at, rt], axis=0).astype(bf16)
            st.append([v, at, rt, bt, kt, wc, lhs])
        for bb in range(nb):
            v, at, rt, bt, kt, wc, lhs = st[bb]
            xy = []
            for p in range(N_PAIR):
                sl = slice(p * PAIR, (p + 1) * PAIR)
                xy.append(lax.dot_general(lhs[:, sl], s_scr[bb, p].astype(bf16), (((1,), (1,)), ((), ())),
                                          preferred_element_type=f32))
            st[bb][6] = jnp.concatenate(xy, axis=1)
            for s in range(CHUNK):
                bs, ks = bcast(bt, s), bcast(kt, s)
                rm = jnp.where(row >= s, rt, 0.0) if s else rt
                if s < CHUNK - 1:
                    am = jnp.where(row > s, at, 0.0)
                    p_scr[bb, 32 * s:32 * s + 16, :] = jnp.concatenate([bs * am, ks * am], axis=0).astype(bf16)
                p_scr[bb, R_OFF[s]:R_OFF[s] + 16, :] = jnp.concatenate([bs * rm, ks * rm], axis=0).astype(bf16)
            for q in range(D_MODEL // SEG):
                sl = slice(q * SEG, (q + 1) * SEG)
                e_scr[bb, :, sl] = jnp.dot(p_scr[bb, :, sl], ones, preferred_element_type=f32)
        accs = []
        for bb in range(nb):
            v, at, rt, bt, kt, wc, xy = st[bb]
            pre_u = pre_o = None
            for s in range(CHUNK):
                vs = bcast(v, s)
                to = e_scr[bb, R_OFF[s] + 8:R_OFF[s] + 16, :] * vs
                pre_o = to if pre_o is None else pre_o + to
                if s < CHUNK - 1:
                    tu = e_scr[bb, 32 * s + 8:32 * s + 16, :] * vs
                    pre_u = tu if pre_u is None else pre_u + tu
            acc_u = xy[0:CHUNK] + pre_u
            acc_o = xy[CHUNK:2 * CHUNK] + pre_o
            for s in range(CHUNK):
                us = bcast(acc_u, s)
                if s < CHUNK - 1:
                    acc_u = acc_u + e_scr[bb, 32 * s:32 * s + 8, :] * us
                acc_o = acc_o + e_scr[bb, R_OFF[s]:R_OFF[s] + 8, :] * us
            o_scr[bb, rows, :] = acc_o
            accs.append(acc_u)
        for bb in range(nb):
            v, at, rt, bt, kt, wc, xy = st[bb]
            uv = jnp.concatenate([accs[bb], v], axis=0)
            uv = jnp.concatenate([jnp.where(head0, uv, 0.0), jnp.where(head0, 0.0, uv)], axis=0).astype(bf16)
            bk = jnp.concatenate([bt * wc, kt * wc], axis=0)
            bk = jnp.concatenate([jnp.where(head0, bk, 0.0), jnp.where(head0, 0.0, bk)], axis=0).astype(bf16)
            for p in range(N_PAIR):
                sl = slice(p * PAIR, (p + 1) * PAIR)
                upd = lax.dot_general(uv[:, sl], bk[:, sl], (((0,), (0,)), ((), ())), preferred_element_type=f32)
                s_scr[bb, p] = s_scr[bb, p] * wc[0:1, sl] + upd
        return carry

    lax.fori_loop(0, tt // CHUNK, chunk, 0)

    for bb in range(nb):
        o = o_scr[bb]
        mu = _segsum(o, ones) * (1.0 / HEAD_DIM)
        od = o - mu
        var = _segsum(od * od, ones) * (1.0 / HEAD_DIM)
        y = od * lax.rsqrt(var + GN_EPS) * lng_ref[...] + lnb_ref[...]
        bonus = _segsum(r_ref[bb] * km_scr[bb] * rkp_ref[...], ones)
        y_ref[bb] = y + bonus * v_ref[bb]

    @pl.when(pl.program_id(1) == pl.num_programs(1) - 1)
    def _():
        for bb in range(nb):
            for p in range(N_PAIR):
                sp = s_scr[bb, p]
                sout_ref[bb, 2 * p] = sp[0:HEAD_DIM, 0:HEAD_DIM]
                sout_ref[bb, 2 * p + 1] = sp[HEAD_DIM:PAIR, HEAD_DIM:PAIR]


def _wkv(r, ld, k, v, a, s0, P, li, nb, tt):
    bsz, tlen, _ = r.shape
    f32 = jnp.float32
    tok = pl.BlockSpec((nb, tt, D_MODEL), lambda b, t: (b, t, 0))
    stt = pl.BlockSpec((nb, HEADS, HEAD_DIM, HEAD_DIM), lambda b, t: (b, 0, 0, 0))
    lstt = pl.BlockSpec((None, nb, HEADS, HEAD_DIM, HEAD_DIM), lambda b, t: (li, b, 0, 0, 0))
    par = _lpar(P['kk'], li)
    return pl.pallas_call(
        functools.partial(_wkv_kernel, nb, tt),
        grid=(bsz // nb, tlen // tt),
        in_specs=[tok] * 5 + [lstt] + [par] * 5 + [pl.BlockSpec((SEG, SEG), lambda b, t: (0, 0))],
        out_specs=[tok, stt],
        out_shape=[jax.ShapeDtypeStruct((bsz, tlen, D_MODEL), f32),
                   jax.ShapeDtypeStruct((bsz, HEADS, HEAD_DIM, HEAD_DIM), f32)],
        scratch_shapes=[pltpu.VMEM((nb, N_PAIR, PAIR, PAIR), f32),
                        pltpu.VMEM((nb, tt, D_MODEL), f32), pltpu.VMEM((nb, tt, D_MODEL), f32),
                        pltpu.VMEM((nb, tt, D_MODEL), f32),
                        pltpu.VMEM((nb, P_ROWS, D_MODEL), jnp.bfloat16),
                        pltpu.VMEM((nb, P_ROWS, D_MODEL), f32)],
        compiler_params=_cparams(("arbitrary", "arbitrary")),
        name="wkv",
    )(r, ld, k, v, a, s0, P['kk'], P['ka'], P['rk'], P['lng'], P['lnb'], P['ones'])


LANES = 128


def _wkv_lanes_kernel(tlen, r_ref, ld_ref, k_ref, v_ref, a_ref, s0_ref, kkp_ref, kap_ref, rkp_ref, lng_ref, lnb_ref,
                      y_ref, s_ref, o_scr):
    s_ref[...] = s0_ref[...]
    for t in range(tlen):
        rt = r_ref[t]
        wt = jnp.exp(ld_ref[t])
        kraw = k_ref[t]
        at = a_ref[t]
        kkp = kraw * kkp_ref[...]
        nrm = jnp.sqrt(jnp.sum(kkp * kkp, axis=0, keepdims=True))
        kk = kkp / jnp.maximum(nrm, 1e-12)
        kt = kraw * (1.0 + (at - 1.0) * kap_ref[...])
        bt = kk * at
        nkk = -kk
        for i in range(HEAD_DIM):
            si = s_ref[i]
            sa = jnp.sum(si * nkk, axis=0, keepdims=True)
            sn = si * wt + sa * bt + v_ref[t, i:i + 1, :] * kt
            s_ref[i] = sn
            o_scr[i:i + 1, :] = jnp.sum(sn * rt, axis=0, keepdims=True)
        o = o_scr[...]
        mu = jnp.mean(o, axis=0, keepdims=True)
        od = o - mu
        var = jnp.mean(od * od, axis=0, keepdims=True)
        y = od * lax.rsqrt(var + GN_EPS) * lng_ref[...] + lnb_ref[...]
        bonus = jnp.sum(rt * kt * rkp_ref[...], axis=0, keepdims=True)
        y_ref[t] = y + bonus * v_ref[t]


def _wkv_lanes(r, ld, k, v, a, s0, P, li):
    tlen, _, _, bsz = r.shape
    f32 = jnp.float32
    tok = pl.BlockSpec((tlen, None, HEAD_DIM, bsz), lambda h: (0, h, 0, 0))
    lstt = pl.BlockSpec((None, None, HEAD_DIM, HEAD_DIM, bsz), lambda h: (li, h, 0, 0, 0))
    stt = pl.BlockSpec((None, HEAD_DIM, HEAD_DIM, bsz), lambda h: (h, 0, 0, 0))
    par = pl.BlockSpec((None, None, HEAD_DIM, bsz), lambda h: (li, h, 0, 0))
    hp = P['head_lanes']
    return pl.pallas_call(
        functools.partial(_wkv_lanes_kernel, tlen),
        grid=(HEADS,),
        in_specs=[tok] * 5 + [lstt] + [par] * 5,
        out_specs=[tok, stt],
        out_shape=[jax.ShapeDtypeStruct((tlen, HEADS, HEAD_DIM, bsz), f32),
                   jax.ShapeDtypeStruct((HEADS, HEAD_DIM, HEAD_DIM, bsz), f32)],
        scratch_shapes=[pltpu.VMEM((HEAD_DIM, bsz), f32)],
        compiler_params=_cparams(("arbitrary",)),
        name="wkv_lanes",
    )(r, ld, k, v, a, s0, hp['kk'], hp['ka'], hp['rk'], hp['lng'], hp['lnb'])


def _merge_kernel(h_ref, xa_ref, yw_ref, g_ref, xc_ref, zg_ref, wa_ref, wb_ref, wc_ref, bc_ref, wo_ref, o_ref):
    ya = _bdot(xa_ref[...], wa_ref[...])
    yb = _bdot(yw_ref[...] * g_ref[...], wb_ref[...])
    yc = _bdot(xc_ref[...], wc_ref[...]) + bc_ref[...]
    m = (_sigmoid(zg_ref[:, 0:D_MODEL]) * ya + _sigmoid(zg_ref[:, D_MODEL:2 * D_MODEL]) * yb
         + _sigmoid(zg_ref[:, 2 * D_MODEL:3 * D_MODEL]) * yc)
    o_ref[...] = h_ref[...] + _bdot(m, wo_ref[...])


def _merge(h, xa, yw, g, xc, z, P, li, tm):
    n = h.shape[0]
    tok = lambda width: pl.BlockSpec((tm, width), lambda i: (i, 0))
    par = lambda a: _lpar(a, li)
    ws = [P[n_] for n_ in ('sc_w_out', 'rk_w_out', 'cf_w_out', 'cf_b_out', 'w_o')]
    return pl.pallas_call(
        _merge_kernel,
        grid=(n // tm,),
        in_specs=[tok(D_MODEL), tok(D_SC), tok(D_MODEL), tok(D_MODEL), tok(D_CF), tok(Z_BLK)] + [par(a) for a in ws],
        out_specs=tok(D_MODEL),
        out_shape=jax.ShapeDtypeStruct((n, D_MODEL), jnp.float32),
        compiler_params=_cparams(("arbitrary",)),
        name="merge",
    )(h, xa, yw, g, xc, z, *ws)


def _mlp_ple_kernel(nk, h_ref, p_ref, g2_ref, w1_ref, w2_ref, gp_ref, wg_ref, wp_ref, o_ref, hn_scr, acc_scr):
    kstep = pl.program_id(1)

    @pl.when(kstep == 0)
    def _():
        hn_scr[...] = _rms(h_ref[...], g2_ref[...]).astype(jnp.bfloat16)
        acc_scr[...] = jnp.zeros_like(acc_scr)

    u = jnp.dot(hn_scr[...], w1_ref[...], preferred_element_type=jnp.float32)
    u = jnp.maximum(u, 0.0)
    acc_scr[...] += _bdot(u * u, w2_ref[...])

    @pl.when(kstep == nk - 1)
    def _():
        h1 = h_ref[...] + acc_scr[...]
        gate = _sigmoid(_bdot(_rms(h1, gp_ref[...]), wg_ref[...]))
        o_ref[...] = h1 + gate * _bdot(p_ref[...], wp_ref[...])


def _mlp_ple(h, p, P, li, tm, tf):
    n = h.shape[0]
    nk = D_FF // tf
    par = lambda a: _lpar(a, li)
    return pl.pallas_call(
        functools.partial(_mlp_ple_kernel, nk),
        grid=(n // tm, nk),
        in_specs=[pl.BlockSpec((tm, D_MODEL), lambda i, k: (i, 0)),
                  pl.BlockSpec((None, tm, D_PLE), lambda i, k: (li, i, 0)),
                  par(P['norm2_g']),
                  pl.BlockSpec((None, D_MODEL, tf), lambda i, k: (li, 0, k)),
                  pl.BlockSpec((None, tf, D_MODEL), lambda i, k: (li, k, 0)),
                  par(P['ple_norm_g']), par(P['ple_gate_w']), par(P['ple_w'])],
        out_specs=pl.BlockSpec((tm, D_MODEL), lambda i, k: (i, 0)),
        out_shape=jax.ShapeDtypeStruct((n, D_MODEL), jnp.float32),
        scratch_shapes=[pltpu.VMEM((tm, D_MODEL), jnp.bfloat16), pltpu.VMEM((tm, D_MODEL), jnp.float32)],
        compiler_params=_cparams(("arbitrary", "arbitrary")),
        name="mlp_ple",
    )(h, p, P['norm2_g'], P['mlp_w1'], P['mlp_w2'], P['ple_norm_g'], P['ple_gate_w'], P['ple_w'])


def _final_norm_kernel(x_ref, g_ref, o_ref):
    o_ref[...] = _rms(x_ref[...], g_ref[...])


def _final_norm(x, g, tm):
    n = x.shape[0]
    return pl.pallas_call(
        _final_norm_kernel,
        grid=(n // tm,),
        in_specs=[pl.BlockSpec((tm, D_MODEL), lambda i: (i, 0)), pl.BlockSpec((1, D_MODEL), lambda i: (0, 0))],
        out_specs=pl.BlockSpec((tm, D_MODEL), lambda i: (i, 0)),
        out_shape=jax.ShapeDtypeStruct((n, D_MODEL), jnp.float32),
        compiler_params=_cparams(("arbitrary",)),
        name="final_norm",
    )(x, g)


def _prep(W):
    bf = jnp.bfloat16
    nl = W['w_in'].shape[0]
    row = lambda a: a.reshape(nl, 1, -1)
    w_in = W['w_in']
    n_sc, n_rk, n_cf = 3 * D_SC, RK_PROJ, 2 * D_CF
    o_rk, o_cf, o_gate = n_sc, n_sc + n_rk, n_sc + n_rk + n_cf
    w_perm = jnp.concatenate([
        w_in[:, :, o_gate:o_gate + 3 * D_MODEL], w_in[:, :, o_rk:o_rk + 3 * D_MODEL], w_in[:, :, 0:n_sc],
        w_in[:, :, o_cf:o_cf + n_cf], w_in[:, :, o_rk + 3 * D_MODEL:o_rk + RK_PROJ],
        jnp.zeros((nl, D_MODEL, LORA_PAD - N_LORA), w_in.dtype)], axis=2).astype(bf)
    mu = W['rk_mu']
    zrows = lambda n: jnp.zeros((nl, n, D_MODEL), jnp.float32)
    o_a, o_g = DECAY_LORA, DECAY_LORA + AAA_LORA + MV_LORA
    seg_head = jnp.arange(SEG) // HEAD_DIM
    on_lanes = lambda a: jnp.broadcast_to(a.reshape(nl, HEADS, HEAD_DIM, 1), (nl, HEADS, HEAD_DIM, LANES))
    return {
        'head_lanes': {'kk': on_lanes(W['rk_k_k']), 'ka': on_lanes(W['rk_k_a']), 'rk': on_lanes(W['rk_r_k']),
                       'lng': on_lanes(W['rk_ln_g']), 'lnb': on_lanes(W['rk_ln_b'])},
        'ones': (seg_head[:, None] == seg_head[None, :]).astype(bf),
        'kk': row(W['rk_k_k']), 'ka': row(W['rk_k_a']), 'rk': row(W['rk_r_k']),
        'lng': row(W['rk_ln_g']), 'lnb': row(W['rk_ln_b']),
        'norm1_g': row(W['norm1_g']), 'w_in': w_perm,
        'mu1': row(mu[:, :3 * D_MODEL]),
        'mu2': row(jnp.pad(mu[:, 3 * D_MODEL:], ((0, 0), (0, LORA_PAD - N_LORA)))),
        'w0': row(W['rk_w0']), 'a0': row(W['rk_a0']), 'v0': row(W['rk_v0']),
        'w2p': jnp.concatenate([W['rk_w2'], zrows(128 - o_a)], axis=1).astype(bf),
        'a2p': jnp.concatenate([zrows(o_a), W['rk_a2']], axis=1).astype(bf),
        'v2p': jnp.concatenate([W['rk_v2'], zrows(128 - MV_LORA)], axis=1).astype(bf),
        'g2p': jnp.concatenate([zrows(o_g - 128), W['rk_g2'], zrows(384 - N_LORA)], axis=1).astype(bf),
        'scw': W['sc_conv_w'], 'cfbin': row(W['cf_b_in']), 'cfw': W['cf_dw_w'],
        'cfdb': row(W['cf_dw_b']), 'cflg': row(W['cf_ln_g']), 'cflb': row(W['cf_ln_b']),
        'sc_w_out': W['sc_w_out'].astype(bf), 'rk_w_out': W['rk_w_out'].astype(bf),
        'cf_w_out': W['cf_w_out'].astype(bf), 'cf_b_out': row(W['cf_b_out']), 'w_o': W['w_o'].astype(bf),
        'norm2_g': row(W['norm2_g']), 'mlp_w1': W['mlp_w1'].astype(bf), 'mlp_w2': W['mlp_w2'].astype(bf),
        'ple_norm_g': row(W['ple_norm_g']), 'ple_gate_w': W['ple_gate_w'].astype(bf),
        'ple_w': W['ple_w'].astype(bf),
    }


def _pick(n, pref):
    return pref if n % pref == 0 else n


def _trunk(x, p, sc0, shift0, wkv0, cf0, P, final_g, cfg):
    bsz, tlen, _ = x.shape
    n = bsz * tlen
    tm = _pick(n, cfg['tm'])
    tt = _pick(tlen, cfg['tt'])
    tpad = -tlen % CHUNK
    tw = _pick(tlen + tpad, cfg['tw'])
    pad_t = lambda a: jnp.pad(a, ((0, 0), (0, tpad), (0, 0))) if tpad else a
    h = x.reshape(n, D_MODEL)
    pf = p.reshape(DEPTH, n, D_PLE)
    sh1 = shift0[:, :, None, :3 * D_MODEL]
    sh2 = jnp.pad(shift0[:, :, 3 * D_MODEL:], ((0, 0), (0, 0), (0, LORA_PAD - N_LORA)))[:, :, None, :]
    lanes_form = bsz == LANES and tlen <= CHUNK
    to_lanes = lambda a: a.reshape(bsz, tlen, HEADS, HEAD_DIM).transpose(1, 2, 3, 0)
    wkv0_l = wkv0.transpose(0, 2, 3, 4, 1) if lanes_form else None
    v_first = None
    new_sc, new_shift, new_wkv, new_cf = [], [], [], []
    for i in range(DEPTH):
        z = _in_proj(h, P, i, tm)
        (r, w, k, v, a, g, xa, xc, nsh1, nsh2, nsc, ncf) = _seq(
            z.reshape(bsz, tlen, NZ), sh1, sh2, sc0, cf0, v_first, P, i, tt)
        if i == 0:
            v_first = v
        if lanes_form:
            y, s_out = _wkv_lanes(to_lanes(r), to_lanes(w), to_lanes(k), to_lanes(v), to_lanes(a), wkv0_l, P, i)
            yw = y.transpose(3, 0, 1, 2).reshape(n, D_MODEL)
        else:
            y, s_out = _wkv(pad_t(r), pad_t(w), pad_t(k), pad_t(v), pad_t(a), wkv0, P, i, cfg['nb'], tw)
            yw = y[:, :tlen].reshape(n, D_MODEL)
        h = _merge(h, xa.reshape(n, D_SC), yw, g.reshape(n, D_MODEL), xc.reshape(n, D_CF), z, P, i, tm)
        h = _mlp_ple(h, pf, P, i, tm, cfg['tf'])
        new_sc.append(nsc)
        new_shift.append(jnp.concatenate([nsh1[:, 0, :], nsh2[:, 0, :N_LORA]], axis=-1))
        new_wkv.append(s_out)
        new_cf.append(ncf)
    y = _final_norm(h, final_g, tm).reshape(bsz, tlen, D_MODEL)
    wkv_out = jnp.stack(new_wkv)
    if lanes_form:
        wkv_out = wkv_out.transpose(0, 4, 1, 2, 3)
    return y, jnp.stack(new_sc), jnp.stack(new_shift), wkv_out, jnp.stack(new_cf)


PROMPT_CFG = {'tm': 512, 'tt': 128, 'tw': 128, 'nb': 4, 'tf': 2048}
SAMPLE_CFG = {'tm': 512, 'tt': 4, 'tw': 8, 'nb': 8, 'tf': 1024}


def kernel(x_prompt, x_sample, p_prompt, p_sample, state_sconv, state_shift, state_wkv, state_cconv,
           norm1_g, w_in, sc_conv_w, sc_w_out, rk_mu, rk_w0, rk_w2, rk_a0, rk_a2, rk_v0, rk_v2, rk_g2,
           rk_k_k, rk_k_a, rk_r_k, rk_ln_g, rk_ln_b, rk_w_out, cf_b_in, cf_dw_w, cf_dw_b, cf_ln_g, cf_ln_b,
           cf_w_out, cf_b_out, w_o, norm2_g, mlp_w1, mlp_w2, ple_w, ple_gate_w, ple_norm_g, final_norm_g):
    W = dict(norm1_g=norm1_g, w_in=w_in, sc_conv_w=sc_conv_w, sc_w_out=sc_w_out, rk_mu=rk_mu, rk_w0=rk_w0,
             rk_w2=rk_w2, rk_a0=rk_a0, rk_a2=rk_a2, rk_v0=rk_v0, rk_v2=rk_v2, rk_g2=rk_g2, rk_k_k=rk_k_k,
             rk_k_a=rk_k_a, rk_r_k=rk_r_k, rk_ln_g=rk_ln_g, rk_ln_b=rk_ln_b, rk_w_out=rk_w_out, cf_b_in=cf_b_in,
             cf_dw_w=cf_dw_w, cf_dw_b=cf_dw_b, cf_ln_g=cf_ln_g, cf_ln_b=cf_ln_b, cf_w_out=cf_w_out,
             cf_b_out=cf_b_out, w_o=w_o, norm2_g=norm2_g, mlp_w1=mlp_w1, mlp_w2=mlp_w2, ple_w=ple_w,
             ple_gate_w=ple_gate_w, ple_norm_g=ple_norm_g)
    P = _prep(W)
    final_g = final_norm_g.reshape(1, D_MODEL)
    bp = x_prompt.shape[0]
    f32 = jnp.float32
    zsc = jnp.zeros((DEPTH, bp, SC_WIDTH - 1, D_SC), f32)
    zsh = jnp.zeros((DEPTH, bp, RK_PROJ), f32)
    zwkv = jnp.zeros((DEPTH, bp, HEADS, HEAD_DIM, HEAD_DIM), f32)
    zcf = jnp.zeros((DEPTH, bp, CF_WIDTH - 1, D_CF), f32)
    y_p, sc_p, sh_p, wkv_p, cf_p = _trunk(x_prompt, p_prompt, zsc, zsh, zwkv, zcf, P, final_g, PROMPT_CFG)
    y_s, sc_s, sh_s, wkv_s, cf_s = _trunk(x_sample, p_sample, state_sconv, state_shift, state_wkv, state_cconv,
                                          P, final_g, SAMPLE_CFG)
    return (y_p, y_s, sc_p, sh_p, wkv_p, cf_p, sc_s, sh_s, wkv_s, cf_s)
```

```python
import functools

import jax
import jax.numpy as jnp
from jax import lax
from jax.experimental import pallas as pl
from jax.experimental.pallas import tpu as pltpu

D_MODEL = 1024
DEPTH = 4
D_SC = 512
D_CF = 512
SC_WIDTH = 3
CF_WIDTH = 31
HEAD_DIM = 64
HEADS = 16
D_FF = 4096
D_PLE = 256
DECAY_LORA, AAA_LORA, MV_LORA, GATE_LORA = 64, 64, 32, 128
N_LORA = DECAY_LORA + AAA_LORA + MV_LORA + GATE_LORA
RK_PROJ = 3 * D_MODEL + N_LORA
RMS_EPS = 1e-6
LN_EPS = 1e-5
GN_EPS = 64e-5
EXP_M05 = 0.6065306597126334

Z_BLK = 3 * D_MODEL
NZ = 3 * Z_BLK
Z2_SC = 0
Z2_CF = 3 * D_SC
Z2_LORA = Z2_CF + 2 * D_CF
LORA_PAD = Z_BLK - Z2_LORA

VMEM_LIMIT_BYTES = 56 * 1024 * 1024


def _cparams(sem):
    return pltpu.CompilerParams(dimension_semantics=sem, vmem_limit_bytes=VMEM_LIMIT_BYTES)


def _sigmoid(x):
    return 1.0 / (1.0 + jnp.exp(-x))


def _rms(x, g):
    return x * lax.rsqrt(jnp.mean(x * x, axis=-1, keepdims=True) + RMS_EPS) * g


def _bdot(a, b):
    return jnp.dot(a.astype(jnp.bfloat16), b, preferred_element_type=jnp.float32)


def _in_proj_kernel(x_ref, g_ref, w_ref, o_ref):
    o_ref[...] = _bdot(_rms(x_ref[...], g_ref[...]), w_ref[...])


def _lpar(a, li):
    return pl.BlockSpec((None,) + a.shape[1:], lambda *_: (li,) + (0,) * (a.ndim - 1))


def _in_proj(x, P, li, tm):
    n = x.shape[0]
    g, w = P['norm1_g'], P['w_in']
    return pl.pallas_call(
        _in_proj_kernel,
        grid=(NZ // Z_BLK, n // tm),
        in_specs=[
            pl.BlockSpec((tm, D_MODEL), lambda j, i: (i, 0)),
            _lpar(g, li),
            pl.BlockSpec((None, D_MODEL, Z_BLK), lambda j, i: (li, 0, j)),
        ],
        out_specs=pl.BlockSpec((tm, Z_BLK), lambda j, i: (i, j)),
        out_shape=jax.ShapeDtypeStruct((n, NZ), jnp.float32),
        compiler_params=_cparams(("arbitrary", "arbitrary")),
        name="in_proj",
    )(x, g, w)


def _seq_kernel(first_layer, tt, *refs):
    it = iter(refs)
    z1_ref, z2_ref, sh1_ref, sh2_ref, sc0_ref, cf0_ref = (next(it) for _ in range(6))
    vf_ref = None if first_layer else next(it)
    (mu1_ref, mu2_ref, w0_ref, w2_ref, a0_ref, a2_ref, v0_ref, v2_ref, g2_ref,
     scw_ref, cfbin_ref, cfw_ref, cfdb_ref, cflg_ref, cflb_ref) = (next(it) for _ in range(15))
    (r_ref, w_ref, k_ref, v_ref, a_ref, g_ref, xa_ref, xc_ref,
     nsh1_ref, nsh2_ref, nsc_ref, ncf_ref) = (next(it) for _ in range(12))
    s1buf, s2buf, cbuf, gbuf, pbuf = (next(it) for _ in range(5))

    @pl.when(pl.program_id(1) == 0)
    def _():
        s1buf[7:8, :] = sh1_ref[...]
        s2buf[7:8, :] = sh2_ref[...]
        cbuf[6:8, :] = sc0_ref[...]
        gbuf[2:CF_WIDTH + 1, :] = cf0_ref[...]
        gbuf[0:2, :] = jnp.zeros((2, D_CF), jnp.float32)
        gbuf[CF_WIDTH + 1 + tt:CF_WIDTH + 1 + tt + 8, :] = jnp.zeros((8, D_CF), jnp.float32)

    z1 = z1_ref[...]
    s1buf[8:8 + tt, :] = z1
    zm1 = z1 + (s1buf[7:7 + tt, :] - z1) * mu1_ref[...]
    z2l = z2_ref[:, Z2_LORA:Z_BLK]
    s2buf[8:8 + tt, :] = z2l
    zm2 = z2l + (s2buf[7:7 + tt, :] - z2l) * mu2_ref[...]
    last1 = s1buf[7 + tt:8 + tt, :]
    last2 = s2buf[7 + tt:8 + tt, :]
    nsh1_ref[...] = last1
    nsh2_ref[...] = last2
    s1buf[7:8, :] = last1
    s2buf[7:8, :] = last2

    x01 = zm2[:, 0:128]
    lw = _bdot(jnp.tanh(x01), w2_ref[...])
    la = _bdot(x01, a2_ref[...])
    w_ref[...] = -EXP_M05 * _sigmoid(w0_ref[...] + lw)
    a_ref[...] = _sigmoid(a0_ref[...] + la)
    g_ref[...] = _bdot(_sigmoid(zm2[:, 128:384]), g2_ref[...])
    r_ref[...] = zm1[:, 0:D_MODEL]
    k_ref[...] = zm1[:, D_MODEL:2 * D_MODEL]
    v = zm1[:, 2 * D_MODEL:3 * D_MODEL]
    if not first_layer:
        lv = _bdot(zm2[:, 128:256], v2_ref[...])
        v = v + (vf_ref[...] - v) * _sigmoid(v0_ref[...] + lv)
    v_ref[...] = v

    cu = z2_ref[:, Z2_SC + D_SC:Z2_SC + 2 * D_SC] * z2_ref[:, Z2_SC + 2 * D_SC:Z2_SC + 3 * D_SC]
    cbuf[8:8 + tt, :] = cu
    conv_a = (scw_ref[0:1, :] * cbuf[6:6 + tt, :] + scw_ref[1:2, :] * cbuf[7:7 + tt, :]
              + scw_ref[2:3, :] * cu)
    xa_ref[...] = z2_ref[:, Z2_SC:Z2_SC + D_SC] * conv_a
    tail_a = cbuf[6 + tt:8 + tt, :]
    nsc_ref[...] = tail_a
    cbuf[6:8, :] = tail_a

    zc = z2_ref[:, Z2_CF:Z2_CF + 2 * D_CF] + cfbin_ref[...]
    glu = zc[:, 0:D_CF] * _sigmoid(zc[:, D_CF:2 * D_CF])
    gbuf[CF_WIDTH + 1:CF_WIDTH + 1 + tt, :] = glu
    if tt % 8 == 0:
        acc = None
        for ph in range(8):
            part = None
            for j in range(ph, CF_WIDTH + 2, 8):
                if j < 2:
                    continue
                term = cfw_ref[j - 2:j - 1, :] * gbuf[j - ph:j - ph + tt + 8, :]
                part = term if part is None else part + term
            pbuf[...] = part
            sh = pbuf[ph:ph + tt, :]
            acc = sh if acc is None else acc + sh
    else:
        acc = cfw_ref[0:1, :] * gbuf[2:2 + tt, :]
        for kk in range(1, CF_WIDTH):
            acc = acc + cfw_ref[kk:kk + 1, :] * gbuf[2 + kk:2 + kk + tt, :]
    cc = acc + cfdb_ref[...]
    mu = jnp.mean(cc, axis=-1, keepdims=True)
    cd = cc - mu
    var = jnp.mean(cd * cd, axis=-1, keepdims=True)
    ln = cd * lax.rsqrt(var + LN_EPS) * cflg_ref[...] + cflb_ref[...]
    xc_ref[...] = ln * _sigmoid(ln)
    tail_c = gbuf[2 + tt:CF_WIDTH + 1 + tt, :]
    ncf_ref[...] = tail_c
    gbuf[2:CF_WIDTH + 1, :] = tail_c


def _seq(z, sh1, sh2, sc0, cf0, vfirst, P, li, tt):
    bsz, tlen, _ = z.shape
    first_layer = vfirst is None
    tok = lambda width, blk: pl.BlockSpec((None, tt, width), lambda b, t, blk=blk: (b, t, blk))
    st = lambda rows, width: pl.BlockSpec((None, rows, width), lambda b, t: (b, 0, 0))
    lst = lambda rows, width: pl.BlockSpec((None, None, rows, width), lambda b, t: (li, b, 0, 0))
    par = lambda a: _lpar(a, li)
    params = [P[n] for n in ('mu1', 'mu2', 'w0', 'w2p', 'a0', 'a2p', 'v0', 'v2p', 'g2p',
                              'scw', 'cfbin', 'cfw', 'cfdb', 'cflg', 'cflb')]
    args = [z, z, sh1, sh2, sc0, cf0]
    in_specs = [tok(Z_BLK, 1), tok(Z_BLK, 2), lst(1, Z_BLK), lst(1, LORA_PAD),
                lst(SC_WIDTH - 1, D_SC), lst(CF_WIDTH - 1, D_CF)]
    if not first_layer:
        args.append(vfirst)
        in_specs.append(tok(D_MODEL, 0))
    args += params
    in_specs += [par(a) for a in params]
    f32 = jnp.float32
    out_shape = ([jax.ShapeDtypeStruct((bsz, tlen, D_MODEL), f32)] * 6
                 + [jax.ShapeDtypeStruct((bsz, tlen, D_SC), f32), jax.ShapeDtypeStruct((bsz, tlen, D_CF), f32),
                    jax.ShapeDtypeStruct((bsz, 1, Z_BLK), f32), jax.ShapeDtypeStruct((bsz, 1, LORA_PAD), f32),
                    jax.ShapeDtypeStruct((bsz, SC_WIDTH - 1, D_SC), f32),
                    jax.ShapeDtypeStruct((bsz, CF_WIDTH - 1, D_CF), f32)])
    out_specs = ([tok(D_MODEL, 0)] * 6 + [tok(D_SC, 0), tok(D_CF, 0), st(1, Z_BLK), st(1, LORA_PAD),
                                           st(SC_WIDTH - 1, D_SC), st(CF_WIDTH - 1, D_CF)])
    scratch = [pltpu.VMEM((8 + tt, Z_BLK), f32), pltpu.VMEM((8 + tt, LORA_PAD), f32),
               pltpu.VMEM((8 + tt, D_SC), f32), pltpu.VMEM((CF_WIDTH + 1 + tt + 8, D_CF), f32),
               pltpu.VMEM((tt + 8, D_CF), f32)]
    return pl.pallas_call(
        functools.partial(_seq_kernel, first_layer, tt),
        grid=(bsz, tlen // tt),
        in_specs=in_specs, out_specs=out_specs, out_shape=out_shape, scratch_shapes=scratch,
        compiler_params=_cparams(("arbitrary", "arbitrary")),
        name="seq",
    )(*args)


CHUNK = 8
PAIR = 2 * HEAD_DIM
N_PAIR = HEADS // 2
SEG = 256
P_ROWS = 4 * CHUNK * CHUNK - 2 * CHUNK
R_OFF = [32 * s + 16 if s < CHUNK - 1 else 32 * s for s in range(CHUNK)]


def _segsum(x, ones):
    xb = x.astype(jnp.bfloat16)
    return jnp.concatenate([jnp.dot(xb[:, q * SEG:(q + 1) * SEG], ones, preferred_element_type=jnp.float32)
                            for q in range(D_MODEL // SEG)], axis=1)


def _wkv_kernel(nb, tt, r_ref, ld_ref, k_ref, v_ref, a_ref, s0_ref, kkp_ref, kap_ref, rkp_ref, lng_ref, lnb_ref,
                ones_ref, y_ref, sout_ref, s_scr, kk_scr, km_scr, o_scr, p_scr, e_scr):
    f32, bf16 = jnp.float32, jnp.bfloat16
    ones = ones_ref[...]

    @pl.when(pl.program_id(1) == 0)
    def _():
        for bb in range(nb):
            for p in range(N_PAIR):
                s_scr[bb, p] = jnp.concatenate([s0_ref[bb, 2 * p], s0_ref[bb, 2 * p + 1]], axis=1)

    for bb in range(nb):
        kraw = k_ref[bb]
        kkp = kraw * kkp_ref[...]
        nrm = jnp.sqrt(_segsum(kkp * kkp, ones))
        kk_scr[bb] = kkp / jnp.maximum(nrm, 1e-12)
        km_scr[bb] = kraw * (1.0 + (a_ref[bb] - 1.0) * kap_ref[...])

    row = lax.broadcasted_iota(jnp.int32, (CHUNK, D_MODEL), 0)
    head0 = (lax.broadcasted_iota(jnp.int32, (2 * CHUNK, D_MODEL), 1) // HEAD_DIM) % 2 == 0
    bcast = lambda x, s: jnp.broadcast_to(x[s:s + 1, :], (CHUNK, D_MODEL))

    def chunk(c, carry):
        rows = pl.ds(pl.multiple_of(c * CHUNK, CHUNK), CHUNK)
        st = []
        for bb in range(nb):
            r = r_ref[bb, rows, :]
            ld = ld_ref[bb, rows, :]
            v = v_ref[bb, rows, :]
            kk = kk_scr[bb, rows, :]
            cum = ld
            for sh in (1, 2, 4):
                cum = cum + jnp.where(row >= sh, pltpu.roll(cum, sh, axis=0), 0.0)
            w_cum = jnp.exp(cum)
            w_inv = jnp.exp(-cum)
            at = -kk * jnp.exp(cum - ld)
            rt = r * w_cum
            bt = kk * a_ref[bb, rows, :] * w_inv
            kt = km_scr[bb, rows, :] * w_inv
            wc = bcast(w_cum, CHUNK - 1)
            lhs = jnp.concatenate([at, rt], axis=0)
            lhs = jnp.concatenate([jnp.where(head0, lhs, 0.0), jnp.where(head0, 0.0, lhs)], axis=0).astype(bf16)
            st.append([v, at, rt, bt, kt, wc, lhs])
        for bb in range(nb):
            v, at, rt, bt, kt, wc, lhs = st[bb]
            xy = []
            for p in range(N_PAIR):
                sl = slice(p * PAIR, (p + 1) * PAIR)
                xh = lax.dot_general(lhs[:, sl], s_scr[bb, p].astype(bf16), (((1,), (1,)), ((), ())),
                                     preferred_element_type=f32)
                xy.append(xh[0:2 * CHUNK])
                xy.append(xh[2 * CHUNK:4 * CHUNK])
            st[bb][6] = jnp.concatenate(xy, axis=1)
            for s in range(CHUNK):
                bs, ks = bcast(bt, s), bcast(kt, s)
                rm = jnp.where(row >= s, rt, 0.0) if s else rt
                if s < CHUNK - 1:
                    am = jnp.where(row > s, at, 0.0)
                    p_scr[bb, 32 * s:32 * s + 16, :] = jnp.concatenate([bs * am, ks * am], axis=0).astype(bf16)
                p_scr[bb, R_OFF[s]:R_OFF[s] + 16, :] = jnp.concatenate([bs * rm, ks * rm], axis=0).astype(bf16)
            for q in range(D_MODEL // SEG):
                sl = slice(q * SEG, (q + 1) * SEG)
                e_scr[bb, :, sl] = jnp.dot(p_scr[bb, :, sl], ones, preferred_element_type=f32)
        accs = []
        for bb in range(nb):
            v, at, rt, bt, kt, wc, xy = st[bb]
            pre_u = pre_o = None
            for s in range(CHUNK):
                vs = bcast(v, s)
                to = e_scr[bb, R_OFF[s] + 8:R_OFF[s] + 16, :] * vs
                pre_o = to if pre_o is None else pre_o + to
                if s < CHUNK - 1:
                    tu = e_scr[bb, 32 * s + 8:32 * s + 16, :] * vs
                    pre_u = tu if pre_u is None else pre_u + tu
            acc_u = xy[0:CHUNK] + pre_u
            acc_o = xy[CHUNK:2 * CHUNK] + pre_o
            for s in range(CHUNK):
                us = bcast(acc_u, s)
                if s < CHUNK - 1:
                    acc_u = acc_u + e_scr[bb, 32 * s:32 * s + 8, :] * us
                acc_o = acc_o + e_scr[bb, R_OFF[s]:R_OFF[s] + 8, :] * us
            o_scr[bb, rows, :] = acc_o
            accs.append(acc_u)
        for bb in range(nb):
            v, at, rt, bt, kt, wc, xy = st[bb]
            uv = jnp.concatenate([accs[bb], v], axis=0)
            uv_swapped = pltpu.roll(uv, D_MODEL - HEAD_DIM, axis=1)
            uv = jnp.concatenate([jnp.where(head0, uv, 0.0), jnp.where(head0, uv_swapped, 0.0)], axis=0).astype(bf16)
            bk = jnp.concatenate([bt * wc, kt * wc], axis=0)
            bk = jnp.concatenate([jnp.where(head0, bk, 0.0), jnp.where(head0, 0.0, bk)], axis=0).astype(bf16)
            for p in range(N_PAIR):
                sl = slice(p * PAIR, (p + 1) * PAIR)
                upd = lax.dot_general(uv[:, p * PAIR:p * PAIR + HEAD_DIM], bk[:, sl], (((0,), (0,)), ((), ())),
                                      preferred_element_type=f32)
                s_scr[bb, p] = s_scr[bb, p] * wc[0:1, sl] + upd
        return carry

    lax.fori_loop(0, tt // CHUNK, chunk, 0)

    for bb in range(nb):
        o = o_scr[bb]
        mu = _segsum(o, ones) * (1.0 / HEAD_DIM)
        od = o - mu
        var = _segsum(od * od, ones) * (1.0 / HEAD_DIM)
        y = od * lax.rsqrt(var + GN_EPS) * lng_ref[...] + lnb_ref[...]
        bonus = _segsum(r_ref[bb] * km_scr[bb] * rkp_ref[...], ones)
        y_ref[bb] = y + bonus * v_ref[bb]

    @pl.when(pl.program_id(1) == pl.num_programs(1) - 1)
    def _():
        for bb in range(nb):
            for p in range(N_PAIR):
                sp = s_scr[bb, p]
                sout_ref[bb, 2 * p] = sp[:, 0:HEAD_DIM]
                sout_ref[bb, 2 * p + 1] = sp[:, HEAD_DIM:PAIR]


def _wkv(r, ld, k, v, a, s0, P, li, nb, tt):
    bsz, tlen, _ = r.shape
    f32 = jnp.float32
    tok = pl.BlockSpec((nb, tt, D_MODEL), lambda b, t: (b, t, 0))
    stt = pl.BlockSpec((nb, HEADS, HEAD_DIM, HEAD_DIM), lambda b, t: (b, 0, 0, 0))
    lstt = pl.BlockSpec((None, nb, HEADS, HEAD_DIM, HEAD_DIM), lambda b, t: (li, b, 0, 0, 0))
    par = _lpar(P['kk'], li)
    return pl.pallas_call(
        functools.partial(_wkv_kernel, nb, tt),
        grid=(bsz // nb, tlen // tt),
        in_specs=[tok] * 5 + [lstt] + [par] * 5 + [pl.BlockSpec((SEG, SEG), lambda b, t: (0, 0))],
        out_specs=[tok, stt],
        out_shape=[jax.ShapeDtypeStruct((bsz, tlen, D_MODEL), f32),
                   jax.ShapeDtypeStruct((bsz, HEADS, HEAD_DIM, HEAD_DIM), f32)],
        scratch_shapes=[pltpu.VMEM((nb, N_PAIR, HEAD_DIM, PAIR), f32),
                        pltpu.VMEM((nb, tt, D_MODEL), f32), pltpu.VMEM((nb, tt, D_MODEL), f32),
                        pltpu.VMEM((nb, tt, D_MODEL), f32),
                        pltpu.VMEM((nb, P_ROWS, D_MODEL), jnp.bfloat16),
                        pltpu.VMEM((nb, P_ROWS, D_MODEL), f32)],
        compiler_params=_cparams(("arbitrary", "arbitrary")),
        name="wkv",
    )(r, ld, k, v, a, s0, P['kk'], P['ka'], P['rk'], P['lng'], P['lnb'], P['ones'])


LANES = 128


def _wkv_lanes_kernel(tlen, r_ref, ld_ref, k_ref, v_ref, a_ref, s0_ref, kkp_ref, kap_ref, rkp_ref, lng_ref, lnb_ref,
                      y_ref, s_ref, o_scr):
    s_ref[...] = s0_ref[...]
    for t in range(tlen):
        rt = r_ref[t]
        wt = jnp.exp(ld_ref[t])
        kraw = k_ref[t]
        at = a_ref[t]
        kkp = kraw * kkp_ref[...]
        nrm = jnp.sqrt(jnp.sum(kkp * kkp, axis=0, keepdims=True))
        kk = kkp / jnp.maximum(nrm, 1e-12)
        kt = kraw * (1.0 + (at - 1.0) * kap_ref[...])
        bt = kk * at
        nkk = -kk
        for i in range(HEAD_DIM):
            si = s_ref[i]
            sa = jnp.sum(si * nkk, axis=0, keepdims=True)
            sn = si * wt + sa * bt + v_ref[t, i:i + 1, :] * kt
            s_ref[i] = sn
            o_scr[i:i + 1, :] = jnp.sum(sn * rt, axis=0, keepdims=True)
        o = o_scr[...]
        mu = jnp.mean(o, axis=0, keepdims=True)
        od = o - mu
        var = jnp.mean(od * od, axis=0, keepdims=True)
        y = od * lax.rsqrt(var + GN_EPS) * lng_ref[...] + lnb_ref[...]
        bonus = jnp.sum(rt * kt * rkp_ref[...], axis=0, keepdims=True)
        y_ref[t] = y + bonus * v_ref[t]


def _wkv_lanes(r, ld, k, v, a, s0, P, li):
    tlen, _, _, bsz = r.shape
    f32 = jnp.float32
    tok = pl.BlockSpec((tlen, None, HEAD_DIM, bsz), lambda h: (0, h, 0, 0))
    lstt = pl.BlockSpec((None, None, HEAD_DIM, HEAD_DIM, bsz), lambda h: (li, h, 0, 0, 0))
    stt = pl.BlockSpec((None, HEAD_DIM, HEAD_DIM, bsz), lambda h: (h, 0, 0, 0))
    par = pl.BlockSpec((None, None, HEAD_DIM, bsz), lambda h: (li, h, 0, 0))
    hp = P['head_lanes']
    return pl.pallas_call(
        functools.partial(_wkv_lanes_kernel, tlen),
        grid=(HEADS,),
        in_specs=[tok] * 5 + [lstt] + [par] * 5,
        out_specs=[tok, stt],
        out_shape=[jax.ShapeDtypeStruct((tlen, HEADS, HEAD_DIM, bsz), f32),
                   jax.ShapeDtypeStruct((HEADS, HEAD_DIM, HEAD_DIM, bsz), f32)],
        scratch_shapes=[pltpu.VMEM((HEAD_DIM, bsz), f32)],
        compiler_params=_cparams(("arbitrary",)),
        name="wkv_lanes",
    )(r, ld, k, v, a, s0, hp['kk'], hp['ka'], hp['rk'], hp['lng'], hp['lnb'])


def _merge_kernel(h_ref, xa_ref, yw_ref, g_ref, xc_ref, zg_ref, wa_ref, wb_ref, wc_ref, bc_ref, wo_ref, o_ref):
    ya = _bdot(xa_ref[...], wa_ref[...])
    yb = _bdot(yw_ref[...] * g_ref[...], wb_ref[...])
    yc = _bdot(xc_ref[...], wc_ref[...]) + bc_ref[...]
    m = (_sigmoid(zg_ref[:, 0:D_MODEL]) * ya + _sigmoid(zg_ref[:, D_MODEL:2 * D_MODEL]) * yb
         + _sigmoid(zg_ref[:, 2 * D_MODEL:3 * D_MODEL]) * yc)
    o_ref[...] = h_ref[...] + _bdot(m, wo_ref[...])


def _merge(h, xa, yw, g, xc, z, P, li, tm):
    n = h.shape[0]
    tok = lambda width: pl.BlockSpec((tm, width), lambda i: (i, 0))
    par = lambda a: _lpar(a, li)
    ws = [P[n_] for n_ in ('sc_w_out', 'rk_w_out', 'cf_w_out', 'cf_b_out', 'w_o')]
    return pl.pallas_call(
        _merge_kernel,
        grid=(n // tm,),
        in_specs=[tok(D_MODEL), tok(D_SC), tok(D_MODEL), tok(D_MODEL), tok(D_CF), tok(Z_BLK)] + [par(a) for a in ws],
        out_specs=tok(D_MODEL),
        out_shape=jax.ShapeDtypeStruct((n, D_MODEL), jnp.float32),
        compiler_params=_cparams(("arbitrary",)),
        name="merge",
    )(h, xa, yw, g, xc, z, *ws)


def _mlp_ple_kernel(nk, h_ref, p_ref, g2_ref, w1_ref, w2_ref, gp_ref, wg_ref, wp_ref, o_ref, hn_scr, acc_scr):
    kstep = pl.program_id(1)

    @pl.when(kstep == 0)
    def _():
        hn_scr[...] = _rms(h_ref[...], g2_ref[...]).astype(jnp.bfloat16)
        acc_scr[...] = jnp.zeros_like(acc_scr)

    u = jnp.dot(hn_scr[...], w1_ref[...], preferred_element_type=jnp.float32)
    u = jnp.maximum(u, 0.0)
    acc_scr[...] += _bdot(u * u, w2_ref[...])

    @pl.when(kstep == nk - 1)
    def _():
        h1 = h_ref[...] + acc_scr[...]
        gate = _sigmoid(_bdot(_rms(h1, gp_ref[...]), wg_ref[...]))
        o_ref[...] = h1 + gate * _bdot(p_ref[...], wp_ref[...])


def _mlp_ple(h, p, P, li, tm, tf):
    n = h.shape[0]
    nk = D_FF // tf
    par = lambda a: _lpar(a, li)
    return pl.pallas_call(
        functools.partial(_mlp_ple_kernel, nk),
        grid=(n // tm, nk),
        in_specs=[pl.BlockSpec((tm, D_MODEL), lambda i, k: (i, 0)),
                  pl.BlockSpec((None, tm, D_PLE), lambda i, k: (li, i, 0)),
                  par(P['norm2_g']),
                  pl.BlockSpec((None, D_MODEL, tf), lambda i, k: (li, 0, k)),
                  pl.BlockSpec((None, tf, D_MODEL), lambda i, k: (li, k, 0)),
                  par(P['ple_norm_g']), par(P['ple_gate_w']), par(P['ple_w'])],
        out_specs=pl.BlockSpec((tm, D_MODEL), lambda i, k: (i, 0)),
        out_shape=jax.ShapeDtypeStruct((n, D_MODEL), jnp.float32),
        scratch_shapes=[pltpu.VMEM((tm, D_MODEL), jnp.bfloat16), pltpu.VMEM((tm, D_MODEL), jnp.float32)],
        compiler_params=_cparams(("arbitrary", "arbitrary")),
        name="mlp_ple",
    )(h, p, P['norm2_g'], P['mlp_w1'], P['mlp_w2'], P['ple_norm_g'], P['ple_gate_w'], P['ple_w'])


def _final_norm_kernel(x_ref, g_ref, o_ref):
    o_ref[...] = _rms(x_ref[...], g_ref[...])


def _final_norm(x, g, tm):
    n = x.shape[0]
    return pl.pallas_call(
        _final_norm_kernel,
        grid=(n // tm,),
        in_specs=[pl.BlockSpec((tm, D_MODEL), lambda i: (i, 0)), pl.BlockSpec((1, D_MODEL), lambda i: (0, 0))],
        out_specs=pl.BlockSpec((tm, D_MODEL), lambda i: (i, 0)),
        out_shape=jax.ShapeDtypeStruct((n, D_MODEL), jnp.float32),
        compiler_params=_cparams(("arbitrary",)),
        name="final_norm",
    )(x, g)


def _prep(W):
    bf = jnp.bfloat16
    nl = W['w_in'].shape[0]
    row = lambda a: a.reshape(nl, 1, -1)
    w_in = W['w_in']
    n_sc, n_rk, n_cf = 3 * D_SC, RK_PROJ, 2 * D_CF
    o_rk, o_cf, o_gate = n_sc, n_sc + n_rk, n_sc + n_rk + n_cf
    w_perm = jnp.concatenate([
        w_in[:, :, o_gate:o_gate + 3 * D_MODEL], w_in[:, :, o_rk:o_rk + 3 * D_MODEL], w_in[:, :, 0:n_sc],
        w_in[:, :, o_cf:o_cf + n_cf], w_in[:, :, o_rk + 3 * D_MODEL:o_rk + RK_PROJ],
        jnp.zeros((nl, D_MODEL, LORA_PAD - N_LORA), w_in.dtype)], axis=2).astype(bf)
    mu = W['rk_mu']
    zrows = lambda n: jnp.zeros((nl, n, D_MODEL), jnp.float32)
    o_a, o_g = DECAY_LORA, DECAY_LORA + AAA_LORA + MV_LORA
    seg_head = jnp.arange(SEG) // HEAD_DIM
    on_lanes = lambda a: jnp.broadcast_to(a.reshape(nl, HEADS, HEAD_DIM, 1), (nl, HEADS, HEAD_DIM, LANES))
    return {
        'head_lanes': {'kk': on_lanes(W['rk_k_k']), 'ka': on_lanes(W['rk_k_a']), 'rk': on_lanes(W['rk_r_k']),
                       'lng': on_lanes(W['rk_ln_g']), 'lnb': on_lanes(W['rk_ln_b'])},
        'ones': (seg_head[:, None] == seg_head[None, :]).astype(bf),
        'kk': row(W['rk_k_k']), 'ka': row(W['rk_k_a']), 'rk': row(W['rk_r_k']),
        'lng': row(W['rk_ln_g']), 'lnb': row(W['rk_ln_b']),
        'norm1_g': row(W['norm1_g']), 'w_in': w_perm,
        'mu1': row(mu[:, :3 * D_MODEL]),
        'mu2': row(jnp.pad(mu[:, 3 * D_MODEL:], ((0, 0), (0, LORA_PAD - N_LORA)))),
        'w0': row(W['rk_w0']), 'a0': row(W['rk_a0']), 'v0': row(W['rk_v0']),
        'w2p': jnp.concatenate([W['rk_w2'], zrows(128 - o_a)], axis=1).astype(bf),
        'a2p': jnp.concatenate([zrows(o_a), W['rk_a2']], axis=1).astype(bf),
        'v2p': jnp.concatenate([W['rk_v2'], zrows(128 - MV_LORA)], axis=1).astype(bf),
        'g2p': jnp.concatenate([zrows(o_g - 128), W['rk_g2'], zrows(384 - N_LORA)], axis=1).astype(bf),
        'scw': W['sc_conv_w'], 'cfbin': row(W['cf_b_in']), 'cfw': W['cf_dw_w'],
        'cfdb': row(W['cf_dw_b']), 'cflg': row(W['cf_ln_g']), 'cflb': row(W['cf_ln_b']),
        'sc_w_out': W['sc_w_out'].astype(bf), 'rk_w_out': W['rk_w_out'].astype(bf),
        'cf_w_out': W['cf_w_out'].astype(bf), 'cf_b_out': row(W['cf_b_out']), 'w_o': W['w_o'].astype(bf),
        'norm2_g': row(W['norm2_g']), 'mlp_w1': W['mlp_w1'].astype(bf), 'mlp_w2': W['mlp_w2'].astype(bf),
        'ple_norm_g': row(W['ple_norm_g']), 'ple_gate_w': W['ple_gate_w'].astype(bf),
        'ple_w': W['ple_w'].astype(bf),
    }


def _pick(n, pref):
    return pref if n % pref == 0 else n


def _trunk(x, p, sc0, shift0, wkv0, cf0, P, final_g, cfg):
    bsz, tlen, _ = x.shape
    n = bsz * tlen
    tm = _pick(n, cfg['tm'])
    tt = _pick(tlen, cfg['tt'])
    tpad = -tlen % CHUNK
    tw = _pick(tlen + tpad, cfg['tw'])
    pad_t = lambda a: jnp.pad(a, ((0, 0), (0, tpad), (0, 0))) if tpad else a
    h = x.reshape(n, D_MODEL)
    pf = p.reshape(DEPTH, n, D_PLE)
    sh1 = shift0[:, :, None, :3 * D_MODEL]
    sh2 = jnp.pad(shift0[:, :, 3 * D_MODEL:], ((0, 0), (0, 0), (0, LORA_PAD - N_LORA)))[:, :, None, :]
    lanes_form = bsz == LANES and tlen <= CHUNK
    to_lanes = lambda a: a.reshape(bsz, tlen, HEADS, HEAD_DIM).transpose(1, 2, 3, 0)
    wkv0_l = wkv0.transpose(0, 2, 3, 4, 1) if lanes_form else None
    v_first = None
    new_sc, new_shift, new_wkv, new_cf = [], [], [], []
    for i in range(DEPTH):
        z = _in_proj(h, P, i, tm)
        (r, w, k, v, a, g, xa, xc, nsh1, nsh2, nsc, ncf) = _seq(
            z.reshape(bsz, tlen, NZ), sh1, sh2, sc0, cf0, v_first, P, i, tt)
        if i == 0:
            v_first = v
        if lanes_form:
            y, s_out = _wkv_lanes(to_lanes(r), to_lanes(w), to_lanes(k), to_lanes(v), to_lanes(a), wkv0_l, P, i)
            yw = y.transpose(3, 0, 1, 2).reshape(n, D_MODEL)
        else:
            y, s_out = _wkv(pad_t(r), pad_t(w), pad_t(k), pad_t(v), pad_t(a), wkv0, P, i, cfg['nb'], tw)
            yw = y[:, :tlen].reshape(n, D_MODEL)
        h = _merge(h, xa.reshape(n, D_SC), yw, g.reshape(n, D_MODEL), xc.reshape(n, D_CF), z, P, i, tm)
        h = _mlp_ple(h, pf, P, i, tm, cfg['tf'])
        new_sc.append(nsc)
        new_shift.append(jnp.concatenate([nsh1[:, 0, :], nsh2[:, 0, :N_LORA]], axis=-1))
        new_wkv.append(s_out)
        new_cf.append(ncf)
    y = _final_norm(h, final_g, tm).reshape(bsz, tlen, D_MODEL)
    wkv_out = jnp.stack(new_wkv)
    if lanes_form:
        wkv_out = wkv_out.transpose(0, 4, 1, 2, 3)
    return y, jnp.stack(new_sc), jnp.stack(new_shift), wkv_out, jnp.stack(new_cf)


PROMPT_CFG = {'tm': 512, 'tt': 128, 'tw': 128, 'nb': 4, 'tf': 2048}
SAMPLE_CFG = {'tm': 512, 'tt': 4, 'tw': 8, 'nb': 8, 'tf': 1024}


def kernel(x_prompt, x_sample, p_prompt, p_sample, state_sconv, state_shift, state_wkv, state_cconv,
           norm1_g, w_in, sc_conv_w, sc_w_out, rk_mu, rk_w0, rk_w2, rk_a0, rk_a2, rk_v0, rk_v2, rk_g2,
           rk_k_k, rk_k_a, rk_r_k, rk_ln_g, rk_ln_b, rk_w_out, cf_b_in, cf_dw_w, cf_dw_b, cf_ln_g, cf_ln_b,
           cf_w_out, cf_b_out, w_o, norm2_g, mlp_w1, mlp_w2, ple_w, ple_gate_w, ple_norm_g, final_norm_g):
    W = dict(norm1_g=norm1_g, w_in=w_in, sc_conv_w=sc_conv_w, sc_w_out=sc_w_out, rk_mu=rk_mu, rk_w0=rk_w0,
             rk_w2=rk_w2, rk_a0=rk_a0, rk_a2=rk_a2, rk_v0=rk_v0, rk_v2=rk_v2, rk_g2=rk_g2, rk_k_k=rk_k_k,
             rk_k_a=rk_k_a, rk_r_k=rk_r_k, rk_ln_g=rk_ln_g, rk_ln_b=rk_ln_b, rk_w_out=rk_w_out, cf_b_in=cf_b_in,
             cf_dw_w=cf_dw_w, cf_dw_b=cf_dw_b, cf_ln_g=cf_ln_g, cf_ln_b=cf_ln_b, cf_w_out=cf_w_out,
             cf_b_out=cf_b_out, w_o=w_o, norm2_g=norm2_g, mlp_w1=mlp_w1, mlp_w2=mlp_w2, ple_w=ple_w,
             ple_gate_w=ple_gate_w, ple_norm_g=ple_norm_g)
    P = _prep(W)
    final_g = final_norm_g.reshape(1, D_MODEL)
    bp = x_prompt.shape[0]
    f32 = jnp.float32
    zsc = jnp.zeros((DEPTH, bp, SC_WIDTH - 1, D_SC), f32)
    zsh = jnp.zeros((DEPTH, bp, RK_PROJ), f32)
    zwkv = jnp.zeros((DEPTH, bp, HEADS, HEAD_DIM, HEAD_DIM), f32)
    zcf = jnp.zeros((DEPTH, bp, CF_WIDTH - 1, D_CF), f32)
    y_p, sc_p, sh_p, wkv_p, cf_p = _trunk(x_prompt, p_prompt, zsc, zsh, zwkv, zcf, P, final_g, PROMPT_CFG)
    y_s, sc_s, sh_s, wkv_s, cf_s = _trunk(x_sample, p_sample, state_sconv, state_shift, state_wkv, state_cconv,
                                          P, final_g, SAMPLE_CFG)
    return (y_p, y_s, sc_p, sh_p, wkv_p, cf_p, sc_s, sh_s, wkv_s, cf_s)
```

```python
import functools

import jax
import jax.numpy as jnp
from jax import lax
from jax.experimental import pallas as pl
from jax.experimental.pallas import tpu as pltpu

D_MODEL = 1024
DEPTH = 4
D_SC = 512
D_CF = 512
SC_WIDTH = 3
CF_WIDTH = 31
HEAD_DIM = 64
HEADS = 16
D_FF = 4096
D_PLE = 256
DECAY_LORA, AAA_LORA, MV_LORA, GATE_LORA = 64, 64, 32, 128
N_LORA = DECAY_LORA + AAA_LORA + MV_LORA + GATE_LORA
RK_PROJ = 3 * D_MODEL + N_LORA
RMS_EPS = 1e-6
LN_EPS = 1e-5
GN_EPS = 64e-5
EXP_M05 = 0.6065306597126334

Z_BLK = 3 * D_MODEL
NZ = 3 * Z_BLK
Z2_SC = 0
Z2_CF = 3 * D_SC
Z2_LORA = Z2_CF + 2 * D_CF
LORA_PAD = Z_BLK - Z2_LORA

VMEM_LIMIT_BYTES = 56 * 1024 * 1024


def _cparams(sem):
    return pltpu.CompilerParams(dimension_semantics=sem, vmem_limit_bytes=VMEM_LIMIT_BYTES)


def _sigmoid(x):
    return 1.0 / (1.0 + jnp.exp(-x))


def _rms(x, g):
    return x * lax.rsqrt(jnp.mean(x * x, axis=-1, keepdims=True) + RMS_EPS) * g


def _bdot(a, b):
    return jnp.dot(a.astype(jnp.bfloat16), b, preferred_element_type=jnp.float32)


def _in_proj_kernel(gate, x_ref, g_ref, w_ref, o_ref):
    z = _bdot(_rms(x_ref[...], g_ref[...]), w_ref[...])
    o_ref[...] = _sigmoid(z).astype(o_ref.dtype) if gate else z


def _lpar(a, li):
    return pl.BlockSpec((None,) + a.shape[1:], lambda *_: (li,) + (0,) * (a.ndim - 1))


def _in_proj(x, P, li, tm):
    n = x.shape[0]
    g, w = P['norm1_g'], P['w_in']
    gates = pl.pallas_call(
        functools.partial(_in_proj_kernel, True),
        grid=(n // tm,),
        in_specs=[pl.BlockSpec((tm, D_MODEL), lambda i: (i, 0)), _lpar(g, li),
                  pl.BlockSpec((None, D_MODEL, Z_BLK), lambda i: (li, 0, 0))],
        out_specs=pl.BlockSpec((tm, Z_BLK), lambda i: (i, 0)),
        out_shape=jax.ShapeDtypeStruct((n, Z_BLK), jnp.bfloat16),
        compiler_params=_cparams(("arbitrary",)),
        name="gate_proj",
    )(x, g, w)
    z = pl.pallas_call(
        functools.partial(_in_proj_kernel, False),
        grid=(NZ // Z_BLK - 1, n // tm),
        in_specs=[pl.BlockSpec((tm, D_MODEL), lambda j, i: (i, 0)), _lpar(g, li),
                  pl.BlockSpec((None, D_MODEL, Z_BLK), lambda j, i: (li, 0, j + 1))],
        out_specs=pl.BlockSpec((tm, Z_BLK), lambda j, i: (i, j)),
        out_shape=jax.ShapeDtypeStruct((n, NZ - Z_BLK), jnp.float32),
        compiler_params=_cparams(("arbitrary", "arbitrary")),
        name="in_proj",
    )(x, g, w)
    return gates, z


def _seq_kernel(first_layer, tt, *refs):
    it = iter(refs)
    z1_ref, z2_ref, sh1_ref, sh2_ref, sc0_ref, cf0_ref = (next(it) for _ in range(6))
    vf_ref = None if first_layer else next(it)
    (mu1_ref, mu2_ref, w0_ref, w2_ref, a0_ref, a2_ref, v0_ref, v2_ref, g2_ref,
     scw_ref, cfbin_ref, cfw_ref, cfdb_ref, cflg_ref, cflb_ref) = (next(it) for _ in range(15))
    (r_ref, w_ref, k_ref, v_ref, a_ref, g_ref, xa_ref, xc_ref,
     nsh1_ref, nsh2_ref, nsc_ref, ncf_ref) = (next(it) for _ in range(12))
    s1buf, s2buf, cbuf, gbuf, pbuf = (next(it) for _ in range(5))

    @pl.when(pl.program_id(1) == 0)
    def _():
        s1buf[7:8, :] = sh1_ref[...]
        s2buf[7:8, :] = sh2_ref[...]
        cbuf[6:8, :] = sc0_ref[...]
        gbuf[2:CF_WIDTH + 1, :] = cf0_ref[...]
        gbuf[0:2, :] = jnp.zeros((2, D_CF), jnp.float32)
        gbuf[CF_WIDTH + 1 + tt:CF_WIDTH + 1 + tt + 8, :] = jnp.zeros((8, D_CF), jnp.float32)

    z1 = z1_ref[...]
    s1buf[8:8 + tt, :] = z1
    zm1 = z1 + (s1buf[7:7 + tt, :] - z1) * mu1_ref[...]
    z2l = z2_ref[:, Z2_LORA:Z_BLK]
    s2buf[8:8 + tt, :] = z2l
    zm2 = z2l + (s2buf[7:7 + tt, :] - z2l) * mu2_ref[...]
    last1 = s1buf[7 + tt:8 + tt, :]
    last2 = s2buf[7 + tt:8 + tt, :]
    nsh1_ref[...] = last1
    nsh2_ref[...] = last2
    s1buf[7:8, :] = last1
    s2buf[7:8, :] = last2

    x01 = zm2[:, 0:128]
    lw = _bdot(jnp.tanh(x01), w2_ref[...])
    la = _bdot(x01, a2_ref[...])
    w_ref[...] = -EXP_M05 * _sigmoid(w0_ref[...] + lw)
    a_ref[...] = _sigmoid(a0_ref[...] + la)
    g_ref[...] = _bdot(_sigmoid(zm2[:, 128:384]), g2_ref[...])
    r_ref[...] = zm1[:, 0:D_MODEL]
    k_ref[...] = zm1[:, D_MODEL:2 * D_MODEL]
    v = zm1[:, 2 * D_MODEL:3 * D_MODEL]
    if not first_layer:
        lv = _bdot(zm2[:, 128:256], v2_ref[...])
        v = v + (vf_ref[...] - v) * _sigmoid(v0_ref[...] + lv)
    v_ref[...] = v

    cu = z2_ref[:, Z2_SC + D_SC:Z2_SC + 2 * D_SC] * z2_ref[:, Z2_SC + 2 * D_SC:Z2_SC + 3 * D_SC]
    cbuf[8:8 + tt, :] = cu
    conv_a = (scw_ref[0:1, :] * cbuf[6:6 + tt, :] + scw_ref[1:2, :] * cbuf[7:7 + tt, :]
              + scw_ref[2:3, :] * cu)
    xa_ref[...] = z2_ref[:, Z2_SC:Z2_SC + D_SC] * conv_a
    tail_a = cbuf[6 + tt:8 + tt, :]
    nsc_ref[...] = tail_a
    cbuf[6:8, :] = tail_a

    zc = z2_ref[:, Z2_CF:Z2_CF + 2 * D_CF] + cfbin_ref[...]
    glu = zc[:, 0:D_CF] * _sigmoid(zc[:, D_CF:2 * D_CF])
    gbuf[CF_WIDTH + 1:CF_WIDTH + 1 + tt, :] = glu
    if tt % 8 == 0:
        acc = None
        for ph in range(8):
            part = None
            for j in range(ph, CF_WIDTH + 2, 8):
                if j < 2:
                    continue
                term = cfw_ref[j - 2:j - 1, :] * gbuf[j - ph:j - ph + tt + 8, :]
                part = term if part is None else part + term
            pbuf[...] = part
            sh = pbuf[ph:ph + tt, :]
            acc = sh if acc is None else acc + sh
    else:
        acc = cfw_ref[0:1, :] * gbuf[2:2 + tt, :]
        for kk in range(1, CF_WIDTH):
            acc = acc + cfw_ref[kk:kk + 1, :] * gbuf[2 + kk:2 + kk + tt, :]
    cc = acc + cfdb_ref[...]
    mu = jnp.mean(cc, axis=-1, keepdims=True)
    cd = cc - mu
    var = jnp.mean(cd * cd, axis=-1, keepdims=True)
    ln = cd * lax.rsqrt(var + LN_EPS) * cflg_ref[...] + cflb_ref[...]
    xc_ref[...] = ln * _sigmoid(ln)
    tail_c = gbuf[2 + tt:CF_WIDTH + 1 + tt, :]
    ncf_ref[...] = tail_c
    gbuf[2:CF_WIDTH + 1, :] = tail_c


def _seq(z, sh1, sh2, sc0, cf0, vfirst, P, li, tt):
    bsz, tlen, _ = z.shape
    first_layer = vfirst is None
    tok = lambda width, blk: pl.BlockSpec((None, tt, width), lambda b, t, blk=blk: (b, t, blk))
    st = lambda rows, width: pl.BlockSpec((None, rows, width), lambda b, t: (b, 0, 0))
    lst = lambda rows, width: pl.BlockSpec((None, None, rows, width), lambda b, t: (li, b, 0, 0))
    par = lambda a: _lpar(a, li)
    params = [P[n] for n in ('mu1', 'mu2', 'w0', 'w2p', 'a0', 'a2p', 'v0', 'v2p', 'g2p',
                              'scw', 'cfbin', 'cfw', 'cfdb', 'cflg', 'cflb')]
    args = [z, z, sh1, sh2, sc0, cf0]
    in_specs = [tok(Z_BLK, 0), tok(Z_BLK, 1), lst(1, Z_BLK), lst(1, LORA_PAD),
                lst(SC_WIDTH - 1, D_SC), lst(CF_WIDTH - 1, D_CF)]
    if not first_layer:
        args.append(vfirst)
        in_specs.append(tok(D_MODEL, 0))
    args += params
    in_specs += [par(a) for a in params]
    f32 = jnp.float32
    out_shape = ([jax.ShapeDtypeStruct((bsz, tlen, D_MODEL), f32)] * 6
                 + [jax.ShapeDtypeStruct((bsz, tlen, D_SC), f32), jax.ShapeDtypeStruct((bsz, tlen, D_CF), f32),
                    jax.ShapeDtypeStruct((bsz, 1, Z_BLK), f32), jax.ShapeDtypeStruct((bsz, 1, LORA_PAD), f32),
                    jax.ShapeDtypeStruct((bsz, SC_WIDTH - 1, D_SC), f32),
                    jax.ShapeDtypeStruct((bsz, CF_WIDTH - 1, D_CF), f32)])
    out_specs = ([tok(D_MODEL, 0)] * 6 + [tok(D_SC, 0), tok(D_CF, 0), st(1, Z_BLK), st(1, LORA_PAD),
                                           st(SC_WIDTH - 1, D_SC), st(CF_WIDTH - 1, D_CF)])
    scratch = [pltpu.VMEM((8 + tt, Z_BLK), f32), pltpu.VMEM((8 + tt, LORA_PAD), f32),
               pltpu.VMEM((8 + tt, D_SC), f32), pltpu.VMEM((CF_WIDTH + 1 + tt + 8, D_CF), f32),
               pltpu.VMEM((tt + 8, D_CF), f32)]
    return pl.pallas_call(
        functools.partial(_seq_kernel, first_layer, tt),
        grid=(bsz, tlen // tt),
        in_specs=in_specs, out_specs=out_specs, out_shape=out_shape, scratch_shapes=scratch,
        compiler_params=_cparams(("arbitrary", "arbitrary")),
        name="seq",
    )(*args)


CHUNK = 8
PAIR = 2 * HEAD_DIM
N_PAIR = HEADS // 2
SEG = 256
P_ROWS = 4 * CHUNK * CHUNK - 2 * CHUNK
R_OFF = [32 * s + 16 if s < CHUNK - 1 else 32 * s for s in range(CHUNK)]


def _segsum(x, ones):
    xb = x.astype(jnp.bfloat16)
    return jnp.concatenate([jnp.dot(xb[:, q * SEG:(q + 1) * SEG], ones, preferred_element_type=jnp.float32)
                            for q in range(D_MODEL // SEG)], axis=1)


def _wkv_kernel(nb, tt, r_ref, ld_ref, k_ref, v_ref, a_ref, s0_ref, kkp_ref, kap_ref, rkp_ref, lng_ref, lnb_ref,
                ones_ref, y_ref, sout_ref, s_scr, kk_scr, km_scr, o_scr, p_scr, e_scr):
    f32, bf16 = jnp.float32, jnp.bfloat16
    ones = ones_ref[...]

    @pl.when(pl.program_id(1) == 0)
    def _():
        for bb in range(nb):
            for p in range(N_PAIR):
                s_scr[bb, p] = jnp.concatenate([s0_ref[bb, 2 * p], s0_ref[bb, 2 * p + 1]], axis=1)

    for bb in range(nb):
        kraw = k_ref[bb]
        kkp = kraw * kkp_ref[...]
        nrm = jnp.sqrt(_segsum(kkp * kkp, ones))
        kk_scr[bb] = kkp / jnp.maximum(nrm, 1e-12)
        km_scr[bb] = kraw * (1.0 + (a_ref[bb] - 1.0) * kap_ref[...])

    row = lax.broadcasted_iota(jnp.int32, (CHUNK, D_MODEL), 0)
    head0 = (lax.broadcasted_iota(jnp.int32, (2 * CHUNK, D_MODEL), 1) // HEAD_DIM) % 2 == 0
    bcast = lambda x, s: jnp.broadcast_to(x[s:s + 1, :], (CHUNK, D_MODEL))

    def chunk(c, carry):
        rows = pl.ds(pl.multiple_of(c * CHUNK, CHUNK), CHUNK)
        st = []
        for bb in range(nb):
            r = r_ref[bb, rows, :]
            ld = ld_ref[bb, rows, :]
            v = v_ref[bb, rows, :]
            kk = kk_scr[bb, rows, :]
            cum = ld
            for sh in (1, 2, 4):
                cum = cum + jnp.where(row >= sh, pltpu.roll(cum, sh, axis=0), 0.0)
            w_cum = jnp.exp(cum)
            w_inv = jnp.exp(-cum)
            at = -kk * jnp.exp(cum - ld)
            rt = r * w_cum
            bt = kk * a_ref[bb, rows, :] * w_inv
            kt = km_scr[bb, rows, :] * w_inv
            wc = bcast(w_cum, CHUNK - 1)
            lhs = jnp.concatenate([at, rt], axis=0)
            lhs = jnp.concatenate([jnp.where(head0, lhs, 0.0), jnp.where(head0, 0.0, lhs)], axis=0).astype(bf16)
            st.append([v, at, rt, bt, kt, wc, lhs])
        for bb in range(nb):
            v, at, rt, bt, kt, wc, lhs = st[bb]
            xy = []
            for p in range(N_PAIR):
                sl = slice(p * PAIR, (p + 1) * PAIR)
                xh = lax.dot_general(lhs[:, sl], s_scr[bb, p].astype(bf16), (((1,), (1,)), ((), ())),
                                     preferred_element_type=f32)
                xy.append(xh[0:2 * CHUNK])
                xy.append(xh[2 * CHUNK:4 * CHUNK])
            st[bb][6] = jnp.concatenate(xy, axis=1)
            for s in range(CHUNK):
                bs, ks = bcast(bt, s), bcast(kt, s)
                rm = jnp.where(row >= s, rt, 0.0) if s else rt
                if s < CHUNK - 1:
                    am = jnp.where(row > s, at, 0.0)
                    p_scr[bb, 32 * s:32 * s + 16, :] = jnp.concatenate([bs * am, ks * am], axis=0).astype(bf16)
                p_scr[bb, R_OFF[s]:R_OFF[s] + 16, :] = jnp.concatenate([bs * rm, ks * rm], axis=0).astype(bf16)
            for q in range(D_MODEL // SEG):
                sl = slice(q * SEG, (q + 1) * SEG)
                e_scr[bb, :, sl] = jnp.dot(p_scr[bb, :, sl], ones, preferred_element_type=f32)
        accs = []
        for bb in range(nb):
            v, at, rt, bt, kt, wc, xy = st[bb]
            pre_u = pre_o = None
            for s in range(CHUNK):
                vs = bcast(v, s)
                to = e_scr[bb, R_OFF[s] + 8:R_OFF[s] + 16, :] * vs
                pre_o = to if pre_o is None else pre_o + to
                if s < CHUNK - 1:
                    tu = e_scr[bb, 32 * s + 8:32 * s + 16, :] * vs
                    pre_u = tu if pre_u is None else pre_u + tu
            acc_u = xy[0:CHUNK] + pre_u
            acc_o = xy[CHUNK:2 * CHUNK] + pre_o
            for s in range(CHUNK):
                us = bcast(acc_u, s)
                if s < CHUNK - 1:
                    acc_u = acc_u + e_scr[bb, 32 * s:32 * s + 8, :] * us
                acc_o = acc_o + e_scr[bb, R_OFF[s]:R_OFF[s] + 8, :] * us
            o_scr[bb, rows, :] = acc_o
            accs.append(acc_u)
        for bb in range(nb):
            v, at, rt, bt, kt, wc, xy = st[bb]
            uv = jnp.concatenate([accs[bb], v], axis=0)
            uv_swapped = pltpu.roll(uv, D_MODEL - HEAD_DIM, axis=1)
            uv = jnp.concatenate([jnp.where(head0, uv, 0.0), jnp.where(head0, uv_swapped, 0.0)], axis=0).astype(bf16)
            bk = jnp.concatenate([bt * wc, kt * wc], axis=0)
            bk = jnp.concatenate([jnp.where(head0, bk, 0.0), jnp.where(head0, 0.0, bk)], axis=0).astype(bf16)
            for p in range(N_PAIR):
                sl = slice(p * PAIR, (p + 1) * PAIR)
                upd = lax.dot_general(uv[:, p * PAIR:p * PAIR + HEAD_DIM], bk[:, sl], (((0,), (0,)), ((), ())),
                                      preferred_element_type=f32)
                s_scr[bb, p] = s_scr[bb, p] * wc[0:1, sl] + upd
        return carry

    lax.fori_loop(0, tt // CHUNK, chunk, 0)

    for bb in range(nb):
        o = o_scr[bb]
        mu = _segsum(o, ones) * (1.0 / HEAD_DIM)
        od = o - mu
        var = _segsum(od * od, ones) * (1.0 / HEAD_DIM)
        y = od * lax.rsqrt(var + GN_EPS) * lng_ref[...] + lnb_ref[...]
        bonus = _segsum(r_ref[bb] * km_scr[bb] * rkp_ref[...], ones)
        y_ref[bb] = y + bonus * v_ref[bb]

    @pl.when(pl.program_id(1) == pl.num_programs(1) - 1)
    def _():
        for bb in range(nb):
            for p in range(N_PAIR):
                sp = s_scr[bb, p]
                sout_ref[bb, 2 * p] = sp[:, 0:HEAD_DIM]
                sout_ref[bb, 2 * p + 1] = sp[:, HEAD_DIM:PAIR]


def _wkv(r, ld, k, v, a, s0, P, li, nb, tt):
    bsz, tlen, _ = r.shape
    f32 = jnp.float32
    tok = pl.BlockSpec((nb, tt, D_MODEL), lambda b, t: (b, t, 0))
    stt = pl.BlockSpec((nb, HEADS, HEAD_DIM, HEAD_DIM), lambda b, t: (b, 0, 0, 0))
    lstt = pl.BlockSpec((None, nb, HEADS, HEAD_DIM, HEAD_DIM), lambda b, t: (li, b, 0, 0, 0))
    par = _lpar(P['kk'], li)
    return pl.pallas_call(
        functools.partial(_wkv_kernel, nb, tt),
        grid=(bsz // nb, tlen // tt),
        in_specs=[tok] * 5 + [lstt] + [par] * 5 + [pl.BlockSpec((SEG, SEG), lambda b, t: (0, 0))],
        out_specs=[tok, stt],
        out_shape=[jax.ShapeDtypeStruct((bsz, tlen, D_MODEL), f32),
                   jax.ShapeDtypeStruct((bsz, HEADS, HEAD_DIM, HEAD_DIM), f32)],
        scratch_shapes=[pltpu.VMEM((nb, N_PAIR, HEAD_DIM, PAIR), f32),
                        pltpu.VMEM((nb, tt, D_MODEL), f32), pltpu.VMEM((nb, tt, D_MODEL), f32),
                        pltpu.VMEM((nb, tt, D_MODEL), f32),
                        pltpu.VMEM((nb, P_ROWS, D_MODEL), jnp.bfloat16),
                        pltpu.VMEM((nb, P_ROWS, D_MODEL), f32)],
        compiler_params=_cparams(("arbitrary", "arbitrary")),
        name="wkv",
    )(r, ld, k, v, a, s0, P['kk'], P['ka'], P['rk'], P['lng'], P['lnb'], P['ones'])


LANES = 128


def _wkv_lanes_kernel(tlen, r_ref, ld_ref, k_ref, v_ref, a_ref, s0_ref, kkp_ref, kap_ref, rkp_ref, lng_ref, lnb_ref,
                      y_ref, s_ref, o_scr):
    s_ref[...] = s0_ref[...]
    for t in range(tlen):
        rt = r_ref[t]
        wt = jnp.exp(ld_ref[t])
        kraw = k_ref[t]
        at = a_ref[t]
        kkp = kraw * kkp_ref[...]
        nrm = jnp.sqrt(jnp.sum(kkp * kkp, axis=0, keepdims=True))
        kk = kkp / jnp.maximum(nrm, 1e-12)
        kt = kraw * (1.0 + (at - 1.0) * kap_ref[...])
        bt = kk * at
        nkk = -kk
        for i in range(HEAD_DIM):
            si = s_ref[i]
            sa = jnp.sum(si * nkk, axis=0, keepdims=True)
            sn = si * wt + sa * bt + v_ref[t, i:i + 1, :] * kt
            s_ref[i] = sn
            o_scr[i:i + 1, :] = jnp.sum(sn * rt, axis=0, keepdims=True)
        o = o_scr[...]
        mu = jnp.mean(o, axis=0, keepdims=True)
        od = o - mu
        var = jnp.mean(od * od, axis=0, keepdims=True)
        y = od * lax.rsqrt(var + GN_EPS) * lng_ref[...] + lnb_ref[...]
        bonus = jnp.sum(rt * kt * rkp_ref[...], axis=0, keepdims=True)
        y_ref[t] = y + bonus * v_ref[t]


def _wkv_lanes(r, ld, k, v, a, s0, P, li):
    tlen, _, _, bsz = r.shape
    f32 = jnp.float32
    tok = pl.BlockSpec((tlen, None, HEAD_DIM, bsz), lambda h: (0, h, 0, 0))
    lstt = pl.BlockSpec((None, None, HEAD_DIM, HEAD_DIM, bsz), lambda h: (li, h, 0, 0, 0))
    stt = pl.BlockSpec((None, HEAD_DIM, HEAD_DIM, bsz), lambda h: (h, 0, 0, 0))
    par = pl.BlockSpec((None, None, HEAD_DIM, bsz), lambda h: (li, h, 0, 0))
    hp = P['head_lanes']
    return pl.pallas_call(
        functools.partial(_wkv_lanes_kernel, tlen),
        grid=(HEADS,),
        in_specs=[tok] * 5 + [lstt] + [par] * 5,
        out_specs=[tok, stt],
        out_shape=[jax.ShapeDtypeStruct((tlen, HEADS, HEAD_DIM, bsz), f32),
                   jax.ShapeDtypeStruct((HEADS, HEAD_DIM, HEAD_DIM, bsz), f32)],
        scratch_shapes=[pltpu.VMEM((HEAD_DIM, bsz), f32)],
        compiler_params=_cparams(("arbitrary",)),
        name="wkv_lanes",
    )(r, ld, k, v, a, s0, hp['kk'], hp['ka'], hp['rk'], hp['lng'], hp['lnb'])


def _merge_kernel(h_ref, xa_ref, yw_ref, g_ref, xc_ref, zg_ref, wa_ref, wb_ref, wc_ref, bc_ref, wo_ref, o_ref):
    ya = _bdot(xa_ref[...], wa_ref[...])
    yb = _bdot(yw_ref[...] * g_ref[...], wb_ref[...])
    yc = _bdot(xc_ref[...], wc_ref[...]) + bc_ref[...]
    gate = lambda i: zg_ref[:, i * D_MODEL:(i + 1) * D_MODEL].astype(jnp.float32)
    m = gate(0) * ya + gate(1) * yb + gate(2) * yc
    o_ref[...] = h_ref[...] + _bdot(m, wo_ref[...])


def _merge(h, xa, yw, g, xc, z, P, li, tm):
    n = h.shape[0]
    tok = lambda width: pl.BlockSpec((tm, width), lambda i: (i, 0))
    par = lambda a: _lpar(a, li)
    ws = [P[n_] for n_ in ('sc_w_out', 'rk_w_out', 'cf_w_out', 'cf_b_out', 'w_o')]
    return pl.pallas_call(
        _merge_kernel,
        grid=(n // tm,),
        in_specs=[tok(D_MODEL), tok(D_SC), tok(D_MODEL), tok(D_MODEL), tok(D_CF), tok(Z_BLK)] + [par(a) for a in ws],
        out_specs=tok(D_MODEL),
        out_shape=jax.ShapeDtypeStruct((n, D_MODEL), jnp.float32),
        compiler_params=_cparams(("arbitrary",)),
        name="merge",
    )(h, xa, yw, g, xc, z, *ws)


def _mlp_ple_kernel(nk, final, h_ref, p_ref, g2_ref, w1_ref, w2_ref, gp_ref, wg_ref, wp_ref, *rest):
    gf_ref = rest[0] if final else None
    o_ref, hn_scr, acc_scr = rest[-3:]
    kstep = pl.program_id(1)

    @pl.when(kstep == 0)
    def _():
        hn_scr[...] = _rms(h_ref[...], g2_ref[...]).astype(jnp.bfloat16)
        acc_scr[...] = jnp.zeros_like(acc_scr)

    u = jnp.dot(hn_scr[...], w1_ref[...], preferred_element_type=jnp.float32)
    u = jnp.maximum(u, 0.0)
    acc_scr[...] += _bdot(u * u, w2_ref[...])

    @pl.when(kstep == nk - 1)
    def _():
        h1 = h_ref[...] + acc_scr[...]
        gate = _sigmoid(_bdot(_rms(h1, gp_ref[...]), wg_ref[...]))
        h2 = h1 + gate * _bdot(p_ref[...], wp_ref[...])
        o_ref[...] = h2 if gf_ref is None else _rms(h2, gf_ref[...])


def _mlp_ple(h, p, P, li, tm, tf, final_g=None):
    n = h.shape[0]
    nk = D_FF // tf
    par = lambda a: _lpar(a, li)
    extra = [] if final_g is None else [final_g]
    return pl.pallas_call(
        functools.partial(_mlp_ple_kernel, nk, final_g is not None),
        grid=(n // tm, nk),
        in_specs=[pl.BlockSpec((tm, D_MODEL), lambda i, k: (i, 0)),
                  pl.BlockSpec((None, tm, D_PLE), lambda i, k: (li, i, 0)),
                  par(P['norm2_g']),
                  pl.BlockSpec((None, D_MODEL, tf), lambda i, k: (li, 0, k)),
                  pl.BlockSpec((None, tf, D_MODEL), lambda i, k: (li, k, 0)),
                  par(P['ple_norm_g']), par(P['ple_gate_w']), par(P['ple_w'])]
                 + [pl.BlockSpec((1, D_MODEL), lambda i, k: (0, 0)) for _ in extra],
        out_specs=pl.BlockSpec((tm, D_MODEL), lambda i, k: (i, 0)),
        out_shape=jax.ShapeDtypeStruct((n, D_MODEL), jnp.float32),
        scratch_shapes=[pltpu.VMEM((tm, D_MODEL), jnp.bfloat16), pltpu.VMEM((tm, D_MODEL), jnp.float32)],
        compiler_params=_cparams(("arbitrary", "arbitrary")),
        name="mlp_ple",
    )(h, p, P['norm2_g'], P['mlp_w1'], P['mlp_w2'], P['ple_norm_g'], P['ple_gate_w'], P['ple_w'], *extra)


def _prep(W):
    bf = jnp.bfloat16
    nl = W['w_in'].shape[0]
    row = lambda a: a.reshape(nl, 1, -1)
    w_in = W['w_in']
    n_sc, n_rk, n_cf = 3 * D_SC, RK_PROJ, 2 * D_CF
    o_rk, o_cf, o_gate = n_sc, n_sc + n_rk, n_sc + n_rk + n_cf
    w_perm = jnp.concatenate([
        w_in[:, :, o_gate:o_gate + 3 * D_MODEL], w_in[:, :, o_rk:o_rk + 3 * D_MODEL], w_in[:, :, 0:n_sc],
        w_in[:, :, o_cf:o_cf + n_cf], w_in[:, :, o_rk + 3 * D_MODEL:o_rk + RK_PROJ],
        jnp.zeros((nl, D_MODEL, LORA_PAD - N_LORA), w_in.dtype)], axis=2).astype(bf)
    mu = W['rk_mu']
    zrows = lambda n: jnp.zeros((nl, n, D_MODEL), jnp.float32)
    o_a, o_g = DECAY_LORA, DECAY_LORA + AAA_LORA + MV_LORA
    seg_head = jnp.arange(SEG) // HEAD_DIM
    on_lanes = lambda a: jnp.broadcast_to(a.reshape(nl, HEADS, HEAD_DIM, 1), (nl, HEADS, HEAD_DIM, LANES))
    return {
        'head_lanes': {'kk': on_lanes(W['rk_k_k']), 'ka': on_lanes(W['rk_k_a']), 'rk': on_lanes(W['rk_r_k']),
                       'lng': on_lanes(W['rk_ln_g']), 'lnb': on_lanes(W['rk_ln_b'])},
        'ones': (seg_head[:, None] == seg_head[None, :]).astype(bf),
        'kk': row(W['rk_k_k']), 'ka': row(W['rk_k_a']), 'rk': row(W['rk_r_k']),
        'lng': row(W['rk_ln_g']), 'lnb': row(W['rk_ln_b']),
        'norm1_g': row(W['norm1_g']), 'w_in': w_perm,
        'mu1': row(mu[:, :3 * D_MODEL]),
        'mu2': row(jnp.pad(mu[:, 3 * D_MODEL:], ((0, 0), (0, LORA_PAD - N_LORA)))),
        'w0': row(W['rk_w0']), 'a0': row(W['rk_a0']), 'v0': row(W['rk_v0']),
        'w2p': jnp.concatenate([W['rk_w2'], zrows(128 - o_a)], axis=1).astype(bf),
        'a2p': jnp.concatenate([zrows(o_a), W['rk_a2']], axis=1).astype(bf),
        'v2p': jnp.concatenate([W['rk_v2'], zrows(128 - MV_LORA)], axis=1).astype(bf),
        'g2p': jnp.concatenate([zrows(o_g - 128), W['rk_g2'], zrows(384 - N_LORA)], axis=1).astype(bf),
        'scw': W['sc_conv_w'], 'cfbin': row(W['cf_b_in']), 'cfw': W['cf_dw_w'],
        'cfdb': row(W['cf_dw_b']), 'cflg': row(W['cf_ln_g']), 'cflb': row(W['cf_ln_b']),
        'sc_w_out': W['sc_w_out'].astype(bf), 'rk_w_out': W['rk_w_out'].astype(bf),
        'cf_w_out': W['cf_w_out'].astype(bf), 'cf_b_out': row(W['cf_b_out']), 'w_o': W['w_o'].astype(bf),
        'norm2_g': row(W['norm2_g']), 'mlp_w1': W['mlp_w1'].astype(bf), 'mlp_w2': W['mlp_w2'].astype(bf),
        'ple_norm_g': row(W['ple_norm_g']), 'ple_gate_w': W['ple_gate_w'].astype(bf),
        'ple_w': W['ple_w'].astype(bf),
    }


def _pick(n, pref):
    return pref if n % pref == 0 else n


def _trunk(x, p, sc0, shift0, wkv0, cf0, P, final_g, cfg):
    bsz, tlen, _ = x.shape
    n = bsz * tlen
    tm = _pick(n, cfg['tm'])
    tt = _pick(tlen, cfg['tt'])
    tpad = -tlen % CHUNK
    tw = _pick(tlen + tpad, cfg['tw'])
    pad_t = lambda a: jnp.pad(a, ((0, 0), (0, tpad), (0, 0))) if tpad else a
    h = x.reshape(n, D_MODEL)
    pf = p.reshape(DEPTH, n, D_PLE)
    sh1 = shift0[:, :, None, :3 * D_MODEL]
    sh2 = jnp.pad(shift0[:, :, 3 * D_MODEL:], ((0, 0), (0, 0), (0, LORA_PAD - N_LORA)))[:, :, None, :]
    lanes_form = bsz == LANES and tlen <= CHUNK
    to_lanes = lambda a: a.reshape(bsz, tlen, HEADS, HEAD_DIM).transpose(1, 2, 3, 0)
    wkv0_l = wkv0.transpose(0, 2, 3, 4, 1) if lanes_form else None
    v_first = None
    new_sc, new_shift, new_wkv, new_cf = [], [], [], []
    for i in range(DEPTH):
        gates, z = _in_proj(h, P, i, tm)
        (r, w, k, v, a, g, xa, xc, nsh1, nsh2, nsc, ncf) = _seq(
            z.reshape(bsz, tlen, NZ - Z_BLK), sh1, sh2, sc0, cf0, v_first, P, i, tt)
        if i == 0:
            v_first = v
        if lanes_form:
            y, s_out = _wkv_lanes(to_lanes(r), to_lanes(w), to_lanes(k), to_lanes(v), to_lanes(a), wkv0_l, P, i)
            yw = y.transpose(3, 0, 1, 2).reshape(n, D_MODEL)
        else:
            y, s_out = _wkv(pad_t(r), pad_t(w), pad_t(k), pad_t(v), pad_t(a), wkv0, P, i, cfg['nb'], tw)
            yw = y[:, :tlen].reshape(n, D_MODEL)
        h = _merge(h, xa.reshape(n, D_SC), yw, g.reshape(n, D_MODEL), xc.reshape(n, D_CF), gates, P, i, tm)
        h = _mlp_ple(h, pf, P, i, tm, cfg['tf'], final_g if i == DEPTH - 1 else None)
        new_sc.append(nsc)
        new_shift.append(jnp.concatenate([nsh1[:, 0, :], nsh2[:, 0, :N_LORA]], axis=-1))
        new_wkv.append(s_out)
        new_cf.append(ncf)
    y = h.reshape(bsz, tlen, D_MODEL)
    wkv_out = jnp.stack(new_wkv)
    if lanes_form:
        wkv_out = wkv_out.transpose(0, 4, 1, 2, 3)
    return y, jnp.stack(new_sc), jnp.stack(new_shift), wkv_out, jnp.stack(new_cf)


PROMPT_CFG = {'tm': 512, 'tt': 128, 'tw': 128, 'nb': 4, 'tf': 2048}
SAMPLE_CFG = {'tm': 512, 'tt': 4, 'tw': 8, 'nb': 8, 'tf': 1024}


def kernel(x_prompt, x_sample, p_prompt, p_sample, state_sconv, state_shift, state_wkv, state_cconv,
           norm1_g, w_in, sc_conv_w, sc_w_out, rk_mu, rk_w0, rk_w2, rk_a0, rk_a2, rk_v0, rk_v2, rk_g2,
           rk_k_k, rk_k_a, rk_r_k, rk_ln_g, rk_ln_b, rk_w_out, cf_b_in, cf_dw_w, cf_dw_b, cf_ln_g, cf_ln_b,
           cf_w_out, cf_b_out, w_o, norm2_g, mlp_w1, mlp_w2, ple_w, ple_gate_w, ple_norm_g, final_norm_g):
    W = dict(norm1_g=norm1_g, w_in=w_in, sc_conv_w=sc_conv_w, sc_w_out=sc_w_out, rk_mu=rk_mu, rk_w0=rk_w0,
             rk_w2=rk_w2, rk_a0=rk_a0, rk_a2=rk_a2, rk_v0=rk_v0, rk_v2=rk_v2, rk_g2=rk_g2, rk_k_k=rk_k_k,
             rk_k_a=rk_k_a, rk_r_k=rk_r_k, rk_ln_g=rk_ln_g, rk_ln_b=rk_ln_b, rk_w_out=rk_w_out, cf_b_in=cf_b_in,
             cf_dw_w=cf_dw_w, cf_dw_b=cf_dw_b, cf_ln_g=cf_ln_g, cf_ln_b=cf_ln_b, cf_w_out=cf_w_out,
             cf_b_out=cf_b_out, w_o=w_o, norm2_g=norm2_g, mlp_w1=mlp_w1, mlp_w2=mlp_w2, ple_w=ple_w,
             ple_gate_w=ple_gate_w, ple_norm_g=ple_norm_g)
    P = _prep(W)
    final_g = final_norm_g.reshape(1, D_MODEL)
    bp = x_prompt.shape[0]
    f32 = jnp.float32
    zsc = jnp.zeros((DEPTH, bp, SC_WIDTH - 1, D_SC), f32)
    zsh = jnp.zeros((DEPTH, bp, RK_PROJ), f32)
    zwkv = jnp.zeros((DEPTH, bp, HEADS, HEAD_DIM, HEAD_DIM), f32)
    zcf = jnp.zeros((DEPTH, bp, CF_WIDTH - 1, D_CF), f32)
    y_p, sc_p, sh_p, wkv_p, cf_p = _trunk(x_prompt, p_prompt, zsc, zsh, zwkv, zcf, P, final_g, PROMPT_CFG)
    y_s, sc_s, sh_s, wkv_s, cf_s = _trunk(x_sample, p_sample, state_sconv, state_shift, state_wkv, state_cconv,
                                          P, final_g, SAMPLE_CFG)
    return (y_p, y_s, sc_p, sh_p, wkv_p, cf_p, sc_s, sh_s, wkv_s, cf_s)
```

```python
import functools

import jax
import jax.numpy as jnp
from jax import lax
from jax.experimental import pallas as pl
from jax.experimental.pallas import tpu as pltpu

D_MODEL = 1024
DEPTH = 4
D_SC = 512
D_CF = 512
SC_WIDTH = 3
CF_WIDTH = 31
HEAD_DIM = 64
HEADS = 16
D_FF = 4096
D_PLE = 256
DECAY_LORA, AAA_LORA, MV_LORA, GATE_LORA = 64, 64, 32, 128
N_LORA = DECAY_LORA + AAA_LORA + MV_LORA + GATE_LORA
RK_PROJ = 3 * D_MODEL + N_LORA
RMS_EPS = 1e-6
LN_EPS = 1e-5
GN_EPS = 64e-5
EXP_M05 = 0.6065306597126334

Z_BLK = 3 * D_MODEL
NZ = 3 * Z_BLK
Z2_SC = 0
Z2_CF = 3 * D_SC
Z2_LORA = Z2_CF + 2 * D_CF
LORA_PAD = Z_BLK - Z2_LORA

VMEM_LIMIT_BYTES = 56 * 1024 * 1024


def _cparams(sem):
    return pltpu.CompilerParams(dimension_semantics=sem, vmem_limit_bytes=VMEM_LIMIT_BYTES)


def _sigmoid(x):
    return 1.0 / (1.0 + jnp.exp(-x))


def _rms(x, g):
    return x * lax.rsqrt(jnp.mean(x * x, axis=-1, keepdims=True) + RMS_EPS) * g


def _bdot(a, b):
    return jnp.dot(a.astype(jnp.bfloat16), b, preferred_element_type=jnp.float32)


def _in_proj_kernel(x_ref, g_ref, w_ref, o_ref):
    o_ref[...] = _bdot(_rms(x_ref[...], g_ref[...]), w_ref[...])


def _lpar(a, li):
    return pl.BlockSpec((None,) + a.shape[1:], lambda *_: (li,) + (0,) * (a.ndim - 1))


def _in_proj(x, P, li, tm):
    n = x.shape[0]
    g, w = P['norm1_g'], P['w_in']
    return pl.pallas_call(
        _in_proj_kernel,
        grid=(NZ // Z_BLK, n // tm),
        in_specs=[
            pl.BlockSpec((tm, D_MODEL), lambda j, i: (i, 0)),
            _lpar(g, li),
            pl.BlockSpec((None, D_MODEL, Z_BLK), lambda j, i: (li, 0, j)),
        ],
        out_specs=pl.BlockSpec((tm, Z_BLK), lambda j, i: (i, j)),
        out_shape=jax.ShapeDtypeStruct((n, NZ), jnp.float32),
        compiler_params=_cparams(("arbitrary", "arbitrary")),
        name="in_proj",
    )(x, g, w)


def _seq_batched(first_layer):
    n_in = 6 if first_layer else 7
    n_par = 15
    return set(range(n_in)) | set(range(n_in + n_par, n_in + n_par + 12 + 5))


SEQ_BATCHED = {True: _seq_batched(True), False: _seq_batched(False)}


def _seq_kernel(first_layer, tt, nbs, *refs):
    for bb in range(nbs):
        _seq_body(first_layer, tt, [r.at[bb] if i in SEQ_BATCHED[first_layer] else r for i, r in enumerate(refs)])


def _seq_body(first_layer, tt, refs):
    it = iter(refs)
    z1_ref, z2_ref, sh1_ref, sh2_ref, sc0_ref, cf0_ref = (next(it) for _ in range(6))
    vf_ref = None if first_layer else next(it)
    (mu1_ref, mu2_ref, w0_ref, w2_ref, a0_ref, a2_ref, v0_ref, v2_ref, g2_ref,
     scw_ref, cfbin_ref, cfw_ref, cfdb_ref, cflg_ref, cflb_ref) = (next(it) for _ in range(15))
    (r_ref, w_ref, k_ref, v_ref, a_ref, g_ref, xa_ref, xc_ref,
     nsh1_ref, nsh2_ref, nsc_ref, ncf_ref) = (next(it) for _ in range(12))
    s1buf, s2buf, cbuf, gbuf, pbuf = (next(it) for _ in range(5))

    @pl.when(pl.program_id(1) == 0)
    def _():
        s1buf[7:8, :] = sh1_ref[...]
        s2buf[7:8, :] = sh2_ref[...]
        cbuf[6:8, :] = sc0_ref[...]
        gbuf[2:CF_WIDTH + 1, :] = cf0_ref[...]
        gbuf[0:2, :] = jnp.zeros((2, D_CF), jnp.float32)
        gbuf[CF_WIDTH + 1 + tt:CF_WIDTH + 1 + tt + 8, :] = jnp.zeros((8, D_CF), jnp.float32)

    z1 = z1_ref[...]
    s1buf[8:8 + tt, :] = z1
    zm1 = z1 + (s1buf[7:7 + tt, :] - z1) * mu1_ref[...]
    z2l = z2_ref[:, Z2_LORA:Z_BLK]
    s2buf[8:8 + tt, :] = z2l
    zm2 = z2l + (s2buf[7:7 + tt, :] - z2l) * mu2_ref[...]
    last1 = s1buf[7 + tt:8 + tt, :]
    last2 = s2buf[7 + tt:8 + tt, :]
    nsh1_ref[...] = last1
    nsh2_ref[...] = last2
    s1buf[7:8, :] = last1
    s2buf[7:8, :] = last2

    x01 = zm2[:, 0:128]
    lw = _bdot(jnp.tanh(x01), w2_ref[...])
    la = _bdot(x01, a2_ref[...])
    w_ref[...] = -EXP_M05 * _sigmoid(w0_ref[...] + lw)
    a_ref[...] = _sigmoid(a0_ref[...] + la)
    g_ref[...] = _bdot(_sigmoid(zm2[:, 128:384]), g2_ref[...])
    r_ref[...] = zm1[:, 0:D_MODEL]
    k_ref[...] = zm1[:, D_MODEL:2 * D_MODEL]
    v = zm1[:, 2 * D_MODEL:3 * D_MODEL]
    if not first_layer:
        lv = _bdot(zm2[:, 128:256], v2_ref[...])
        v = v + (vf_ref[...] - v) * _sigmoid(v0_ref[...] + lv)
    v_ref[...] = v

    cu = z2_ref[:, Z2_SC + D_SC:Z2_SC + 2 * D_SC] * z2_ref[:, Z2_SC + 2 * D_SC:Z2_SC + 3 * D_SC]
    cbuf[8:8 + tt, :] = cu
    conv_a = (scw_ref[0:1, :] * cbuf[6:6 + tt, :] + scw_ref[1:2, :] * cbuf[7:7 + tt, :]
              + scw_ref[2:3, :] * cu)
    xa_ref[...] = z2_ref[:, Z2_SC:Z2_SC + D_SC] * conv_a
    tail_a = cbuf[6 + tt:8 + tt, :]
    nsc_ref[...] = tail_a
    cbuf[6:8, :] = tail_a

    zc = z2_ref[:, Z2_CF:Z2_CF + 2 * D_CF] + cfbin_ref[...]
    glu = zc[:, 0:D_CF] * _sigmoid(zc[:, D_CF:2 * D_CF])
    gbuf[CF_WIDTH + 1:CF_WIDTH + 1 + tt, :] = glu
    if tt % 8 == 0:
        acc = None
        for ph in range(8):
            part = None
            for j in range(ph, CF_WIDTH + 2, 8):
                if j < 2:
                    continue
                term = cfw_ref[j - 2:j - 1, :] * gbuf[j - ph:j - ph + tt + 8, :]
                part = term if part is None else part + term
            pbuf[...] = part
            sh = pbuf[ph:ph + tt, :]
            acc = sh if acc is None else acc + sh
    else:
        acc = cfw_ref[0:1, :] * gbuf[2:2 + tt, :]
        for kk in range(1, CF_WIDTH):
            acc = acc + cfw_ref[kk:kk + 1, :] * gbuf[2 + kk:2 + kk + tt, :]
    cc = acc + cfdb_ref[...]
    mu = jnp.mean(cc, axis=-1, keepdims=True)
    cd = cc - mu
    var = jnp.mean(cd * cd, axis=-1, keepdims=True)
    ln = cd * lax.rsqrt(var + LN_EPS) * cflg_ref[...] + cflb_ref[...]
    xc_ref[...] = ln * _sigmoid(ln)
    tail_c = gbuf[2 + tt:CF_WIDTH + 1 + tt, :]
    ncf_ref[...] = tail_c
    gbuf[2:CF_WIDTH + 1, :] = tail_c


def _seq(z, sh1, sh2, sc0, cf0, vfirst, P, li, tt, nbs):
    bsz, tlen, _ = z.shape
    first_layer = vfirst is None
    tok = lambda width, blk: pl.BlockSpec((nbs, tt, width), lambda b, t, blk=blk: (b, t, blk))
    st = lambda rows, width: pl.BlockSpec((nbs, rows, width), lambda b, t: (b, 0, 0))
    lst = lambda rows, width: pl.BlockSpec((None, nbs, rows, width), lambda b, t: (li, b, 0, 0))
    par = lambda a: _lpar(a, li)
    params = [P[n] for n in ('mu1', 'mu2', 'w0', 'w2p', 'a0', 'a2p', 'v0', 'v2p', 'g2p',
                              'scw', 'cfbin', 'cfw', 'cfdb', 'cflg', 'cflb')]
    args = [z, z, sh1, sh2, sc0, cf0]
    in_specs = [tok(Z_BLK, 1), tok(Z_BLK, 2), lst(1, Z_BLK), lst(1, LORA_PAD),
                lst(SC_WIDTH - 1, D_SC), lst(CF_WIDTH - 1, D_CF)]
    if not first_layer:
        args.append(vfirst)
        in_specs.append(tok(D_MODEL, 0))
    args += params
    in_specs += [par(a) for a in params]
    f32 = jnp.float32
    out_shape = ([jax.ShapeDtypeStruct((bsz, tlen, D_MODEL), f32)] * 6
                 + [jax.ShapeDtypeStruct((bsz, tlen, D_SC), f32), jax.ShapeDtypeStruct((bsz, tlen, D_CF), f32),
                    jax.ShapeDtypeStruct((bsz, 1, Z_BLK), f32), jax.ShapeDtypeStruct((bsz, 1, LORA_PAD), f32),
                    jax.ShapeDtypeStruct((bsz, SC_WIDTH - 1, D_SC), f32),
                    jax.ShapeDtypeStruct((bsz, CF_WIDTH - 1, D_CF), f32)])
    out_specs = ([tok(D_MODEL, 0)] * 6 + [tok(D_SC, 0), tok(D_CF, 0), st(1, Z_BLK), st(1, LORA_PAD),
                                           st(SC_WIDTH - 1, D_SC), st(CF_WIDTH - 1, D_CF)])
    scratch = [pltpu.VMEM((nbs, 8 + tt, Z_BLK), f32), pltpu.VMEM((nbs, 8 + tt, LORA_PAD), f32),
               pltpu.VMEM((nbs, 8 + tt, D_SC), f32), pltpu.VMEM((nbs, CF_WIDTH + 1 + tt + 8, D_CF), f32),
               pltpu.VMEM((nbs, tt + 8, D_CF), f32)]
    return pl.pallas_call(
        functools.partial(_seq_kernel, first_layer, tt, nbs),
        grid=(bsz // nbs, tlen // tt),
        in_specs=in_specs, out_specs=out_specs, out_shape=out_shape, scratch_shapes=scratch,
        compiler_params=_cparams(("arbitrary", "arbitrary")),
        name="seq",
    )(*args)


CHUNK = 8
PAIR = 2 * HEAD_DIM
N_PAIR = HEADS // 2
SEG = 256
P_ROWS = 4 * CHUNK * CHUNK - 2 * CHUNK
R_OFF = [32 * s + 16 if s < CHUNK - 1 else 32 * s for s in range(CHUNK)]


def _segsum(x, ones):
    xb = x.astype(jnp.bfloat16)
    return jnp.concatenate([jnp.dot(xb[:, q * SEG:(q + 1) * SEG], ones, preferred_element_type=jnp.float32)
                            for q in range(D_MODEL // SEG)], axis=1)


def _wkv_kernel(nb, tt, r_ref, ld_ref, k_ref, v_ref, a_ref, s0_ref, kkp_ref, kap_ref, rkp_ref, lng_ref, lnb_ref,
                ones_ref, y_ref, sout_ref, s_scr, kk_scr, km_scr, o_scr, p_scr, e_scr):
    f32, bf16 = jnp.float32, jnp.bfloat16
    ones = ones_ref[...]

    @pl.when(pl.program_id(1) == 0)
    def _():
        for bb in range(nb):
            for p in range(N_PAIR):
                s_scr[bb, p] = jnp.concatenate([s0_ref[bb, 2 * p], s0_ref[bb, 2 * p + 1]], axis=1)

    for bb in range(nb):
        kraw = k_ref[bb]
        kkp = kraw * kkp_ref[...]
        nrm = jnp.sqrt(_segsum(kkp * kkp, ones))
        kk_scr[bb] = kkp / jnp.maximum(nrm, 1e-12)
        km_scr[bb] = kraw * (1.0 + (a_ref[bb] - 1.0) * kap_ref[...])

    row = lax.broadcasted_iota(jnp.int32, (CHUNK, D_MODEL), 0)
    head0 = (lax.broadcasted_iota(jnp.int32, (2 * CHUNK, D_MODEL), 1) // HEAD_DIM) % 2 == 0
    bcast = lambda x, s: jnp.broadcast_to(x[s:s + 1, :], (CHUNK, D_MODEL))

    def chunk(c, carry):
        rows = pl.ds(pl.multiple_of(c * CHUNK, CHUNK), CHUNK)
        st = []
        for bb in range(nb):
            r = r_ref[bb, rows, :]
            ld = ld_ref[bb, rows, :]
            v = v_ref[bb, rows, :]
            kk = kk_scr[bb, rows, :]
            cum = ld
            for sh in (1, 2, 4):
                cum = cum + jnp.where(row >= sh, pltpu.roll(cum, sh, axis=0), 0.0)
            w_cum = jnp.exp(cum)
            w_inv = jnp.exp(-cum)
            at = -kk * jnp.exp(cum - ld)
            rt = r * w_cum
            bt = kk * a_ref[bb, rows, :] * w_inv
            kt = km_scr[bb, rows, :] * w_inv
            wc = bcast(w_cum, CHUNK - 1)
            lhs = jnp.concatenate([at, rt], axis=0)
            lhs = jnp.concatenate([jnp.where(head0, lhs, 0.0), jnp.where(head0, 0.0, lhs)], axis=0).astype(bf16)
            st.append([v, at, rt, bt, kt, wc, lhs])
        for bb in range(nb):
            v, at, rt, bt, kt, wc, lhs = st[bb]
            xy = []
            for p in range(N_PAIR):
                sl = slice(p * PAIR, (p + 1) * PAIR)
                xh = lax.dot_general(lhs[:, sl], s_scr[bb, p].astype(bf16), (((1,), (1,)), ((), ())),
                                     preferred_element_type=f32)
                xy.append(xh[0:2 * CHUNK])
                xy.append(xh[2 * CHUNK:4 * CHUNK])
            st[bb][6] = jnp.concatenate(xy, axis=1)
            for s in range(CHUNK):
                bs, ks = bcast(bt, s), bcast(kt, s)
                rm = jnp.where(row >= s, rt, 0.0) if s else rt
                if s < CHUNK - 1:
                    am = jnp.where(row > s, at, 0.0)
                    p_scr[bb, 32 * s:32 * s + 16, :] = jnp.concatenate([bs * am, ks * am], axis=0).astype(bf16)
                p_scr[bb, R_OFF[s]:R_OFF[s] + 16, :] = jnp.concatenate([bs * rm, ks * rm], axis=0).astype(bf16)
            for q in range(D_MODEL // SEG):
                sl = slice(q * SEG, (q + 1) * SEG)
                e_scr[bb, :, sl] = jnp.dot(p_scr[bb, :, sl], ones, preferred_element_type=f32)
        accs = []
        for bb in range(nb):
            v, at, rt, bt, kt, wc, xy = st[bb]
            pre_u = pre_o = None
            for s in range(CHUNK):
                vs = bcast(v, s)
                to = e_scr[bb, R_OFF[s] + 8:R_OFF[s] + 16, :] * vs
                pre_o = to if pre_o is None else pre_o + to
                if s < CHUNK - 1:
                    tu = e_scr[bb, 32 * s + 8:32 * s + 16, :] * vs
                    pre_u = tu if pre_u is None else pre_u + tu
            acc_u = xy[0:CHUNK] + pre_u
            acc_o = xy[CHUNK:2 * CHUNK] + pre_o
            for s in range(CHUNK):
                us = bcast(acc_u, s)
                if s < CHUNK - 1:
                    acc_u = acc_u + e_scr[bb, 32 * s:32 * s + 8, :] * us
                acc_o = acc_o + e_scr[bb, R_OFF[s]:R_OFF[s] + 8, :] * us
            o_scr[bb, rows, :] = acc_o
            accs.append(acc_u)
        for bb in range(nb):
            v, at, rt, bt, kt, wc, xy = st[bb]
            uv = jnp.concatenate([accs[bb], v], axis=0)
            uv_swapped = pltpu.roll(uv, D_MODEL - HEAD_DIM, axis=1)
            uv = jnp.concatenate([jnp.where(head0, uv, 0.0), jnp.where(head0, uv_swapped, 0.0)], axis=0).astype(bf16)
            bk = jnp.concatenate([bt * wc, kt * wc], axis=0)
            bk = jnp.concatenate([jnp.where(head0, bk, 0.0), jnp.where(head0, 0.0, bk)], axis=0).astype(bf16)
            for p in range(N_PAIR):
                sl = slice(p * PAIR, (p + 1) * PAIR)
                upd = lax.dot_general(uv[:, p * PAIR:p * PAIR + HEAD_DIM], bk[:, sl], (((0,), (0,)), ((), ())),
                                      preferred_element_type=f32)
                s_scr[bb, p] = s_scr[bb, p] * wc[0:1, sl] + upd
        return carry

    lax.fori_loop(0, tt // CHUNK, chunk, 0)

    for bb in range(nb):
        o = o_scr[bb]
        mu = _segsum(o, ones) * (1.0 / HEAD_DIM)
        od = o - mu
        var = _segsum(od * od, ones) * (1.0 / HEAD_DIM)
        y = od * lax.rsqrt(var + GN_EPS) * lng_ref[...] + lnb_ref[...]
        bonus = _segsum(r_ref[bb] * km_scr[bb] * rkp_ref[...], ones)
        y_ref[bb] = y + bonus * v_ref[bb]

    @pl.when(pl.program_id(1) == pl.num_programs(1) - 1)
    def _():
        for bb in range(nb):
            for p in range(N_PAIR):
                sp = s_scr[bb, p]
                sout_ref[bb, 2 * p] = sp[:, 0:HEAD_DIM]
                sout_ref[bb, 2 * p + 1] = sp[:, HEAD_DIM:PAIR]


def _wkv(r, ld, k, v, a, s0, P, li, nb, tt):
    bsz, tlen, _ = r.shape
    f32 = jnp.float32
    tok = pl.BlockSpec((nb, tt, D_MODEL), lambda b, t: (b, t, 0))
    stt = pl.BlockSpec((nb, HEADS, HEAD_DIM, HEAD_DIM), lambda b, t: (b, 0, 0, 0))
    lstt = pl.BlockSpec((None, nb, HEADS, HEAD_DIM, HEAD_DIM), lambda b, t: (li, b, 0, 0, 0))
    par = _lpar(P['kk'], li)
    return pl.pallas_call(
        functools.partial(_wkv_kernel, nb, tt),
        grid=(bsz // nb, tlen // tt),
        in_specs=[tok] * 5 + [lstt] + [par] * 5 + [pl.BlockSpec((SEG, SEG), lambda b, t: (0, 0))],
        out_specs=[tok, stt],
        out_shape=[jax.ShapeDtypeStruct((bsz, tlen, D_MODEL), f32),
                   jax.ShapeDtypeStruct((bsz, HEADS, HEAD_DIM, HEAD_DIM), f32)],
        scratch_shapes=[pltpu.VMEM((nb, N_PAIR, HEAD_DIM, PAIR), f32),
                        pltpu.VMEM((nb, tt, D_MODEL), f32), pltpu.VMEM((nb, tt, D_MODEL), f32),
                        pltpu.VMEM((nb, tt, D_MODEL), f32),
                        pltpu.VMEM((nb, P_ROWS, D_MODEL), jnp.bfloat16),
                        pltpu.VMEM((nb, P_ROWS, D_MODEL), f32)],
        compiler_params=_cparams(("arbitrary", "arbitrary")),
        name="wkv",
    )(r, ld, k, v, a, s0, P['kk'], P['ka'], P['rk'], P['lng'], P['lnb'], P['ones'])


LANES = 128


def _wkv_lanes_kernel(tlen, r_ref, ld_ref, k_ref, v_ref, a_ref, s0_ref, kkp_ref, kap_ref, rkp_ref, lng_ref, lnb_ref,
                      y_ref, s_ref, o_scr):
    s_ref[...] = s0_ref[...]
    for t in range(tlen):
        rt = r_ref[t]
        wt = jnp.exp(ld_ref[t])
        kraw = k_ref[t]
        at = a_ref[t]
        kkp = kraw * kkp_ref[...]
        nrm = jnp.sqrt(jnp.sum(kkp * kkp, axis=0, keepdims=True))
        kk = kkp / jnp.maximum(nrm, 1e-12)
        kt = kraw * (1.0 + (at - 1.0) * kap_ref[...])
        bt = kk * at
        nkk = -kk
        for i in range(HEAD_DIM):
            si = s_ref[i]
            sa = jnp.sum(si * nkk, axis=0, keepdims=True)
            sn = si * wt + sa * bt + v_ref[t, i:i + 1, :] * kt
            s_ref[i] = sn
            o_scr[i:i + 1, :] = jnp.sum(sn * rt, axis=0, keepdims=True)
        o = o_scr[...]
        mu = jnp.mean(o, axis=0, keepdims=True)
        od = o - mu
        var = jnp.mean(od * od, axis=0, keepdims=True)
        y = od * lax.rsqrt(var + GN_EPS) * lng_ref[...] + lnb_ref[...]
        bonus = jnp.sum(rt * kt * rkp_ref[...], axis=0, keepdims=True)
        y_ref[t] = y + bonus * v_ref[t]


def _wkv_lanes(r, ld, k, v, a, s0, P, li):
    tlen, _, _, bsz = r.shape
    f32 = jnp.float32
    tok = pl.BlockSpec((tlen, None, HEAD_DIM, bsz), lambda h: (0, h, 0, 0))
    lstt = pl.BlockSpec((None, None, HEAD_DIM, HEAD_DIM, bsz), lambda h: (li, h, 0, 0, 0))
    stt = pl.BlockSpec((None, HEAD_DIM, HEAD_DIM, bsz), lambda h: (h, 0, 0, 0))
    par = pl.BlockSpec((None, None, HEAD_DIM, bsz), lambda h: (li, h, 0, 0))
    hp = P['head_lanes']
    return pl.pallas_call(
        functools.partial(_wkv_lanes_kernel, tlen),
        grid=(HEADS,),
        in_specs=[tok] * 5 + [lstt] + [par] * 5,
        out_specs=[tok, stt],
        out_shape=[jax.ShapeDtypeStruct((tlen, HEADS, HEAD_DIM, bsz), f32),
                   jax.ShapeDtypeStruct((HEADS, HEAD_DIM, HEAD_DIM, bsz), f32)],
        scratch_shapes=[pltpu.VMEM((HEAD_DIM, bsz), f32)],
        compiler_params=_cparams(("arbitrary",)),
        name="wkv_lanes",
    )(r, ld, k, v, a, s0, hp['kk'], hp['ka'], hp['rk'], hp['lng'], hp['lnb'])


def _merge_kernel(h_ref, xa_ref, yw_ref, g_ref, xc_ref, zg_ref, wa_ref, wb_ref, wc_ref, bc_ref, wo_ref, o_ref):
    ya = _bdot(xa_ref[...], wa_ref[...])
    yb = _bdot(yw_ref[...] * g_ref[...], wb_ref[...])
    yc = _bdot(xc_ref[...], wc_ref[...]) + bc_ref[...]
    m = (_sigmoid(zg_ref[:, 0:D_MODEL]) * ya + _sigmoid(zg_ref[:, D_MODEL:2 * D_MODEL]) * yb
         + _sigmoid(zg_ref[:, 2 * D_MODEL:3 * D_MODEL]) * yc)
    o_ref[...] = h_ref[...] + _bdot(m, wo_ref[...])


def _merge(h, xa, yw, g, xc, z, P, li, tm):
    n = h.shape[0]
    tok = lambda width: pl.BlockSpec((tm, width), lambda i: (i, 0))
    par = lambda a: _lpar(a, li)
    ws = [P[n_] for n_ in ('sc_w_out', 'rk_w_out', 'cf_w_out', 'cf_b_out', 'w_o')]
    return pl.pallas_call(
        _merge_kernel,
        grid=(n // tm,),
        in_specs=[tok(D_MODEL), tok(D_SC), tok(D_MODEL), tok(D_MODEL), tok(D_CF), tok(Z_BLK)] + [par(a) for a in ws],
        out_specs=tok(D_MODEL),
        out_shape=jax.ShapeDtypeStruct((n, D_MODEL), jnp.float32),
        compiler_params=_cparams(("arbitrary",)),
        name="merge",
    )(h, xa, yw, g, xc, z, *ws)


def _mlp_ple_kernel(nk, final, h_ref, p_ref, g2_ref, w1_ref, w2_ref, gp_ref, wg_ref, wp_ref, *rest):
    gf_ref = rest[0] if final else None
    o_ref, hn_scr, acc_scr = rest[-3:]
    kstep = pl.program_id(1)

    @pl.when(kstep == 0)
    def _():
        hn_scr[...] = _rms(h_ref[...], g2_ref[...]).astype(jnp.bfloat16)
        acc_scr[...] = jnp.zeros_like(acc_scr)

    u = jnp.dot(hn_scr[...], w1_ref[...], preferred_element_type=jnp.float32)
    u = jnp.maximum(u, 0.0)
    acc_scr[...] += _bdot(u * u, w2_ref[...])

    @pl.when(kstep == nk - 1)
    def _():
        h1 = h_ref[...] + acc_scr[...]
        gate = _sigmoid(_bdot(_rms(h1, gp_ref[...]), wg_ref[...]))
        h2 = h1 + gate * _bdot(p_ref[...], wp_ref[...])
        o_ref[...] = h2 if gf_ref is None else _rms(h2, gf_ref[...])


def _mlp_ple(h, p, P, li, tm, tf, final_g=None):
    n = h.shape[0]
    nk = D_FF // tf
    par = lambda a: _lpar(a, li)
    extra = [] if final_g is None else [final_g]
    return pl.pallas_call(
        functools.partial(_mlp_ple_kernel, nk, final_g is not None),
        grid=(n // tm, nk),
        in_specs=[pl.BlockSpec((tm, D_MODEL), lambda i, k: (i, 0)),
                  pl.BlockSpec((None, tm, D_PLE), lambda i, k: (li, i, 0)),
                  par(P['norm2_g']),
                  pl.BlockSpec((None, D_MODEL, tf), lambda i, k: (li, 0, k)),
                  pl.BlockSpec((None, tf, D_MODEL), lambda i, k: (li, k, 0)),
                  par(P['ple_norm_g']), par(P['ple_gate_w']), par(P['ple_w'])]
                 + [pl.BlockSpec((1, D_MODEL), lambda i, k: (0, 0)) for _ in extra],
        out_specs=pl.BlockSpec((tm, D_MODEL), lambda i, k: (i, 0)),
        out_shape=jax.ShapeDtypeStruct((n, D_MODEL), jnp.float32),
        scratch_shapes=[pltpu.VMEM((tm, D_MODEL), jnp.bfloat16), pltpu.VMEM((tm, D_MODEL), jnp.float32)],
        compiler_params=_cparams(("arbitrary", "arbitrary")),
        name="mlp_ple",
    )(h, p, P['norm2_g'], P['mlp_w1'], P['mlp_w2'], P['ple_norm_g'], P['ple_gate_w'], P['ple_w'], *extra)


def _prep(W):
    bf = jnp.bfloat16
    nl = W['w_in'].shape[0]
    row = lambda a: a.reshape(nl, 1, -1)
    w_in = W['w_in']
    n_sc, n_rk, n_cf = 3 * D_SC, RK_PROJ, 2 * D_CF
    o_rk, o_cf, o_gate = n_sc, n_sc + n_rk, n_sc + n_rk + n_cf
    w_perm = jnp.concatenate([
        w_in[:, :, o_gate:o_gate + 3 * D_MODEL], w_in[:, :, o_rk:o_rk + 3 * D_MODEL], w_in[:, :, 0:n_sc],
        w_in[:, :, o_cf:o_cf + n_cf], w_in[:, :, o_rk + 3 * D_MODEL:o_rk + RK_PROJ],
        jnp.zeros((nl, D_MODEL, LORA_PAD - N_LORA), w_in.dtype)], axis=2).astype(bf)
    mu = W['rk_mu']
    zrows = lambda n: jnp.zeros((nl, n, D_MODEL), jnp.float32)
    o_a, o_g = DECAY_LORA, DECAY_LORA + AAA_LORA + MV_LORA
    seg_head = jnp.arange(SEG) // HEAD_DIM
    on_lanes = lambda a: jnp.broadcast_to(a.reshape(nl, HEADS, HEAD_DIM, 1), (nl, HEADS, HEAD_DIM, LANES))
    return {
        'head_lanes': {'kk': on_lanes(W['rk_k_k']), 'ka': on_lanes(W['rk_k_a']), 'rk': on_lanes(W['rk_r_k']),
                       'lng': on_lanes(W['rk_ln_g']), 'lnb': on_lanes(W['rk_ln_b'])},
        'ones': (seg_head[:, None] == seg_head[None, :]).astype(bf),
        'kk': row(W['rk_k_k']), 'ka': row(W['rk_k_a']), 'rk': row(W['rk_r_k']),
        'lng': row(W['rk_ln_g']), 'lnb': row(W['rk_ln_b']),
        'norm1_g': row(W['norm1_g']), 'w_in': w_perm,
        'mu1': row(mu[:, :3 * D_MODEL]),
        'mu2': row(jnp.pad(mu[:, 3 * D_MODEL:], ((0, 0), (0, LORA_PAD - N_LORA)))),
        'w0': row(W['rk_w0']), 'a0': row(W['rk_a0']), 'v0': row(W['rk_v0']),
        'w2p': jnp.concatenate([W['rk_w2'], zrows(128 - o_a)], axis=1).astype(bf),
        'a2p': jnp.concatenate([zrows(o_a), W['rk_a2']], axis=1).astype(bf),
        'v2p': jnp.concatenate([W['rk_v2'], zrows(128 - MV_LORA)], axis=1).astype(bf),
        'g2p': jnp.concatenate([zrows(o_g - 128), W['rk_g2'], zrows(384 - N_LORA)], axis=1).astype(bf),
        'scw': W['sc_conv_w'], 'cfbin': row(W['cf_b_in']), 'cfw': W['cf_dw_w'],
        'cfdb': row(W['cf_dw_b']), 'cflg': row(W['cf_ln_g']), 'cflb': row(W['cf_ln_b']),
        'sc_w_out': W['sc_w_out'].astype(bf), 'rk_w_out': W['rk_w_out'].astype(bf),
        'cf_w_out': W['cf_w_out'].astype(bf), 'cf_b_out': row(W['cf_b_out']), 'w_o': W['w_o'].astype(bf),
        'norm2_g': row(W['norm2_g']), 'mlp_w1': W['mlp_w1'].astype(bf), 'mlp_w2': W['mlp_w2'].astype(bf),
        'ple_norm_g': row(W['ple_norm_g']), 'ple_gate_w': W['ple_gate_w'].astype(bf),
        'ple_w': W['ple_w'].astype(bf),
    }


def _pick(n, pref):
    return pref if n % pref == 0 else n


def _trunk(x, p, sc0, shift0, wkv0, cf0, P, final_g, cfg):
    bsz, tlen, _ = x.shape
    n = bsz * tlen
    tm = _pick(n, cfg['tm'])
    tt = _pick(tlen, cfg['tt'])
    tpad = -tlen % CHUNK
    tw = _pick(tlen + tpad, cfg['tw'])
    pad_t = lambda a: jnp.pad(a, ((0, 0), (0, tpad), (0, 0))) if tpad else a
    h = x.reshape(n, D_MODEL)
    pf = p.reshape(DEPTH, n, D_PLE)
    sh1 = shift0[:, :, None, :3 * D_MODEL]
    sh2 = jnp.pad(shift0[:, :, 3 * D_MODEL:], ((0, 0), (0, 0), (0, LORA_PAD - N_LORA)))[:, :, None, :]
    lanes_form = bsz == LANES and tlen <= CHUNK
    to_lanes = lambda a: a.reshape(bsz, tlen, HEADS, HEAD_DIM).transpose(1, 2, 3, 0)
    wkv0_l = wkv0.transpose(0, 2, 3, 4, 1) if lanes_form else None
    v_first = None
    new_sc, new_shift, new_wkv, new_cf = [], [], [], []
    for i in range(DEPTH):
        z = _in_proj(h, P, i, tm)
        (r, w, k, v, a, g, xa, xc, nsh1, nsh2, nsc, ncf) = _seq(
            z.reshape(bsz, tlen, NZ), sh1, sh2, sc0, cf0, v_first, P, i, tt, cfg['nbs'])
        if i == 0:
            v_first = v
        if lanes_form:
            y, s_out = _wkv_lanes(to_lanes(r), to_lanes(w), to_lanes(k), to_lanes(v), to_lanes(a), wkv0_l, P, i)
            yw = y.transpose(3, 0, 1, 2).reshape(n, D_MODEL)
        else:
            y, s_out = _wkv(pad_t(r), pad_t(w), pad_t(k), pad_t(v), pad_t(a), wkv0, P, i, cfg['nb'], tw)
            yw = y[:, :tlen].reshape(n, D_MODEL)
        h = _merge(h, xa.reshape(n, D_SC), yw, g.reshape(n, D_MODEL), xc.reshape(n, D_CF), z, P, i, tm)
        h = _mlp_ple(h, pf, P, i, tm, cfg['tf'], final_g if i == DEPTH - 1 else None)
        new_sc.append(nsc)
        new_shift.append(jnp.concatenate([nsh1[:, 0, :], nsh2[:, 0, :N_LORA]], axis=-1))
        new_wkv.append(s_out)
        new_cf.append(ncf)
    y = h.reshape(bsz, tlen, D_MODEL)
    wkv_out = jnp.stack(new_wkv)
    if lanes_form:
        wkv_out = wkv_out.transpose(0, 4, 1, 2, 3)
    return y, jnp.stack(new_sc), jnp.stack(new_shift), wkv_out, jnp.stack(new_cf)


PROMPT_CFG = {'tm': 512, 'tt': 128, 'nbs': 1, 'tw': 128, 'nb': 4, 'tf': 2048}
SAMPLE_CFG = {'tm': 512, 'tt': 4, 'nbs': 8, 'tw': 8, 'nb': 8, 'tf': 1024}


def kernel(x_prompt, x_sample, p_prompt, p_sample, state_sconv, state_shift, state_wkv, state_cconv,
           norm1_g, w_in, sc_conv_w, sc_w_out, rk_mu, rk_w0, rk_w2, rk_a0, rk_a2, rk_v0, rk_v2, rk_g2,
           rk_k_k, rk_k_a, rk_r_k, rk_ln_g, rk_ln_b, rk_w_out, cf_b_in, cf_dw_w, cf_dw_b, cf_ln_g, cf_ln_b,
           cf_w_out, cf_b_out, w_o, norm2_g, mlp_w1, mlp_w2, ple_w, ple_gate_w, ple_norm_g, final_norm_g):
    W = dict(norm1_g=norm1_g, w_in=w_in, sc_conv_w=sc_conv_w, sc_w_out=sc_w_out, rk_mu=rk_mu, rk_w0=rk_w0,
             rk_w2=rk_w2, rk_a0=rk_a0, rk_a2=rk_a2, rk_v0=rk_v0, rk_v2=rk_v2, rk_g2=rk_g2, rk_k_k=rk_k_k,
             rk_k_a=rk_k_a, rk_r_k=rk_r_k, rk_ln_g=rk_ln_g, rk_ln_b=rk_ln_b, rk_w_out=rk_w_out, cf_b_in=cf_b_in,
             cf_dw_w=cf_dw_w, cf_dw_b=cf_dw_b, cf_ln_g=cf_ln_g, cf_ln_b=cf_ln_b, cf_w_out=cf_w_out,
             cf_b_out=cf_b_out, w_o=w_o, norm2_g=norm2_g, mlp_w1=mlp_w1, mlp_w2=mlp_w2, ple_w=ple_w,
             ple_gate_w=ple_gate_w, ple_norm_g=ple_norm_g)
    P = _prep(W)
    final_g = final_norm_g.reshape(1, D_MODEL)
    bp = x_prompt.shape[0]
    f32 = jnp.float32
    zsc = jnp.zeros((DEPTH, bp, SC_WIDTH - 1, D_SC), f32)
    zsh = jnp.zeros((DEPTH, bp, RK_PROJ), f32)
    zwkv = jnp.zeros((DEPTH, bp, HEADS, HEAD_DIM, HEAD_DIM), f32)
    zcf = jnp.zeros((DEPTH, bp, CF_WIDTH - 1, D_CF), f32)
    y_p, sc_p, sh_p, wkv_p, cf_p = _trunk(x_prompt, p_prompt, zsc, zsh, zwkv, zcf, P, final_g, PROMPT_CFG)
    y_s, sc_s, sh_s, wkv_s, cf_s = _trunk(x_sample, p_sample, state_sconv, state_shift, state_wkv, state_cconv,
                                          P, final_g, SAMPLE_CFG)
    return (y_p, y_s, sc_p, sh_p, wkv_p, cf_p, sc_s, sh_s, wkv_s, cf_s)
```

```python
import functools

import jax
import jax.numpy as jnp
from jax import lax
from jax.experimental import pallas as pl
from jax.experimental.pallas import tpu as pltpu

D_MODEL = 1024
DEPTH = 4
D_SC = 512
D_CF = 512
SC_WIDTH = 3
CF_WIDTH = 31
HEAD_DIM = 64
HEADS = 16
D_FF = 4096
D_PLE = 256
DECAY_LORA, AAA_LORA, MV_LORA, GATE_LORA = 64, 64, 32, 128
N_LORA = DECAY_LORA + AAA_LORA + MV_LORA + GATE_LORA
RK_PROJ = 3 * D_MODEL + N_LORA
RMS_EPS = 1e-6
LN_EPS = 1e-5
GN_EPS = 64e-5
EXP_M05 = 0.6065306597126334

Z_BLK = 3 * D_MODEL
NZ = 3 * Z_BLK
Z2_SC = 0
Z2_CF = 3 * D_SC
Z2_LORA = Z2_CF + 2 * D_CF
LORA_PAD = Z_BLK - Z2_LORA

VMEM_LIMIT_BYTES = 56 * 1024 * 1024


def _cparams(sem):
    return pltpu.CompilerParams(dimension_semantics=sem, vmem_limit_bytes=VMEM_LIMIT_BYTES)


def _sigmoid(x):
    return 1.0 / (1.0 + jnp.exp(-x))


def _rms(x, g):
    return x * lax.rsqrt(jnp.mean(x * x, axis=-1, keepdims=True) + RMS_EPS) * g


def _bdot(a, b):
    return jnp.dot(a.astype(jnp.bfloat16), b, preferred_element_type=jnp.float32)


def _in_proj_kernel(x_ref, g_ref, w_ref, o_ref):
    o_ref[...] = _bdot(_rms(x_ref[...], g_ref[...]), w_ref[...])


def _lpar(a, li):
    return pl.BlockSpec((None,) + a.shape[1:], lambda *_: (li,) + (0,) * (a.ndim - 1))


def _in_proj(x, P, li, tm):
    n = x.shape[0]
    g, w = P['norm1_g'], P['w_in']
    return pl.pallas_call(
        _in_proj_kernel,
        grid=(NZ // Z_BLK, n // tm),
        in_specs=[
            pl.BlockSpec((tm, D_MODEL), lambda j, i: (i, 0)),
            _lpar(g, li),
            pl.BlockSpec((None, D_MODEL, Z_BLK), lambda j, i: (li, 0, j)),
        ],
        out_specs=pl.BlockSpec((tm, Z_BLK), lambda j, i: (i, j)),
        out_shape=jax.ShapeDtypeStruct((n, NZ), jnp.float32),
        compiler_params=_cparams(("arbitrary", "arbitrary")),
        name="in_proj",
    )(x, g, w)


def _seq_batched(first_layer):
    n_in = 6 if first_layer else 7
    n_par = 15
    return set(range(n_in)) | set(range(n_in + n_par, n_in + n_par + 12 + 5))


SEQ_BATCHED = {True: _seq_batched(True), False: _seq_batched(False)}


def _seq_kernel(first_layer, tt, nbs, *refs):
    for bb in range(nbs):
        _seq_body(first_layer, tt, [r.at[bb] if i in SEQ_BATCHED[first_layer] else r for i, r in enumerate(refs)])


def _seq_body(first_layer, tt, refs):
    it = iter(refs)
    z1_ref, z2_ref, sh1_ref, sh2_ref, sc0_ref, cf0_ref = (next(it) for _ in range(6))
    vf_ref = None if first_layer else next(it)
    (mu1_ref, mu2_ref, w0_ref, w2_ref, a0_ref, a2_ref, v0_ref, v2_ref, g2_ref,
     scw_ref, cfbin_ref, cfw_ref, cfdb_ref, cflg_ref, cflb_ref) = (next(it) for _ in range(15))
    (r_ref, w_ref, k_ref, v_ref, a_ref, g_ref, xa_ref, xc_ref,
     nsh1_ref, nsh2_ref, nsc_ref, ncf_ref) = (next(it) for _ in range(12))
    s1buf, s2buf, cbuf, gbuf, pbuf = (next(it) for _ in range(5))

    @pl.when(pl.program_id(1) == 0)
    def _():
        s1buf[7:8, :] = sh1_ref[...]
        s2buf[7:8, :] = sh2_ref[...]
        cbuf[6:8, :] = sc0_ref[...]
        gbuf[2:CF_WIDTH + 1, :] = cf0_ref[...]
        gbuf[0:2, :] = jnp.zeros((2, D_CF), jnp.float32)
        gbuf[CF_WIDTH + 1 + tt:CF_WIDTH + 1 + tt + 8, :] = jnp.zeros((8, D_CF), jnp.float32)

    z1 = z1_ref[...]
    if tt % 8 == 0:
        rowi = lax.broadcasted_iota(jnp.int32, (tt, Z_BLK), 0)
        zp1 = jnp.where(rowi == 0, jnp.broadcast_to(s1buf[7:8, :], (tt, Z_BLK)), pltpu.roll(z1, 1, axis=0))
        s1buf[7 + tt:8 + tt, :] = z1[tt - 1:tt, :]
    else:
        s1buf[8:8 + tt, :] = z1
        zp1 = s1buf[7:7 + tt, :]
    zm1 = z1 + (zp1 - z1) * mu1_ref[...]
    z2l = z2_ref[:, Z2_LORA:Z_BLK]
    if tt % 8 == 0:
        rowl = lax.broadcasted_iota(jnp.int32, (tt, LORA_PAD), 0)
        zp2 = jnp.where(rowl == 0, jnp.broadcast_to(s2buf[7:8, :], (tt, LORA_PAD)),
                        pltpu.roll(z2l, 1, axis=0))
        s2buf[7 + tt:8 + tt, :] = z2l[tt - 1:tt, :]
    else:
        s2buf[8:8 + tt, :] = z2l
        zp2 = s2buf[7:7 + tt, :]
    zm2 = z2l + (zp2 - z2l) * mu2_ref[...]
    last1 = s1buf[7 + tt:8 + tt, :]
    last2 = s2buf[7 + tt:8 + tt, :]
    nsh1_ref[...] = last1
    nsh2_ref[...] = last2
    s1buf[7:8, :] = last1
    s2buf[7:8, :] = last2

    x01 = zm2[:, 0:128]
    lw = _bdot(jnp.tanh(x01), w2_ref[...])
    la = _bdot(x01, a2_ref[...])
    w_ref[...] = -EXP_M05 * _sigmoid(w0_ref[...] + lw)
    a_ref[...] = _sigmoid(a0_ref[...] + la)
    g_ref[...] = _bdot(_sigmoid(zm2[:, 128:384]), g2_ref[...])
    r_ref[...] = zm1[:, 0:D_MODEL]
    k_ref[...] = zm1[:, D_MODEL:2 * D_MODEL]
    v = zm1[:, 2 * D_MODEL:3 * D_MODEL]
    if not first_layer:
        lv = _bdot(zm2[:, 128:256], v2_ref[...])
        v = v + (vf_ref[...] - v) * _sigmoid(v0_ref[...] + lv)
    v_ref[...] = v

    cu = z2_ref[:, Z2_SC + D_SC:Z2_SC + 2 * D_SC] * z2_ref[:, Z2_SC + 2 * D_SC:Z2_SC + 3 * D_SC]
    cbuf[8:8 + tt, :] = cu
    conv_a = (scw_ref[0:1, :] * cbuf[6:6 + tt, :] + scw_ref[1:2, :] * cbuf[7:7 + tt, :]
              + scw_ref[2:3, :] * cu)
    xa_ref[...] = z2_ref[:, Z2_SC:Z2_SC + D_SC] * conv_a
    tail_a = cbuf[6 + tt:8 + tt, :]
    nsc_ref[...] = tail_a
    cbuf[6:8, :] = tail_a

    zc = z2_ref[:, Z2_CF:Z2_CF + 2 * D_CF] + cfbin_ref[...]
    glu = zc[:, 0:D_CF] * _sigmoid(zc[:, D_CF:2 * D_CF])
    gbuf[CF_WIDTH + 1:CF_WIDTH + 1 + tt, :] = glu
    if tt % 8 == 0:
        acc = None
        for ph in range(8):
            part = None
            for j in range(ph, CF_WIDTH + 2, 8):
                if j < 2:
                    continue
                term = cfw_ref[j - 2:j - 1, :] * gbuf[j - ph:j - ph + tt + 8, :]
                part = term if part is None else part + term
            pbuf[...] = part
            sh = pbuf[ph:ph + tt, :]
            acc = sh if acc is None else acc + sh
    else:
        acc = cfw_ref[0:1, :] * gbuf[2:2 + tt, :]
        for kk in range(1, CF_WIDTH):
            acc = acc + cfw_ref[kk:kk + 1, :] * gbuf[2 + kk:2 + kk + tt, :]
    cc = acc + cfdb_ref[...]
    mu = jnp.mean(cc, axis=-1, keepdims=True)
    cd = cc - mu
    var = jnp.mean(cd * cd, axis=-1, keepdims=True)
    ln = cd * lax.rsqrt(var + LN_EPS) * cflg_ref[...] + cflb_ref[...]
    xc_ref[...] = ln * _sigmoid(ln)
    tail_c = gbuf[2 + tt:CF_WIDTH + 1 + tt, :]
    ncf_ref[...] = tail_c
    gbuf[2:CF_WIDTH + 1, :] = tail_c


def _seq(z, sh1, sh2, sc0, cf0, vfirst, P, li, tt, nbs):
    bsz, tlen, _ = z.shape
    first_layer = vfirst is None
    tok = lambda width, blk: pl.BlockSpec((nbs, tt, width), lambda b, t, blk=blk: (b, t, blk))
    st = lambda rows, width: pl.BlockSpec((nbs, rows, width), lambda b, t: (b, 0, 0))
    lst = lambda rows, width: pl.BlockSpec((None, nbs, rows, width), lambda b, t: (li, b, 0, 0))
    par = lambda a: _lpar(a, li)
    params = [P[n] for n in ('mu1', 'mu2', 'w0', 'w2p', 'a0', 'a2p', 'v0', 'v2p', 'g2p',
                              'scw', 'cfbin', 'cfw', 'cfdb', 'cflg', 'cflb')]
    args = [z, z, sh1, sh2, sc0, cf0]
    in_specs = [tok(Z_BLK, 1), tok(Z_BLK, 2), lst(1, Z_BLK), lst(1, LORA_PAD),
                lst(SC_WIDTH - 1, D_SC), lst(CF_WIDTH - 1, D_CF)]
    if not first_layer:
        args.append(vfirst)
        in_specs.append(tok(D_MODEL, 0))
    args += params
    in_specs += [par(a) for a in params]
    f32 = jnp.float32
    out_shape = ([jax.ShapeDtypeStruct((bsz, tlen, D_MODEL), f32)] * 6
                 + [jax.ShapeDtypeStruct((bsz, tlen, D_SC), f32), jax.ShapeDtypeStruct((bsz, tlen, D_CF), f32),
                    jax.ShapeDtypeStruct((bsz, 1, Z_BLK), f32), jax.ShapeDtypeStruct((bsz, 1, LORA_PAD), f32),
                    jax.ShapeDtypeStruct((bsz, SC_WIDTH - 1, D_SC), f32),
                    jax.ShapeDtypeStruct((bsz, CF_WIDTH - 1, D_CF), f32)])
    out_specs = ([tok(D_MODEL, 0)] * 6 + [tok(D_SC, 0), tok(D_CF, 0), st(1, Z_BLK), st(1, LORA_PAD),
                                           st(SC_WIDTH - 1, D_SC), st(CF_WIDTH - 1, D_CF)])
    scratch = [pltpu.VMEM((nbs, 8 + tt, Z_BLK), f32), pltpu.VMEM((nbs, 8 + tt, LORA_PAD), f32),
               pltpu.VMEM((nbs, 8 + tt, D_SC), f32), pltpu.VMEM((nbs, CF_WIDTH + 1 + tt + 8, D_CF), f32),
               pltpu.VMEM((nbs, tt + 8, D_CF), f32)]
    return pl.pallas_call(
        functools.partial(_seq_kernel, first_layer, tt, nbs),
        grid=(bsz // nbs, tlen // tt),
        in_specs=in_specs, out_specs=out_specs, out_shape=out_shape, scratch_shapes=scratch,
        compiler_params=_cparams(("arbitrary", "arbitrary")),
        name="seq",
    )(*args)


CHUNK = 8
PAIR = 2 * HEAD_DIM
N_PAIR = HEADS // 2
SEG = 256
P_ROWS = 4 * CHUNK * CHUNK - 2 * CHUNK
R_OFF = [32 * s + 16 if s < CHUNK - 1 else 32 * s for s in range(CHUNK)]


def _segsum(x, ones):
    xb = x.astype(jnp.bfloat16)
    return jnp.concatenate([jnp.dot(xb[:, q * SEG:(q + 1) * SEG], ones, preferred_element_type=jnp.float32)
                            for q in range(D_MODEL // SEG)], axis=1)


def _wkv_kernel(nb, tt, r_ref, ld_ref, k_ref, v_ref, a_ref, s0_ref, kkp_ref, kap_ref, rkp_ref, lng_ref, lnb_ref,
                ones_ref, y_ref, sout_ref, s_scr, kk_scr, km_scr, o_scr, p_scr, e_scr):
    f32, bf16 = jnp.float32, jnp.bfloat16
    ones = ones_ref[...]

    @pl.when(pl.program_id(1) == 0)
    def _():
        for bb in range(nb):
            for p in range(N_PAIR):
                s_scr[bb, p] = jnp.concatenate([s0_ref[bb, 2 * p], s0_ref[bb, 2 * p + 1]], axis=1)

    for bb in range(nb):
        kraw = k_ref[bb]
        kkp = kraw * kkp_ref[...]
        kk_scr[bb] = kkp * lax.rsqrt(jnp.maximum(_segsum(kkp * kkp, ones), 1e-24))
        km_scr[bb] = kraw * (1.0 + (a_ref[bb] - 1.0) * kap_ref[...])

    row = lax.broadcasted_iota(jnp.int32, (CHUNK, D_MODEL), 0)
    head0 = (lax.broadcasted_iota(jnp.int32, (2 * CHUNK, D_MODEL), 1) // HEAD_DIM) % 2 == 0
    bcast = lambda x, s: jnp.broadcast_to(x[s:s + 1, :], (CHUNK, D_MODEL))

    def chunk(c, carry):
        rows = pl.ds(pl.multiple_of(c * CHUNK, CHUNK), CHUNK)
        st = []
        for bb in range(nb):
            r = r_ref[bb, rows, :]
            ld = ld_ref[bb, rows, :]
            v = v_ref[bb, rows, :]
            kk = kk_scr[bb, rows, :]
            cum = ld
            for sh in (1, 2, 4):
                cum = cum + jnp.where(row >= sh, pltpu.roll(cum, sh, axis=0), 0.0)
            w_cum = jnp.exp(cum)
            w_inv = jnp.exp(-cum)
            at = -kk * jnp.exp(cum - ld)
            rt = r * w_cum
            bt = kk * a_ref[bb, rows, :] * w_inv
            kt = km_scr[bb, rows, :] * w_inv
            wc = bcast(w_cum, CHUNK - 1)
            lhs = jnp.concatenate([at, rt], axis=0)
            lhs = jnp.concatenate([jnp.where(head0, lhs, 0.0), jnp.where(head0, 0.0, lhs)], axis=0).astype(bf16)
            st.append([v, at, rt, bt, kt, wc, lhs])
        for bb in range(nb):
            v, at, rt, bt, kt, wc, lhs = st[bb]
            xy = []
            for p in range(N_PAIR):
                sl = slice(p * PAIR, (p + 1) * PAIR)
                xh = lax.dot_general(lhs[:, sl], s_scr[bb, p].astype(bf16), (((1,), (1,)), ((), ())),
                                     preferred_element_type=f32)
                xy.append(xh[0:2 * CHUNK])
                xy.append(xh[2 * CHUNK:4 * CHUNK])
            st[bb][6] = jnp.concatenate(xy, axis=1)
            for s in range(CHUNK):
                bs, ks = bcast(bt, s), bcast(kt, s)
                rm = jnp.where(row >= s, rt, 0.0) if s else rt
                if s < CHUNK - 1:
                    am = jnp.where(row > s, at, 0.0)
                    p_scr[bb, 32 * s:32 * s + 16, :] = jnp.concatenate([bs * am, ks * am], axis=0).astype(bf16)
                p_scr[bb, R_OFF[s]:R_OFF[s] + 16, :] = jnp.concatenate([bs * rm, ks * rm], axis=0).astype(bf16)
            for q in range(D_MODEL // SEG):
                sl = slice(q * SEG, (q + 1) * SEG)
                e_scr[bb, :, sl] = jnp.dot(p_scr[bb, :, sl], ones, preferred_element_type=f32)
        accs = []
        for bb in range(nb):
            v, at, rt, bt, kt, wc, xy = st[bb]
            pre_u = pre_o = None
            for s in range(CHUNK):
                vs = bcast(v, s)
                to = e_scr[bb, R_OFF[s] + 8:R_OFF[s] + 16, :] * vs
                pre_o = to if pre_o is None else pre_o + to
                if s < CHUNK - 1:
                    tu = e_scr[bb, 32 * s + 8:32 * s + 16, :] * vs
                    pre_u = tu if pre_u is None else pre_u + tu
            acc_u = xy[0:CHUNK] + pre_u
            acc_o = xy[CHUNK:2 * CHUNK] + pre_o
            for s in range(CHUNK):
                us = bcast(acc_u, s)
                if s < CHUNK - 1:
                    acc_u = acc_u + e_scr[bb, 32 * s:32 * s + 8, :] * us
                acc_o = acc_o + e_scr[bb, R_OFF[s]:R_OFF[s] + 8, :] * us
            o_scr[bb, rows, :] = acc_o
            accs.append(acc_u)
        for bb in range(nb):
            v, at, rt, bt, kt, wc, xy = st[bb]
            uv = jnp.concatenate([accs[bb], v], axis=0)
            uv_swapped = pltpu.roll(uv, D_MODEL - HEAD_DIM, axis=1)
            uv = jnp.concatenate([jnp.where(head0, uv, 0.0), jnp.where(head0, uv_swapped, 0.0)], axis=0).astype(bf16)
            bk = jnp.concatenate([bt * wc, kt * wc], axis=0)
            bk = jnp.concatenate([jnp.where(head0, bk, 0.0), jnp.where(head0, 0.0, bk)], axis=0).astype(bf16)
            for p in range(N_PAIR):
                sl = slice(p * PAIR, (p + 1) * PAIR)
                upd = lax.dot_general(uv[:, p * PAIR:p * PAIR + HEAD_DIM], bk[:, sl], (((0,), (0,)), ((), ())),
                                      preferred_element_type=f32)
                s_scr[bb, p] = s_scr[bb, p] * wc[0:1, sl] + upd
        return carry

    lax.fori_loop(0, tt // CHUNK, chunk, 0)

    for bb in range(nb):
        o = o_scr[bb]
        mu = _segsum(o, ones) * (1.0 / HEAD_DIM)
        od = o - mu
        var = _segsum(od * od, ones) * (1.0 / HEAD_DIM)
        y = od * lax.rsqrt(var + GN_EPS) * lng_ref[...] + lnb_ref[...]
        bonus = _segsum(r_ref[bb] * km_scr[bb] * rkp_ref[...], ones)
        y_ref[bb] = y + bonus * v_ref[bb]

    @pl.when(pl.program_id(1) == pl.num_programs(1) - 1)
    def _():
        for bb in range(nb):
            for p in range(N_PAIR):
                sp = s_scr[bb, p]
                sout_ref[bb, 2 * p] = sp[:, 0:HEAD_DIM]
                sout_ref[bb, 2 * p + 1] = sp[:, HEAD_DIM:PAIR]


def _wkv(r, ld, k, v, a, s0, P, li, nb, tt):
    bsz, tlen, _ = r.shape
    f32 = jnp.float32
    tok = pl.BlockSpec((nb, tt, D_MODEL), lambda b, t: (b, t, 0))
    stt = pl.BlockSpec((nb, HEADS, HEAD_DIM, HEAD_DIM), lambda b, t: (b, 0, 0, 0))
    lstt = pl.BlockSpec((None, nb, HEADS, HEAD_DIM, HEAD_DIM), lambda b, t: (li, b, 0, 0, 0))
    par = _lpar(P['kk'], li)
    return pl.pallas_call(
        functools.partial(_wkv_kernel, nb, tt),
        grid=(bsz // nb, tlen // tt),
        in_specs=[tok] * 5 + [lstt] + [par] * 5 + [pl.BlockSpec((SEG, SEG), lambda b, t: (0, 0))],
        out_specs=[tok, stt],
        out_shape=[jax.ShapeDtypeStruct((bsz, tlen, D_MODEL), f32),
                   jax.ShapeDtypeStruct((bsz, HEADS, HEAD_DIM, HEAD_DIM), f32)],
        scratch_shapes=[pltpu.VMEM((nb, N_PAIR, HEAD_DIM, PAIR), f32),
                        pltpu.VMEM((nb, tt, D_MODEL), f32), pltpu.VMEM((nb, tt, D_MODEL), f32),
                        pltpu.VMEM((nb, tt, D_MODEL), f32),
                        pltpu.VMEM((nb, P_ROWS, D_MODEL), jnp.bfloat16),
                        pltpu.VMEM((nb, P_ROWS, D_MODEL), f32)],
        compiler_params=_cparams(("arbitrary", "arbitrary")),
        name="wkv",
    )(r, ld, k, v, a, s0, P['kk'], P['ka'], P['rk'], P['lng'], P['lnb'], P['ones'])


LANES = 128


def _wkv_lanes_kernel(tlen, r_ref, ld_ref, k_ref, v_ref, a_ref, s0_ref, kkp_ref, kap_ref, rkp_ref, lng_ref, lnb_ref,
                      y_ref, s_ref, o_scr):
    s_ref[...] = s0_ref[...]
    for t in range(tlen):
        rt = r_ref[t]
        wt = jnp.exp(ld_ref[t])
        kraw = k_ref[t]
        at = a_ref[t]
        kkp = kraw * kkp_ref[...]
        kk = kkp * lax.rsqrt(jnp.maximum(jnp.sum(kkp * kkp, axis=0, keepdims=True), 1e-24))
        kt = kraw * (1.0 + (at - 1.0) * kap_ref[...])
        bt = kk * at
        nkk = -kk
        for i in range(HEAD_DIM):
            si = s_ref[i]
            sa = jnp.sum(si * nkk, axis=0, keepdims=True)
            sn = si * wt + sa * bt + v_ref[t, i:i + 1, :] * kt
            s_ref[i] = sn
            o_scr[i:i + 1, :] = jnp.sum(sn * rt, axis=0, keepdims=True)
        o = o_scr[...]
        mu = jnp.mean(o, axis=0, keepdims=True)
        od = o - mu
        var = jnp.mean(od * od, axis=0, keepdims=True)
        y = od * lax.rsqrt(var + GN_EPS) * lng_ref[...] + lnb_ref[...]
        bonus = jnp.sum(rt * kt * rkp_ref[...], axis=0, keepdims=True)
        y_ref[t] = y + bonus * v_ref[t]


def _wkv_lanes(r, ld, k, v, a, s0, P, li):
    tlen, _, _, bsz = r.shape
    f32 = jnp.float32
    tok = pl.BlockSpec((tlen, None, HEAD_DIM, bsz), lambda h: (0, h, 0, 0))
    lstt = pl.BlockSpec((None, None, HEAD_DIM, HEAD_DIM, bsz), lambda h: (li, h, 0, 0, 0))
    stt = pl.BlockSpec((None, HEAD_DIM, HEAD_DIM, bsz), lambda h: (h, 0, 0, 0))
    par = pl.BlockSpec((None, None, HEAD_DIM, bsz), lambda h: (li, h, 0, 0))
    hp = P['head_lanes']
    return pl.pallas_call(
        functools.partial(_wkv_lanes_kernel, tlen),
        grid=(HEADS,),
        in_specs=[tok] * 5 + [lstt] + [par] * 5,
        out_specs=[tok, stt],
        out_shape=[jax.ShapeDtypeStruct((tlen, HEADS, HEAD_DIM, bsz), f32),
                   jax.ShapeDtypeStruct((HEADS, HEAD_DIM, HEAD_DIM, bsz), f32)],
        scratch_shapes=[pltpu.VMEM((HEAD_DIM, bsz), f32)],
        compiler_params=_cparams(("arbitrary",)),
        name="wkv_lanes",
    )(r, ld, k, v, a, s0, hp['kk'], hp['ka'], hp['rk'], hp['lng'], hp['lnb'])


def _merge_kernel(h_ref, xa_ref, yw_ref, g_ref, xc_ref, zg_ref, wa_ref, wb_ref, wc_ref, bc_ref, wo_ref, o_ref):
    ya = _bdot(xa_ref[...], wa_ref[...])
    yb = _bdot(yw_ref[...] * g_ref[...], wb_ref[...])
    yc = _bdot(xc_ref[...], wc_ref[...]) + bc_ref[...]
    m = (_sigmoid(zg_ref[:, 0:D_MODEL]) * ya + _sigmoid(zg_ref[:, D_MODEL:2 * D_MODEL]) * yb
         + _sigmoid(zg_ref[:, 2 * D_MODEL:3 * D_MODEL]) * yc)
    o_ref[...] = h_ref[...] + _bdot(m, wo_ref[...])


def _merge(h, xa, yw, g, xc, z, P, li, tm):
    n = h.shape[0]
    tok = lambda width: pl.BlockSpec((tm, width), lambda i: (i, 0))
    par = lambda a: _lpar(a, li)
    ws = [P[n_] for n_ in ('sc_w_out', 'rk_w_out', 'cf_w_out', 'cf_b_out', 'w_o')]
    return pl.pallas_call(
        _merge_kernel,
        grid=(n // tm,),
        in_specs=[tok(D_MODEL), tok(D_SC), tok(D_MODEL), tok(D_MODEL), tok(D_CF), tok(Z_BLK)] + [par(a) for a in ws],
        out_specs=tok(D_MODEL),
        out_shape=jax.ShapeDtypeStruct((n, D_MODEL), jnp.float32),
        compiler_params=_cparams(("arbitrary",)),
        name="merge",
    )(h, xa, yw, g, xc, z, *ws)


def _mlp_ple_kernel(nk, final, h_ref, p_ref, g2_ref, w1_ref, w2_ref, gp_ref, wg_ref, wp_ref, *rest):
    gf_ref = rest[0] if final else None
    o_ref, hn_scr, acc_scr = rest[-3:]
    kstep = pl.program_id(1)

    @pl.when(kstep == 0)
    def _():
        hn_scr[...] = _rms(h_ref[...], g2_ref[...]).astype(jnp.bfloat16)
        acc_scr[...] = jnp.zeros_like(acc_scr)

    u = jnp.dot(hn_scr[...], w1_ref[...], preferred_element_type=jnp.float32)
    u = jnp.maximum(u, 0.0)
    acc_scr[...] += _bdot(u * u, w2_ref[...])

    @pl.when(kstep == nk - 1)
    def _():
        h1 = h_ref[...] + acc_scr[...]
        gate = _sigmoid(_bdot(_rms(h1, gp_ref[...]), wg_ref[...]))
        h2 = h1 + gate * _bdot(p_ref[...], wp_ref[...])
        o_ref[...] = h2 if gf_ref is None else _rms(h2, gf_ref[...])


def _mlp_ple(h, p, P, li, tm, tf, final_g=None):
    n = h.shape[0]
    nk = D_FF // tf
    par = lambda a: _lpar(a, li)
    extra = [] if final_g is None else [final_g]
    return pl.pallas_call(
        functools.partial(_mlp_ple_kernel, nk, final_g is not None),
        grid=(n // tm, nk),
        in_specs=[pl.BlockSpec((tm, D_MODEL), lambda i, k: (i, 0)),
                  pl.BlockSpec((None, tm, D_PLE), lambda i, k: (li, i, 0)),
                  par(P['norm2_g']),
                  pl.BlockSpec((None, D_MODEL, tf), lambda i, k: (li, 0, k)),
                  pl.BlockSpec((None, tf, D_MODEL), lambda i, k: (li, k, 0)),
                  par(P['ple_norm_g']), par(P['ple_gate_w']), par(P['ple_w'])]
                 + [pl.BlockSpec((1, D_MODEL), lambda i, k: (0, 0)) for _ in extra],
        out_specs=pl.BlockSpec((tm, D_MODEL), lambda i, k: (i, 0)),
        out_shape=jax.ShapeDtypeStruct((n, D_MODEL), jnp.float32),
        scratch_shapes=[pltpu.VMEM((tm, D_MODEL), jnp.bfloat16), pltpu.VMEM((tm, D_MODEL), jnp.float32)],
        compiler_params=_cparams(("arbitrary", "arbitrary")),
        name="mlp_ple",
    )(h, p, P['norm2_g'], P['mlp_w1'], P['mlp_w2'], P['ple_norm_g'], P['ple_gate_w'], P['ple_w'], *extra)


def _prep(W):
    bf = jnp.bfloat16
    nl = W['w_in'].shape[0]
    row = lambda a: a.reshape(nl, 1, -1)
    w_in = W['w_in']
    n_sc, n_rk, n_cf = 3 * D_SC, RK_PROJ, 2 * D_CF
    o_rk, o_cf, o_gate = n_sc, n_sc + n_rk, n_sc + n_rk + n_cf
    w_perm = jnp.concatenate([
        w_in[:, :, o_gate:o_gate + 3 * D_MODEL], w_in[:, :, o_rk:o_rk + 3 * D_MODEL], w_in[:, :, 0:n_sc],
        w_in[:, :, o_cf:o_cf + n_cf], w_in[:, :, o_rk + 3 * D_MODEL:o_rk + RK_PROJ],
        jnp.zeros((nl, D_MODEL, LORA_PAD - N_LORA), w_in.dtype)], axis=2).astype(bf)
    mu = W['rk_mu']
    zrows = lambda n: jnp.zeros((nl, n, D_MODEL), jnp.float32)
    o_a, o_g = DECAY_LORA, DECAY_LORA + AAA_LORA + MV_LORA
    seg_head = jnp.arange(SEG) // HEAD_DIM
    on_lanes = lambda a: jnp.broadcast_to(a.reshape(nl, HEADS, HEAD_DIM, 1), (nl, HEADS, HEAD_DIM, LANES))
    return {
        'head_lanes': {'kk': on_lanes(W['rk_k_k']), 'ka': on_lanes(W['rk_k_a']), 'rk': on_lanes(W['rk_r_k']),
                       'lng': on_lanes(W['rk_ln_g']), 'lnb': on_lanes(W['rk_ln_b'])},
        'ones': (seg_head[:, None] == seg_head[None, :]).astype(bf),
        'kk': row(W['rk_k_k']), 'ka': row(W['rk_k_a']), 'rk': row(W['rk_r_k']),
        'lng': row(W['rk_ln_g']), 'lnb': row(W['rk_ln_b']),
        'norm1_g': row(W['norm1_g']), 'w_in': w_perm,
        'mu1': row(mu[:, :3 * D_MODEL]),
        'mu2': row(jnp.pad(mu[:, 3 * D_MODEL:], ((0, 0), (0, LORA_PAD - N_LORA)))),
        'w0': row(W['rk_w0']), 'a0': row(W['rk_a0']), 'v0': row(W['rk_v0']),
        'w2p': jnp.concatenate([W['rk_w2'], zrows(128 - o_a)], axis=1).astype(bf),
        'a2p': jnp.concatenate([zrows(o_a), W['rk_a2']], axis=1).astype(bf),
        'v2p': jnp.concatenate([W['rk_v2'], zrows(128 - MV_LORA)], axis=1).astype(bf),
        'g2p': jnp.concatenate([zrows(o_g - 128), W['rk_g2'], zrows(384 - N_LORA)], axis=1).astype(bf),
        'scw': W['sc_conv_w'], 'cfbin': row(W['cf_b_in']), 'cfw': W['cf_dw_w'],
        'cfdb': row(W['cf_dw_b']), 'cflg': row(W['cf_ln_g']), 'cflb': row(W['cf_ln_b']),
        'sc_w_out': W['sc_w_out'].astype(bf), 'rk_w_out': W['rk_w_out'].astype(bf),
        'cf_w_out': W['cf_w_out'].astype(bf), 'cf_b_out': row(W['cf_b_out']), 'w_o': W['w_o'].astype(bf),
        'norm2_g': row(W['norm2_g']), 'mlp_w1': W['mlp_w1'].astype(bf), 'mlp_w2': W['mlp_w2'].astype(bf),
        'ple_norm_g': row(W['ple_norm_g']), 'ple_gate_w': W['ple_gate_w'].astype(bf),
        'ple_w': W['ple_w'].astype(bf),
    }


def _pick(n, pref):
    return pref if n % pref == 0 else n


def _trunk(x, p, sc0, shift0, wkv0, cf0, P, final_g, cfg):
    bsz, tlen, _ = x.shape
    n = bsz * tlen
    tm = _pick(n, cfg['tm'])
    tt = _pick(tlen, cfg['tt'])
    tpad = -tlen % CHUNK
    tw = _pick(tlen + tpad, cfg['tw'])
    pad_t = lambda a: jnp.pad(a, ((0, 0), (0, tpad), (0, 0))) if tpad else a
    h = x.reshape(n, D_MODEL)
    pf = p.reshape(DEPTH, n, D_PLE)
    sh1 = shift0[:, :, None, :3 * D_MODEL]
    sh2 = jnp.pad(shift0[:, :, 3 * D_MODEL:], ((0, 0), (0, 0), (0, LORA_PAD - N_LORA)))[:, :, None, :]
    lanes_form = bsz == LANES and tlen <= CHUNK
    to_lanes = lambda a: a.reshape(bsz, tlen, HEADS, HEAD_DIM).transpose(1, 2, 3, 0)
    wkv0_l = wkv0.transpose(0, 2, 3, 4, 1) if lanes_form else None
    v_first = None
    new_sc, new_shift, new_wkv, new_cf = [], [], [], []
    for i in range(DEPTH):
        z = _in_proj(h, P, i, tm)
        (r, w, k, v, a, g, xa, xc, nsh1, nsh2, nsc, ncf) = _seq(
            z.reshape(bsz, tlen, NZ), sh1, sh2, sc0, cf0, v_first, P, i, tt, cfg['nbs'])
        if i == 0:
            v_first = v
        if lanes_form:
            y, s_out = _wkv_lanes(to_lanes(r), to_lanes(w), to_lanes(k), to_lanes(v), to_lanes(a), wkv0_l, P, i)
            yw = y.transpose(3, 0, 1, 2).reshape(n, D_MODEL)
        else:
            y, s_out = _wkv(pad_t(r), pad_t(w), pad_t(k), pad_t(v), pad_t(a), wkv0, P, i, cfg['nb'], tw)
            yw = y[:, :tlen].reshape(n, D_MODEL)
        h = _merge(h, xa.reshape(n, D_SC), yw, g.reshape(n, D_MODEL), xc.reshape(n, D_CF), z, P, i, tm)
        h = _mlp_ple(h, pf, P, i, tm, cfg['tf'], final_g if i == DEPTH - 1 else None)
        new_sc.append(nsc)
        new_shift.append(jnp.concatenate([nsh1[:, 0, :], nsh2[:, 0, :N_LORA]], axis=-1))
        new_wkv.append(s_out)
        new_cf.append(ncf)
    y = h.reshape(bsz, tlen, D_MODEL)
    wkv_out = jnp.stack(new_wkv)
    if lanes_form:
        wkv_out = wkv_out.transpose(0, 4, 1, 2, 3)
    return y, jnp.stack(new_sc), jnp.stack(new_shift), wkv_out, jnp.stack(new_cf)


PROMPT_CFG = {'tm': 512, 'tt': 128, 'nbs': 1, 'tw': 128, 'nb': 4, 'tf': 2048}
SAMPLE_CFG = {'tm': 512, 'tt': 4, 'nbs': 8, 'tw': 8, 'nb': 8, 'tf': 1024}


def kernel(x_prompt, x_sample, p_prompt, p_sample, state_sconv, state_shift, state_wkv, state_cconv,
           norm1_g, w_in, sc_conv_w, sc_w_out, rk_mu, rk_w0, rk_w2, rk_a0, rk_a2, rk_v0, rk_v2, rk_g2,
           rk_k_k, rk_k_a, rk_r_k, rk_ln_g, rk_ln_b, rk_w_out, cf_b_in, cf_dw_w, cf_dw_b, cf_ln_g, cf_ln_b,
           cf_w_out, cf_b_out, w_o, norm2_g, mlp_w1, mlp_w2, ple_w, ple_gate_w, ple_norm_g, final_norm_g):
    W = dict(norm1_g=norm1_g, w_in=w_in, sc_conv_w=sc_conv_w, sc_w_out=sc_w_out, rk_mu=rk_mu, rk_w0=rk_w0,
             rk_w2=rk_w2, rk_a0=rk_a0, rk_a2=rk_a2, rk_v0=rk_v0, rk_v2=rk_v2, rk_g2=rk_g2, rk_k_k=rk_k_k,
             rk_k_a=rk_k_a, rk_r_k=rk_r_k, rk_ln_g=rk_ln_g, rk_ln_b=rk_ln_b, rk_w_out=rk_w_out, cf_b_in=cf_b_in,
             cf_dw_w=cf_dw_w, cf_dw_b=cf_dw_b, cf_ln_g=cf_ln_g, cf_ln_b=cf_ln_b, cf_w_out=cf_w_out,
             cf_b_out=cf_b_out, w_o=w_o, norm2_g=norm2_g, mlp_w1=mlp_w1, mlp_w2=mlp_w2, ple_w=ple_w,
             ple_gate_w=ple_gate_w, ple_norm_g=ple_norm_g)
    P = _prep(W)
    final_g = final_norm_g.reshape(1, D_MODEL)
    bp = x_prompt.shape[0]
    f32 = jnp.float32
    zsc = jnp.zeros((DEPTH, bp, SC_WIDTH - 1, D_SC), f32)
    zsh = jnp.zeros((DEPTH, bp, RK_PROJ), f32)
    zwkv = jnp.zeros((DEPTH, bp, HEADS, HEAD_DIM, HEAD_DIM), f32)
    zcf = jnp.zeros((DEPTH, bp, CF_WIDTH - 1, D_CF), f32)
    y_p, sc_p, sh_p, wkv_p, cf_p = _trunk(x_prompt, p_prompt, zsc, zsh, zwkv, zcf, P, final_g, PROMPT_CFG)
    y_s, sc_s, sh_s, wkv_s, cf_s = _trunk(x_sample, p_sample, state_sconv, state_shift, state_wkv, state_cconv,
                                          P, final_g, SAMPLE_CFG)
    return (y_p, y_s, sc_p, sh_p, wkv_p, cf_p, sc_s, sh_s, wkv_s, cf_s)
```

```python
import functools

import jax
import jax.numpy as jnp
from jax import lax
from jax.experimental import pallas as pl
from jax.experimental.pallas import tpu as pltpu

D_MODEL = 1024
DEPTH = 4
D_SC = 512
D_CF = 512
SC_WIDTH = 3
CF_WIDTH = 31
HEAD_DIM = 64
HEADS = 16
D_FF = 4096
D_PLE = 256
DECAY_LORA, AAA_LORA, MV_LORA, GATE_LORA = 64, 64, 32, 128
N_LORA = DECAY_LORA + AAA_LORA + MV_LORA + GATE_LORA
RK_PROJ = 3 * D_MODEL + N_LORA
RMS_EPS = 1e-6
LN_EPS = 1e-5
GN_EPS = 64e-5
EXP_M05 = 0.6065306597126334

Z_BLK = 3 * D_MODEL
NZ = 3 * Z_BLK
Z2_SC = 0
Z2_CF = 3 * D_SC
Z2_LORA = Z2_CF + 2 * D_CF
LORA_PAD = Z_BLK - Z2_LORA

VMEM_LIMIT_BYTES = 56 * 1024 * 1024


def _cparams(sem):
    return pltpu.CompilerParams(dimension_semantics=sem, vmem_limit_bytes=VMEM_LIMIT_BYTES)


def _sigmoid(x):
    return 1.0 / (1.0 + jnp.exp(-x))


def _rms(x, g):
    return x * lax.rsqrt(jnp.mean(x * x, axis=-1, keepdims=True) + RMS_EPS) * g


def _bdot(a, b):
    return jnp.dot(a.astype(jnp.bfloat16), b, preferred_element_type=jnp.float32)


def _in_proj_kernel(x_ref, g_ref, w_ref, o_ref):
    o_ref[...] = _bdot(_rms(x_ref[...], g_ref[...]), w_ref[...])


def _lpar(a, li):
    return pl.BlockSpec((None,) + a.shape[1:], lambda *_: (li,) + (0,) * (a.ndim - 1))


def _in_proj(x, P, li, tm):
    n = x.shape[0]
    g, w = P['norm1_g'], P['w_in']
    return pl.pallas_call(
        _in_proj_kernel,
        grid=(NZ // Z_BLK, n // tm),
        in_specs=[
            pl.BlockSpec((tm, D_MODEL), lambda j, i: (i, 0)),
            _lpar(g, li),
            pl.BlockSpec((None, D_MODEL, Z_BLK), lambda j, i: (li, 0, j)),
        ],
        out_specs=pl.BlockSpec((tm, Z_BLK), lambda j, i: (i, j)),
        out_shape=jax.ShapeDtypeStruct((n, NZ), jnp.float32),
        compiler_params=_cparams(("arbitrary", "arbitrary")),
        name="in_proj",
    )(x, g, w)


def _seq_batched(first_layer):
    n_in = 6 if first_layer else 7
    n_par = 15
    return set(range(n_in)) | set(range(n_in + n_par, n_in + n_par + 12 + 5))


SEQ_BATCHED = {True: _seq_batched(True), False: _seq_batched(False)}


def _seq_kernel(first_layer, tt, nbs, *refs):
    for bb in range(nbs):
        _seq_body(first_layer, tt, [r.at[bb] if i in SEQ_BATCHED[first_layer] else r for i, r in enumerate(refs)])


def _seq_body(first_layer, tt, refs):
    it = iter(refs)
    z1_ref, z2_ref, sh1_ref, sh2_ref, sc0_ref, cf0_ref = (next(it) for _ in range(6))
    vf_ref = None if first_layer else next(it)
    (mu1_ref, mu2_ref, w0_ref, w2_ref, a0_ref, a2_ref, v0_ref, v2_ref, g2_ref,
     scw_ref, cfbin_ref, cfw_ref, cfdb_ref, cflg_ref, cflb_ref) = (next(it) for _ in range(15))
    (r_ref, w_ref, k_ref, v_ref, a_ref, g_ref, xa_ref, xc_ref,
     nsh1_ref, nsh2_ref, nsc_ref, ncf_ref) = (next(it) for _ in range(12))
    s1buf, s2buf, cbuf, gbuf, pbuf = (next(it) for _ in range(5))

    @pl.when(pl.program_id(1) == 0)
    def _():
        s1buf[7:8, :] = sh1_ref[...]
        s2buf[7:8, :] = sh2_ref[...]
        cbuf[6:8, :] = sc0_ref[...]
        gbuf[2:CF_WIDTH + 1, :] = cf0_ref[...]
        gbuf[0:2, :] = jnp.zeros((2, D_CF), jnp.float32)
        gbuf[CF_WIDTH + 1 + tt:CF_WIDTH + 1 + tt + 8, :] = jnp.zeros((8, D_CF), jnp.float32)

    z1 = z1_ref[...]
    if tt % 8 == 0:
        rowi = lax.broadcasted_iota(jnp.int32, (tt, Z_BLK), 0)
        zp1 = jnp.where(rowi == 0, jnp.broadcast_to(s1buf[7:8, :], (tt, Z_BLK)), pltpu.roll(z1, 1, axis=0))
        s1buf[7 + tt:8 + tt, :] = z1[tt - 1:tt, :]
    else:
        s1buf[8:8 + tt, :] = z1
        zp1 = s1buf[7:7 + tt, :]
    zm1 = z1 + (zp1 - z1) * mu1_ref[...]
    z2l = z2_ref[:, Z2_LORA:Z_BLK]
    if tt % 8 == 0:
        rowl = lax.broadcasted_iota(jnp.int32, (tt, LORA_PAD), 0)
        zp2 = jnp.where(rowl == 0, jnp.broadcast_to(s2buf[7:8, :], (tt, LORA_PAD)),
                        pltpu.roll(z2l, 1, axis=0))
        s2buf[7 + tt:8 + tt, :] = z2l[tt - 1:tt, :]
    else:
        s2buf[8:8 + tt, :] = z2l
        zp2 = s2buf[7:7 + tt, :]
    zm2 = z2l + (zp2 - z2l) * mu2_ref[...]
    last1 = s1buf[7 + tt:8 + tt, :]
    last2 = s2buf[7 + tt:8 + tt, :]
    nsh1_ref[...] = last1
    nsh2_ref[...] = last2
    s1buf[7:8, :] = last1
    s2buf[7:8, :] = last2

    x01 = zm2[:, 0:128]
    lw = _bdot(jnp.tanh(x01), w2_ref[...])
    la = _bdot(x01, a2_ref[...])
    w_ref[...] = -EXP_M05 * _sigmoid(w0_ref[...] + lw)
    a_ref[...] = _sigmoid(a0_ref[...] + la)
    g_ref[...] = _bdot(_sigmoid(zm2[:, 128:384]), g2_ref[...])
    r_ref[...] = zm1[:, 0:D_MODEL]
    k_ref[...] = zm1[:, D_MODEL:2 * D_MODEL]
    v = zm1[:, 2 * D_MODEL:3 * D_MODEL]
    if not first_layer:
        lv = _bdot(zm2[:, 128:256], v2_ref[...])
        v = v + (vf_ref[...] - v) * _sigmoid(v0_ref[...] + lv)
    v_ref[...] = v

    cu = z2_ref[:, Z2_SC + D_SC:Z2_SC + 2 * D_SC] * z2_ref[:, Z2_SC + 2 * D_SC:Z2_SC + 3 * D_SC]
    cbuf[8:8 + tt, :] = cu
    conv_a = (scw_ref[0:1, :] * cbuf[6:6 + tt, :] + scw_ref[1:2, :] * cbuf[7:7 + tt, :]
              + scw_ref[2:3, :] * cu)
    xa_ref[...] = z2_ref[:, Z2_SC:Z2_SC + D_SC] * conv_a
    tail_a = cbuf[6 + tt:8 + tt, :]
    nsc_ref[...] = tail_a
    cbuf[6:8, :] = tail_a

    zc = z2_ref[:, Z2_CF:Z2_CF + 2 * D_CF] + cfbin_ref[...]
    glu = zc[:, 0:D_CF] * _sigmoid(zc[:, D_CF:2 * D_CF])
    gbuf[CF_WIDTH + 1:CF_WIDTH + 1 + tt, :] = glu
    if tt % 8 == 0:
        acc = None
        for ph in range(8):
            part = None
            for j in range(ph, CF_WIDTH + 2, 8):
                if j < 2:
                    continue
                term = cfw_ref[j - 2:j - 1, :] * gbuf[j - ph:j - ph + tt + 8, :]
                part = term if part is None else part + term
            pbuf[...] = part
            sh = pbuf[ph:ph + tt, :]
            acc = sh if acc is None else acc + sh
    else:
        acc = cfw_ref[0:1, :] * gbuf[2:2 + tt, :]
        for kk in range(1, CF_WIDTH):
            acc = acc + cfw_ref[kk:kk + 1, :] * gbuf[2 + kk:2 + kk + tt, :]
    cc = acc + cfdb_ref[...]
    mu = jnp.mean(cc, axis=-1, keepdims=True)
    cd = cc - mu
    var = jnp.mean(cd * cd, axis=-1, keepdims=True)
    ln = cd * lax.rsqrt(var + LN_EPS) * cflg_ref[...] + cflb_ref[...]
    xc_ref[...] = ln * _sigmoid(ln)
    tail_c = gbuf[2 + tt:CF_WIDTH + 1 + tt, :]
    ncf_ref[...] = tail_c
    gbuf[2:CF_WIDTH + 1, :] = tail_c


def _seq(z, sh1, sh2, sc0, cf0, vfirst, P, li, tt, nbs):
    bsz, tlen, _ = z.shape
    first_layer = vfirst is None
    tok = lambda width, blk: pl.BlockSpec((nbs, tt, width), lambda b, t, blk=blk: (b, t, blk))
    st = lambda rows, width: pl.BlockSpec((nbs, rows, width), lambda b, t: (b, 0, 0))
    lst = lambda rows, width: pl.BlockSpec((None, nbs, rows, width), lambda b, t: (li, b, 0, 0))
    par = lambda a: _lpar(a, li)
    params = [P[n] for n in ('mu1', 'mu2', 'w0', 'w2p', 'a0', 'a2p', 'v0', 'v2p', 'g2p',
                              'scw', 'cfbin', 'cfw', 'cfdb', 'cflg', 'cflb')]
    args = [z, z, sh1, sh2, sc0, cf0]
    in_specs = [tok(Z_BLK, 1), tok(Z_BLK, 2), lst(1, Z_BLK), lst(1, LORA_PAD),
                lst(SC_WIDTH - 1, D_SC), lst(CF_WIDTH - 1, D_CF)]
    if not first_layer:
        args.append(vfirst)
        in_specs.append(tok(D_MODEL, 0))
    args += params
    in_specs += [par(a) for a in params]
    f32 = jnp.float32
    out_shape = ([jax.ShapeDtypeStruct((bsz, tlen, D_MODEL), f32)] * 6
                 + [jax.ShapeDtypeStruct((bsz, tlen, D_SC), f32), jax.ShapeDtypeStruct((bsz, tlen, D_CF), f32),
                    jax.ShapeDtypeStruct((bsz, 1, Z_BLK), f32), jax.ShapeDtypeStruct((bsz, 1, LORA_PAD), f32),
                    jax.ShapeDtypeStruct((bsz, SC_WIDTH - 1, D_SC), f32),
                    jax.ShapeDtypeStruct((bsz, CF_WIDTH - 1, D_CF), f32)])
    out_specs = ([tok(D_MODEL, 0)] * 6 + [tok(D_SC, 0), tok(D_CF, 0), st(1, Z_BLK), st(1, LORA_PAD),
                                           st(SC_WIDTH - 1, D_SC), st(CF_WIDTH - 1, D_CF)])
    scratch = [pltpu.VMEM((nbs, 8 + tt, Z_BLK), f32), pltpu.VMEM((nbs, 8 + tt, LORA_PAD), f32),
               pltpu.VMEM((nbs, 8 + tt, D_SC), f32), pltpu.VMEM((nbs, CF_WIDTH + 1 + tt + 8, D_CF), f32),
               pltpu.VMEM((nbs, tt + 8, D_CF), f32)]
    return pl.pallas_call(
        functools.partial(_seq_kernel, first_layer, tt, nbs),
        grid=(bsz // nbs, tlen // tt),
        in_specs=in_specs, out_specs=out_specs, out_shape=out_shape, scratch_shapes=scratch,
        compiler_params=_cparams(("arbitrary", "arbitrary")),
        name="seq",
    )(*args)


CHUNK = 8
PAIR = 2 * HEAD_DIM
N_PAIR = HEADS // 2
SEG = 256
P_ROWS = 4 * CHUNK * CHUNK - 2 * CHUNK
R_OFF = [32 * s + 16 if s < CHUNK - 1 else 32 * s for s in range(CHUNK)]


def _segsum(x, ones):
    xb = x.astype(jnp.bfloat16)
    return jnp.concatenate([jnp.dot(xb[:, q * SEG:(q + 1) * SEG], ones, preferred_element_type=jnp.float32)
                            for q in range(D_MODEL // SEG)], axis=1)


def _wkv_kernel(nb, tt, r_ref, ld_ref, k_ref, v_ref, a_ref, s0_ref, kkp_ref, kap_ref, rkp_ref, lng_ref, lnb_ref,
                ones_ref, y_ref, sout_ref, s_scr, kk_scr, km_scr, o_scr, p_scr, e_scr):
    f32, bf16 = jnp.float32, jnp.bfloat16
    ones = ones_ref[...]

    @pl.when(pl.program_id(1) == 0)
    def _():
        for bb in range(nb):
            for p in range(N_PAIR):
                s_scr[bb, p] = jnp.concatenate([s0_ref[bb, 2 * p], s0_ref[bb, 2 * p + 1]], axis=1)

    for bb in range(nb):
        kraw = k_ref[bb]
        kkp = kraw * kkp_ref[...]
        kk_scr[bb] = kkp * lax.rsqrt(jnp.maximum(_segsum(kkp * kkp, ones), 1e-24))
        km_scr[bb] = kraw * (1.0 + (a_ref[bb] - 1.0) * kap_ref[...])

    row = lax.broadcasted_iota(jnp.int32, (CHUNK, D_MODEL), 0)
    head0 = (lax.broadcasted_iota(jnp.int32, (2 * CHUNK, D_MODEL), 1) // HEAD_DIM) % 2 == 0
    bcast = lambda x, s: jnp.broadcast_to(x[s:s + 1, :], (CHUNK, D_MODEL))

    def chunk(c, carry):
        rows = pl.ds(pl.multiple_of(c * CHUNK, CHUNK), CHUNK)
        st = []
        for bb in range(nb):
            r = r_ref[bb, rows, :]
            ld = ld_ref[bb, rows, :]
            v = v_ref[bb, rows, :]
            kk = kk_scr[bb, rows, :]
            cum = ld
            for sh in (1, 2, 4):
                cum = cum + jnp.where(row >= sh, pltpu.roll(cum, sh, axis=0), 0.0)
            w_cum = jnp.exp(cum)
            w_inv = jnp.exp(-cum)
            at = -kk * jnp.exp(cum - ld)
            rt = r * w_cum
            bt = kk * a_ref[bb, rows, :] * w_inv
            kt = km_scr[bb, rows, :] * w_inv
            wc = bcast(w_cum, CHUNK - 1)
            lhs = jnp.concatenate([at, rt], axis=0)
            lhs = jnp.concatenate([jnp.where(head0, lhs, 0.0), jnp.where(head0, 0.0, lhs)], axis=0).astype(bf16)
            st.append([v, at, rt, bt, kt, wc, lhs])
        for bb in range(nb):
            v, at, rt, bt, kt, wc, lhs = st[bb]
            xy = []
            for p in range(N_PAIR):
                sl = slice(p * PAIR, (p + 1) * PAIR)
                xh = lax.dot_general(lhs[:, sl], s_scr[bb, p].astype(bf16), (((1,), (1,)), ((), ())),
                                     preferred_element_type=f32)
                xy.append(xh[0:2 * CHUNK])
                xy.append(xh[2 * CHUNK:4 * CHUNK])
            st[bb][6] = jnp.concatenate(xy, axis=1)
            for s in range(CHUNK):
                bs, ks = bcast(bt, s), bcast(kt, s)
                rm = jnp.where(row >= s, rt, 0.0) if s else rt
                if s < CHUNK - 1:
                    am = jnp.where(row > s, at, 0.0)
                    p_scr[bb, 32 * s:32 * s + 16, :] = jnp.concatenate([bs * am, ks * am], axis=0).astype(bf16)
                p_scr[bb, R_OFF[s]:R_OFF[s] + 16, :] = jnp.concatenate([bs * rm, ks * rm], axis=0).astype(bf16)
            for q in range(D_MODEL // SEG):
                sl = slice(q * SEG, (q + 1) * SEG)
                e_scr[bb, :, sl] = jnp.dot(p_scr[bb, :, sl], ones, preferred_element_type=f32)
        accs = []
        for bb in range(nb):
            v, at, rt, bt, kt, wc, xy = st[bb]
            pre_u = pre_o = None
            for s in range(CHUNK):
                vs = bcast(v, s)
                to = e_scr[bb, R_OFF[s] + 8:R_OFF[s] + 16, :] * vs
                pre_o = to if pre_o is None else pre_o + to
                if s < CHUNK - 1:
                    tu = e_scr[bb, 32 * s + 8:32 * s + 16, :] * vs
                    pre_u = tu if pre_u is None else pre_u + tu
            acc_u = xy[0:CHUNK] + pre_u
            acc_o = xy[CHUNK:2 * CHUNK] + pre_o
            for s in range(CHUNK):
                us = bcast(acc_u, s)
                if s < CHUNK - 1:
                    acc_u = acc_u + e_scr[bb, 32 * s:32 * s + 8, :] * us
                acc_o = acc_o + e_scr[bb, R_OFF[s]:R_OFF[s] + 8, :] * us
            o_scr[bb, rows, :] = acc_o
            accs.append(acc_u)
        for bb in range(nb):
            v, at, rt, bt, kt, wc, xy = st[bb]
            uv = jnp.concatenate([accs[bb], v], axis=0)
            uv_swapped = pltpu.roll(uv, D_MODEL - HEAD_DIM, axis=1)
            uv = jnp.concatenate([jnp.where(head0, uv, 0.0), jnp.where(head0, uv_swapped, 0.0)], axis=0).astype(bf16)
            bk = jnp.concatenate([bt * wc, kt * wc], axis=0)
            bk = jnp.concatenate([jnp.where(head0, bk, 0.0), jnp.where(head0, 0.0, bk)], axis=0).astype(bf16)
            for p in range(N_PAIR):
                sl = slice(p * PAIR, (p + 1) * PAIR)
                upd = lax.dot_general(uv[:, p * PAIR:p * PAIR + HEAD_DIM], bk[:, sl], (((0,), (0,)), ((), ())),
                                      preferred_element_type=f32)
                s_scr[bb, p] = s_scr[bb, p] * wc[0:1, sl] + upd
        return carry

    lax.fori_loop(0, tt // CHUNK, chunk, 0)

    for bb in range(nb):
        o = o_scr[bb]
        mu = _segsum(o, ones) * (1.0 / HEAD_DIM)
        od = o - mu
        var = _segsum(od * od, ones) * (1.0 / HEAD_DIM)
        y = od * lax.rsqrt(var + GN_EPS) * lng_ref[...] + lnb_ref[...]
        bonus = _segsum(r_ref[bb] * km_scr[bb] * rkp_ref[...], ones)
        y_ref[bb] = y + bonus * v_ref[bb]

    @pl.when(pl.program_id(1) == pl.num_programs(1) - 1)
    def _():
        for bb in range(nb):
            for p in range(N_PAIR):
                sp = s_scr[bb, p]
                sout_ref[bb, 2 * p] = sp[:, 0:HEAD_DIM]
                sout_ref[bb, 2 * p + 1] = sp[:, HEAD_DIM:PAIR]


def _wkv(r, ld, k, v, a, s0, P, li, nb, tt):
    bsz, tlen, _ = r.shape
    f32 = jnp.float32
    tok = pl.BlockSpec((nb, tt, D_MODEL), lambda b, t: (b, t, 0))
    stt = pl.BlockSpec((nb, HEADS, HEAD_DIM, HEAD_DIM), lambda b, t: (b, 0, 0, 0))
    lstt = pl.BlockSpec((None, nb, HEADS, HEAD_DIM, HEAD_DIM), lambda b, t: (li, b, 0, 0, 0))
    par = _lpar(P['kk'], li)
    return pl.pallas_call(
        functools.partial(_wkv_kernel, nb, tt),
        grid=(bsz // nb, tlen // tt),
        in_specs=[tok] * 5 + [lstt] + [par] * 5 + [pl.BlockSpec((SEG, SEG), lambda b, t: (0, 0))],
        out_specs=[tok, stt],
        out_shape=[jax.ShapeDtypeStruct((bsz, tlen, D_MODEL), f32),
                   jax.ShapeDtypeStruct((bsz, HEADS, HEAD_DIM, HEAD_DIM), f32)],
        scratch_shapes=[pltpu.VMEM((nb, N_PAIR, HEAD_DIM, PAIR), f32),
                        pltpu.VMEM((nb, tt, D_MODEL), f32), pltpu.VMEM((nb, tt, D_MODEL), f32),
                        pltpu.VMEM((nb, tt, D_MODEL), f32),
                        pltpu.VMEM((nb, P_ROWS, D_MODEL), jnp.bfloat16),
                        pltpu.VMEM((nb, P_ROWS, D_MODEL), f32)],
        compiler_params=_cparams(("arbitrary", "arbitrary")),
        name="wkv",
    )(r, ld, k, v, a, s0, P['kk'], P['ka'], P['rk'], P['lng'], P['lnb'], P['ones'])


LANES = 128


def _wkv_lanes_kernel(tlen, r_ref, ld_ref, k_ref, v_ref, a_ref, s0_ref, kkp_ref, kap_ref, rkp_ref, lng_ref, lnb_ref,
                      y_ref, s_ref, o_scr):
    s_ref[...] = s0_ref[...]
    for t in range(tlen):
        rt = r_ref[t]
        wt = jnp.exp(ld_ref[t])
        kraw = k_ref[t]
        at = a_ref[t]
        kkp = kraw * kkp_ref[...]
        kk = kkp * lax.rsqrt(jnp.maximum(jnp.sum(kkp * kkp, axis=0, keepdims=True), 1e-24))
        kt = kraw * (1.0 + (at - 1.0) * kap_ref[...])
        bt = kk * at
        nkk = -kk
        for i in range(HEAD_DIM):
            si = s_ref[i]
            sa = jnp.sum(si * nkk, axis=0, keepdims=True)
            sn = si * wt + sa * bt + v_ref[t, i:i + 1, :] * kt
            s_ref[i] = sn
            o_scr[i:i + 1, :] = jnp.sum(sn * rt, axis=0, keepdims=True)
        o = o_scr[...]
        mu = jnp.mean(o, axis=0, keepdims=True)
        od = o - mu
        var = jnp.mean(od * od, axis=0, keepdims=True)
        y = od * lax.rsqrt(var + GN_EPS) * lng_ref[...] + lnb_ref[...]
        bonus = jnp.sum(rt * kt * rkp_ref[...], axis=0, keepdims=True)
        y_ref[t] = y + bonus * v_ref[t]


def _wkv_lanes(r, ld, k, v, a, s0, P, li):
    tlen, _, _, bsz = r.shape
    f32 = jnp.float32
    tok = pl.BlockSpec((tlen, None, HEAD_DIM, bsz), lambda h: (0, h, 0, 0))
    lstt = pl.BlockSpec((None, None, HEAD_DIM, HEAD_DIM, bsz), lambda h: (li, h, 0, 0, 0))
    stt = pl.BlockSpec((None, HEAD_DIM, HEAD_DIM, bsz), lambda h: (h, 0, 0, 0))
    par = pl.BlockSpec((None, None, HEAD_DIM, bsz), lambda h: (li, h, 0, 0))
    hp = P['head_lanes']
    return pl.pallas_call(
        functools.partial(_wkv_lanes_kernel, tlen),
        grid=(HEADS,),
        in_specs=[tok] * 5 + [lstt] + [par] * 5,
        out_specs=[tok, stt],
        out_shape=[jax.ShapeDtypeStruct((tlen, HEADS, HEAD_DIM, bsz), f32),
                   jax.ShapeDtypeStruct((HEADS, HEAD_DIM, HEAD_DIM, bsz), f32)],
        scratch_shapes=[pltpu.VMEM((HEAD_DIM, bsz), f32)],
        compiler_params=_cparams(("arbitrary",)),
        name="wkv_lanes",
    )(r, ld, k, v, a, s0, hp['kk'], hp['ka'], hp['rk'], hp['lng'], hp['lnb'])


def _merge_kernel(h_ref, xa_ref, yw_ref, g_ref, xc_ref, zg_ref, wa_ref, wb_ref, wc_ref, bc_ref, wo_ref, o_ref):
    ya = _bdot(xa_ref[...], wa_ref[...])
    yb = _bdot(yw_ref[...] * g_ref[...], wb_ref[...])
    yc = _bdot(xc_ref[...], wc_ref[...]) + bc_ref[...]
    m = (_sigmoid(zg_ref[:, 0:D_MODEL]) * ya + _sigmoid(zg_ref[:, D_MODEL:2 * D_MODEL]) * yb
         + _sigmoid(zg_ref[:, 2 * D_MODEL:3 * D_MODEL]) * yc)
    o_ref[...] = h_ref[...] + _bdot(m, wo_ref[...])


def _merge(h, xa, yw, g, xc, z, P, li, tm):
    n = h.shape[0]
    tok = lambda width: pl.BlockSpec((tm, width), lambda i: (i, 0))
    par = lambda a: _lpar(a, li)
    ws = [P[n_] for n_ in ('sc_w_out', 'rk_w_out', 'cf_w_out', 'cf_b_out', 'w_o')]
    return pl.pallas_call(
        _merge_kernel,
        grid=(n // tm,),
        in_specs=[tok(D_MODEL), tok(D_SC), tok(D_MODEL), tok(D_MODEL), tok(D_CF), tok(Z_BLK)] + [par(a) for a in ws],
        out_specs=tok(D_MODEL),
        out_shape=jax.ShapeDtypeStruct((n, D_MODEL), jnp.float32),
        compiler_params=_cparams(("arbitrary",)),
        name="merge",
    )(h, xa, yw, g, xc, z, *ws)


def _mlp_ple_kernel(nk, final, h_ref, p_ref, g2_ref, w1_ref, w2_ref, gp_ref, wg_ref, wp_ref, *rest):
    gf_ref = rest[0] if final else None
    o_ref, hn_scr, acc_scr = rest[-3:]
    kstep = pl.program_id(1)

    @pl.when(kstep == 0)
    def _():
        hn_scr[...] = _rms(h_ref[...], g2_ref[...]).astype(jnp.bfloat16)
        acc_scr[...] = jnp.zeros_like(acc_scr)

    u = jnp.dot(hn_scr[...], w1_ref[...], preferred_element_type=jnp.float32)
    u = jnp.maximum(u, 0.0)
    acc_scr[...] += _bdot(u * u, w2_ref[...])

    @pl.when(kstep == nk - 1)
    def _():
        h1 = h_ref[...] + acc_scr[...]
        gate = _sigmoid(_bdot(_rms(h1, gp_ref[...]), wg_ref[...]))
        h2 = h1 + gate * _bdot(p_ref[...], wp_ref[...])
        o_ref[...] = h2 if gf_ref is None else _rms(h2, gf_ref[...])


def _mlp_ple(h, p, P, li, tm, tf, final_g=None):
    n = h.shape[0]
    nk = D_FF // tf
    par = lambda a: _lpar(a, li)
    extra = [] if final_g is None else [final_g]
    return pl.pallas_call(
        functools.partial(_mlp_ple_kernel, nk, final_g is not None),
        grid=(n // tm, nk),
        in_specs=[pl.BlockSpec((tm, D_MODEL), lambda i, k: (i, 0)),
                  pl.BlockSpec((None, tm, D_PLE), lambda i, k: (li, i, 0)),
                  par(P['norm2_g']),
                  pl.BlockSpec((None, D_MODEL, tf), lambda i, k: (li, 0, k)),
                  pl.BlockSpec((None, tf, D_MODEL), lambda i, k: (li, k, 0)),
                  par(P['ple_norm_g']), par(P['ple_gate_w']), par(P['ple_w'])]
                 + [pl.BlockSpec((1, D_MODEL), lambda i, k: (0, 0)) for _ in extra],
        out_specs=pl.BlockSpec((tm, D_MODEL), lambda i, k: (i, 0)),
        out_shape=jax.ShapeDtypeStruct((n, D_MODEL), jnp.float32),
        scratch_shapes=[pltpu.VMEM((tm, D_MODEL), jnp.bfloat16), pltpu.VMEM((tm, D_MODEL), jnp.float32)],
        compiler_params=_cparams(("arbitrary", "arbitrary")),
        name="mlp_ple",
    )(h, p, P['norm2_g'], P['mlp_w1'], P['mlp_w2'], P['ple_norm_g'], P['ple_gate_w'], P['ple_w'], *extra)


def _prep(W):
    bf = jnp.bfloat16
    nl = W['w_in'].shape[0]
    row = lambda a: a.reshape(nl, 1, -1)
    w_in = W['w_in']
    n_sc, n_rk, n_cf = 3 * D_SC, RK_PROJ, 2 * D_CF
    o_rk, o_cf, o_gate = n_sc, n_sc + n_rk, n_sc + n_rk + n_cf
    w_perm = jnp.concatenate([
        w_in[:, :, o_gate:o_gate + 3 * D_MODEL], w_in[:, :, o_rk:o_rk + 3 * D_MODEL], w_in[:, :, 0:n_sc],
        w_in[:, :, o_cf:o_cf + n_cf], w_in[:, :, o_rk + 3 * D_MODEL:o_rk + RK_PROJ],
        jnp.zeros((nl, D_MODEL, LORA_PAD - N_LORA), w_in.dtype)], axis=2).astype(bf)
    mu = W['rk_mu']
    zrows = lambda n: jnp.zeros((nl, n, D_MODEL), jnp.float32)
    o_a, o_g = DECAY_LORA, DECAY_LORA + AAA_LORA + MV_LORA
    seg_head = jnp.arange(SEG) // HEAD_DIM
    on_lanes = lambda a: jnp.broadcast_to(a.reshape(nl, HEADS, HEAD_DIM, 1), (nl, HEADS, HEAD_DIM, LANES))
    return {
        'head_lanes': {'kk': on_lanes(W['rk_k_k']), 'ka': on_lanes(W['rk_k_a']), 'rk': on_lanes(W['rk_r_k']),
                       'lng': on_lanes(W['rk_ln_g']), 'lnb': on_lanes(W['rk_ln_b'])},
        'ones': (seg_head[:, None] == seg_head[None, :]).astype(bf),
        'kk': row(W['rk_k_k']), 'ka': row(W['rk_k_a']), 'rk': row(W['rk_r_k']),
        'lng': row(W['rk_ln_g']), 'lnb': row(W['rk_ln_b']),
        'norm1_g': row(W['norm1_g']), 'w_in': w_perm,
        'mu1': row(mu[:, :3 * D_MODEL]),
        'mu2': row(jnp.pad(mu[:, 3 * D_MODEL:], ((0, 0), (0, LORA_PAD - N_LORA)))),
        'w0': row(W['rk_w0']), 'a0': row(W['rk_a0']), 'v0': row(W['rk_v0']),
        'w2p': jnp.concatenate([W['rk_w2'], zrows(128 - o_a)], axis=1).astype(bf),
        'a2p': jnp.concatenate([zrows(o_a), W['rk_a2']], axis=1).astype(bf),
        'v2p': jnp.concatenate([W['rk_v2'], zrows(128 - MV_LORA)], axis=1).astype(bf),
        'g2p': jnp.concatenate([zrows(o_g - 128), W['rk_g2'], zrows(384 - N_LORA)], axis=1).astype(bf),
        'scw': W['sc_conv_w'], 'cfbin': row(W['cf_b_in']), 'cfw': W['cf_dw_w'],
        'cfdb': row(W['cf_dw_b']), 'cflg': row(W['cf_ln_g']), 'cflb': row(W['cf_ln_b']),
        'sc_w_out': W['sc_w_out'].astype(bf), 'rk_w_out': W['rk_w_out'].astype(bf),
        'cf_w_out': W['cf_w_out'].astype(bf), 'cf_b_out': row(W['cf_b_out']), 'w_o': W['w_o'].astype(bf),
        'norm2_g': row(W['norm2_g']), 'mlp_w1': W['mlp_w1'].astype(bf), 'mlp_w2': W['mlp_w2'].astype(bf),
        'ple_norm_g': row(W['ple_norm_g']), 'ple_gate_w': W['ple_gate_w'].astype(bf),
        'ple_w': W['ple_w'].astype(bf),
    }


def _pick(n, pref):
    return pref if n % pref == 0 else n


def _trunk(x, p, sc0, shift0, wkv0, cf0, P, final_g, cfg):
    bsz, tlen, _ = x.shape
    n = bsz * tlen
    tm = _pick(n, cfg['tm'])
    tt = _pick(tlen, cfg['tt'])
    tpad = -tlen % CHUNK
    tw = _pick(tlen + tpad, cfg['tw'])
    pad_t = lambda a: jnp.pad(a, ((0, 0), (0, tpad), (0, 0))) if tpad else a
    h = x.reshape(n, D_MODEL)
    pf = p.reshape(DEPTH, n, D_PLE)
    sh1 = shift0[:, :, None, :3 * D_MODEL]
    sh2 = jnp.pad(shift0[:, :, 3 * D_MODEL:], ((0, 0), (0, 0), (0, LORA_PAD - N_LORA)))[:, :, None, :]
    lanes_form = bsz == LANES and tlen <= CHUNK
    to_lanes = lambda a: a.reshape(bsz, tlen, HEADS, HEAD_DIM).transpose(1, 2, 3, 0)
    wkv0_l = wkv0.transpose(0, 2, 3, 4, 1) if lanes_form else None
    v_first = None
    new_sc, new_shift, new_wkv, new_cf = [], [], [], []
    for i in range(DEPTH):
        z = _in_proj(h, P, i, tm)
        (r, w, k, v, a, g, xa, xc, nsh1, nsh2, nsc, ncf) = _seq(
            z.reshape(bsz, tlen, NZ), sh1, sh2, sc0, cf0, v_first, P, i, tt, cfg['nbs'])
        if i == 0:
            v_first = v
        if lanes_form:
            y, s_out = _wkv_lanes(to_lanes(r), to_lanes(w), to_lanes(k), to_lanes(v), to_lanes(a), wkv0_l, P, i)
            yw = y.transpose(3, 0, 1, 2).reshape(n, D_MODEL)
        else:
            y, s_out = _wkv(pad_t(r), pad_t(w), pad_t(k), pad_t(v), pad_t(a), wkv0, P, i, cfg['nb'], tw)
            yw = y[:, :tlen].reshape(n, D_MODEL)
        h = _merge(h, xa.reshape(n, D_SC), yw, g.reshape(n, D_MODEL), xc.reshape(n, D_CF), z, P, i, tm)
        h = _mlp_ple(h, pf, P, i, tm, cfg['tf'], final_g if i == DEPTH - 1 else None)
        new_sc.append(nsc)
        new_shift.append(jnp.concatenate([nsh1[:, 0, :], nsh2[:, 0, :N_LORA]], axis=-1))
        new_wkv.append(s_out)
        new_cf.append(ncf)
    y = h.reshape(bsz, tlen, D_MODEL)
    wkv_out = jnp.stack(new_wkv)
    if lanes_form:
        wkv_out = wkv_out.transpose(0, 4, 1, 2, 3)
    return y, jnp.stack(new_sc), jnp.stack(new_shift), wkv_out, jnp.stack(new_cf)


PROMPT_CFG = {'tm': 512, 'tt': 128, 'nbs': 1, 'tw': 64, 'nb': 8, 'tf': 2048}
SAMPLE_CFG = {'tm': 512, 'tt': 4, 'nbs': 8, 'tw': 8, 'nb': 8, 'tf': 1024}


def kernel(x_prompt, x_sample, p_prompt, p_sample, state_sconv, state_shift, state_wkv, state_cconv,
           norm1_g, w_in, sc_conv_w, sc_w_out, rk_mu, rk_w0, rk_w2, rk_a0, rk_a2, rk_v0, rk_v2, rk_g2,
           rk_k_k, rk_k_a, rk_r_k, rk_ln_g, rk_ln_b, rk_w_out, cf_b_in, cf_dw_w, cf_dw_b, cf_ln_g, cf_ln_b,
           cf_w_out, cf_b_out, w_o, norm2_g, mlp_w1, mlp_w2, ple_w, ple_gate_w, ple_norm_g, final_norm_g):
    W = dict(norm1_g=norm1_g, w_in=w_in, sc_conv_w=sc_conv_w, sc_w_out=sc_w_out, rk_mu=rk_mu, rk_w0=rk_w0,
             rk_w2=rk_w2, rk_a0=rk_a0, rk_a2=rk_a2, rk_v0=rk_v0, rk_v2=rk_v2, rk_g2=rk_g2, rk_k_k=rk_k_k,
             rk_k_a=rk_k_a, rk_r_k=rk_r_k, rk_ln_g=rk_ln_g, rk_ln_b=rk_ln_b, rk_w_out=rk_w_out, cf_b_in=cf_b_in,
             cf_dw_w=cf_dw_w, cf_dw_b=cf_dw_b, cf_ln_g=cf_ln_g, cf_ln_b=cf_ln_b, cf_w_out=cf_w_out,
             cf_b_out=cf_b_out, w_o=w_o, norm2_g=norm2_g, mlp_w1=mlp_w1, mlp_w2=mlp_w2, ple_w=ple_w,
             ple_gate_w=ple_gate_w, ple_norm_g=ple_norm_g)
    P = _prep(W)
    final_g = final_norm_g.reshape(1, D_MODEL)
    bp = x_prompt.shape[0]
    f32 = jnp.float32
    zsc = jnp.zeros((DEPTH, bp, SC_WIDTH - 1, D_SC), f32)
    zsh = jnp.zeros((DEPTH, bp, RK_PROJ), f32)
    zwkv = jnp.zeros((DEPTH, bp, HEADS, HEAD_DIM, HEAD_DIM), f32)
    zcf = jnp.zeros((DEPTH, bp, CF_WIDTH - 1, D_CF), f32)
    y_p, sc_p, sh_p, wkv_p, cf_p = _trunk(x_prompt, p_prompt, zsc, zsh, zwkv, zcf, P, final_g, PROMPT_CFG)
    y_s, sc_s, sh_s, wkv_s, cf_s = _trunk(x_sample, p_sample, state_sconv, state_shift, state_wkv, state_cconv,
                                          P, final_g, SAMPLE_CFG)
    return (y_p, y_s, sc_p, sh_p, wkv_p, cf_p, sc_s, sh_s, wkv_s, cf_s)
```

```python
import functools

import jax
import jax.numpy as jnp
from jax import lax
from jax.experimental import pallas as pl
from jax.experimental.pallas import tpu as pltpu

D_MODEL = 1024
DEPTH = 4
D_SC = 512
D_CF = 512
SC_WIDTH = 3
CF_WIDTH = 31
HEAD_DIM = 64
HEADS = 16
D_FF = 4096
D_PLE = 256
DECAY_LORA, AAA_LORA, MV_LORA, GATE_LORA = 64, 64, 32, 128
N_LORA = DECAY_LORA + AAA_LORA + MV_LORA + GATE_LORA
RK_PROJ = 3 * D_MODEL + N_LORA
RMS_EPS = 1e-6
LN_EPS = 1e-5
GN_EPS = 64e-5
EXP_M05 = 0.6065306597126334

Z_BLK = 3 * D_MODEL
NZ = 3 * Z_BLK
Z2_SC = 0
Z2_CF = 3 * D_SC
Z2_LORA = Z2_CF + 2 * D_CF
LORA_PAD = Z_BLK - Z2_LORA

VMEM_LIMIT_BYTES = 56 * 1024 * 1024


def _cparams(sem):
    return pltpu.CompilerParams(dimension_semantics=sem, vmem_limit_bytes=VMEM_LIMIT_BYTES)


def _sigmoid(x):
    return 1.0 / (1.0 + jnp.exp(-x))


def _rms(x, g):
    return x * lax.rsqrt(jnp.mean(x * x, axis=-1, keepdims=True) + RMS_EPS) * g


def _bdot(a, b):
    return jnp.dot(a.astype(jnp.bfloat16), b, preferred_element_type=jnp.float32)


def _in_proj_kernel(x_ref, g_ref, w_ref, o_ref):
    o_ref[...] = _bdot(_rms(x_ref[...], g_ref[...]), w_ref[...])


def _lpar(a, li):
    return pl.BlockSpec((None,) + a.shape[1:], lambda *_: (li,) + (0,) * (a.ndim - 1))


def _in_proj(x, P, li, tm):
    n = x.shape[0]
    g, w = P['norm1_g'], P['w_in']
    return pl.pallas_call(
        _in_proj_kernel,
        grid=(NZ // Z_BLK, n // tm),
        in_specs=[
            pl.BlockSpec((tm, D_MODEL), lambda j, i: (i, 0)),
            _lpar(g, li),
            pl.BlockSpec((None, D_MODEL, Z_BLK), lambda j, i: (li, 0, j)),
        ],
        out_specs=pl.BlockSpec((tm, Z_BLK), lambda j, i: (i, j)),
        out_shape=jax.ShapeDtypeStruct((n, NZ), jnp.float32),
        compiler_params=_cparams(("arbitrary", "arbitrary")),
        name="in_proj",
    )(x, g, w)


def _seq_batched(first_layer):
    n_in = 6 if first_layer else 7
    n_par = 15
    return set(range(n_in)) | set(range(n_in + n_par, n_in + n_par + 12 + 5))


SEQ_BATCHED = {True: _seq_batched(True), False: _seq_batched(False)}


def _seq_kernel(first_layer, tt, nbs, *refs):
    for bb in range(nbs):
        _seq_body(first_layer, tt, [r.at[bb] if i in SEQ_BATCHED[first_layer] else r for i, r in enumerate(refs)])


def _seq_body(first_layer, tt, refs):
    it = iter(refs)
    z1_ref, z2_ref, sh1_ref, sh2_ref, sc0_ref, cf0_ref = (next(it) for _ in range(6))
    vf_ref = None if first_layer else next(it)
    (mu1_ref, mu2_ref, w0_ref, w2_ref, a0_ref, a2_ref, v0_ref, v2_ref, g2_ref,
     scw_ref, cfbin_ref, cfw_ref, cfdb_ref, cflg_ref, cflb_ref) = (next(it) for _ in range(15))
    (r_ref, w_ref, k_ref, v_ref, a_ref, g_ref, xa_ref, xc_ref,
     nsh1_ref, nsh2_ref, nsc_ref, ncf_ref) = (next(it) for _ in range(12))
    s1buf, s2buf, cbuf, gbuf, pbuf = (next(it) for _ in range(5))

    @pl.when(pl.program_id(1) == 0)
    def _():
        s1buf[7:8, :] = sh1_ref[...]
        s2buf[7:8, :] = sh2_ref[...]
        cbuf[6:8, :] = sc0_ref[...]
        gbuf[2:CF_WIDTH + 1, :] = cf0_ref[...]
        gbuf[0:2, :] = jnp.zeros((2, D_CF), jnp.float32)
        gbuf[CF_WIDTH + 1 + tt:CF_WIDTH + 1 + tt + 8, :] = jnp.zeros((8, D_CF), jnp.float32)

    z1 = z1_ref[...]
    if tt % 8 == 0:
        rowi = lax.broadcasted_iota(jnp.int32, (tt, Z_BLK), 0)
        zp1 = jnp.where(rowi == 0, jnp.broadcast_to(s1buf[7:8, :], (tt, Z_BLK)), pltpu.roll(z1, 1, axis=0))
        s1buf[7 + tt:8 + tt, :] = z1[tt - 1:tt, :]
    else:
        s1buf[8:8 + tt, :] = z1
        zp1 = s1buf[7:7 + tt, :]
    zm1 = z1 + (zp1 - z1) * mu1_ref[...]
    z2l = z2_ref[:, Z2_LORA:Z_BLK]
    if tt % 8 == 0:
        rowl = lax.broadcasted_iota(jnp.int32, (tt, LORA_PAD), 0)
        zp2 = jnp.where(rowl == 0, jnp.broadcast_to(s2buf[7:8, :], (tt, LORA_PAD)),
                        pltpu.roll(z2l, 1, axis=0))
        s2buf[7 + tt:8 + tt, :] = z2l[tt - 1:tt, :]
    else:
        s2buf[8:8 + tt, :] = z2l
        zp2 = s2buf[7:7 + tt, :]
    zm2 = z2l + (zp2 - z2l) * mu2_ref[...]
    last1 = s1buf[7 + tt:8 + tt, :]
    last2 = s2buf[7 + tt:8 + tt, :]
    nsh1_ref[...] = last1
    nsh2_ref[...] = last2
    s1buf[7:8, :] = last1
    s2buf[7:8, :] = last2

    x01 = zm2[:, 0:128]
    lw = _bdot(jnp.tanh(x01), w2_ref[...])
    la = _bdot(x01, a2_ref[...])
    w_ref[...] = -EXP_M05 * _sigmoid(w0_ref[...] + lw)
    a_ref[...] = _sigmoid(a0_ref[...] + la)
    g_ref[...] = _bdot(_sigmoid(zm2[:, 128:384]), g2_ref[...])
    r_ref[...] = zm1[:, 0:D_MODEL]
    k_ref[...] = zm1[:, D_MODEL:2 * D_MODEL]
    v = zm1[:, 2 * D_MODEL:3 * D_MODEL]
    if not first_layer:
        lv = _bdot(zm2[:, 128:256], v2_ref[...])
        v = v + (vf_ref[...] - v) * _sigmoid(v0_ref[...] + lv)
    v_ref[...] = v

    cu = z2_ref[:, Z2_SC + D_SC:Z2_SC + 2 * D_SC] * z2_ref[:, Z2_SC + 2 * D_SC:Z2_SC + 3 * D_SC]
    cbuf[8:8 + tt, :] = cu
    conv_a = (scw_ref[0:1, :] * cbuf[6:6 + tt, :] + scw_ref[1:2, :] * cbuf[7:7 + tt, :]
              + scw_ref[2:3, :] * cu)
    xa_ref[...] = z2_ref[:, Z2_SC:Z2_SC + D_SC] * conv_a
    tail_a = cbuf[6 + tt:8 + tt, :]
    nsc_ref[...] = tail_a
    cbuf[6:8, :] = tail_a

    zc = z2_ref[:, Z2_CF:Z2_CF + 2 * D_CF] + cfbin_ref[...]
    glu = zc[:, 0:D_CF] * _sigmoid(zc[:, D_CF:2 * D_CF])
    gbuf[CF_WIDTH + 1:CF_WIDTH + 1 + tt, :] = glu
    if tt % 8 == 0:
        acc = None
        for ph in range(8):
            part = None
            for j in range(ph, CF_WIDTH + 2, 8):
                if j < 2:
                    continue
                term = cfw_ref[j - 2:j - 1, :] * gbuf[j - ph:j - ph + tt + 8, :]
                part = term if part is None else part + term
            pbuf[...] = part
            sh = pbuf[ph:ph + tt, :]
            acc = sh if acc is None else acc + sh
    else:
        acc = cfw_ref[0:1, :] * gbuf[2:2 + tt, :]
        for kk in range(1, CF_WIDTH):
            acc = acc + cfw_ref[kk:kk + 1, :] * gbuf[2 + kk:2 + kk + tt, :]
    cc = acc + cfdb_ref[...]
    mu = jnp.mean(cc, axis=-1, keepdims=True)
    cd = cc - mu
    var = jnp.mean(cd * cd, axis=-1, keepdims=True)
    ln = cd * lax.rsqrt(var + LN_EPS) * cflg_ref[...] + cflb_ref[...]
    xc_ref[...] = ln * _sigmoid(ln)
    tail_c = gbuf[2 + tt:CF_WIDTH + 1 + tt, :]
    ncf_ref[...] = tail_c
    gbuf[2:CF_WIDTH + 1, :] = tail_c


def _seq(z, sh1, sh2, sc0, cf0, vfirst, P, li, tt, nbs):
    bsz, tlen, _ = z.shape
    first_layer = vfirst is None
    tok = lambda width, blk: pl.BlockSpec((nbs, tt, width), lambda b, t, blk=blk: (b, t, blk))
    st = lambda rows, width: pl.BlockSpec((nbs, rows, width), lambda b, t: (b, 0, 0))
    lst = lambda rows, width: pl.BlockSpec((None, nbs, rows, width), lambda b, t: (li, b, 0, 0))
    par = lambda a: _lpar(a, li)
    params = [P[n] for n in ('mu1', 'mu2', 'w0', 'w2p', 'a0', 'a2p', 'v0', 'v2p', 'g2p',
                              'scw', 'cfbin', 'cfw', 'cfdb', 'cflg', 'cflb')]
    args = [z, z, sh1, sh2, sc0, cf0]
    in_specs = [tok(Z_BLK, 1), tok(Z_BLK, 2), lst(1, Z_BLK), lst(1, LORA_PAD),
                lst(SC_WIDTH - 1, D_SC), lst(CF_WIDTH - 1, D_CF)]
    if not first_layer:
        args.append(vfirst)
        in_specs.append(tok(D_MODEL, 0))
    args += params
    in_specs += [par(a) for a in params]
    f32 = jnp.float32
    out_shape = ([jax.ShapeDtypeStruct((bsz, tlen, D_MODEL), f32)] * 6
                 + [jax.ShapeDtypeStruct((bsz, tlen, D_SC), f32), jax.ShapeDtypeStruct((bsz, tlen, D_CF), f32),
                    jax.ShapeDtypeStruct((bsz, 1, Z_BLK), f32), jax.ShapeDtypeStruct((bsz, 1, LORA_PAD), f32),
                    jax.ShapeDtypeStruct((bsz, SC_WIDTH - 1, D_SC), f32),
                    jax.ShapeDtypeStruct((bsz, CF_WIDTH - 1, D_CF), f32)])
    out_specs = ([tok(D_MODEL, 0)] * 6 + [tok(D_SC, 0), tok(D_CF, 0), st(1, Z_BLK), st(1, LORA_PAD),
                                           st(SC_WIDTH - 1, D_SC), st(CF_WIDTH - 1, D_CF)])
    scratch = [pltpu.VMEM((nbs, 8 + tt, Z_BLK), f32), pltpu.VMEM((nbs, 8 + tt, LORA_PAD), f32),
               pltpu.VMEM((nbs, 8 + tt, D_SC), f32), pltpu.VMEM((nbs, CF_WIDTH + 1 + tt + 8, D_CF), f32),
               pltpu.VMEM((nbs, tt + 8, D_CF), f32)]
    return pl.pallas_call(
        functools.partial(_seq_kernel, first_layer, tt, nbs),
        grid=(bsz // nbs, tlen // tt),
        in_specs=in_specs, out_specs=out_specs, out_shape=out_shape, scratch_shapes=scratch,
        compiler_params=_cparams(("arbitrary", "arbitrary")),
        name="seq",
    )(*args)


CHUNK = 8
PAIR = 2 * HEAD_DIM
N_PAIR = HEADS // 2
SEG = 256
P_ROWS = 4 * CHUNK * CHUNK - 2 * CHUNK
R_OFF = [32 * s + 16 if s < CHUNK - 1 else 32 * s for s in range(CHUNK)]


def _segsum(x, ones):
    xb = x.astype(jnp.bfloat16)
    return jnp.concatenate([jnp.dot(xb[:, q * SEG:(q + 1) * SEG], ones, preferred_element_type=jnp.float32)
                            for q in range(D_MODEL // SEG)], axis=1)


def _wkv_kernel(nb, tt, r_ref, ld_ref, k_ref, v_ref, a_ref, s0_ref, kkp_ref, kap_ref, rkp_ref, lng_ref, lnb_ref,
                ones_ref, y_ref, sout_ref, s_scr, kk_scr, km_scr, o_scr, p_scr, e_scr):
    f32, bf16 = jnp.float32, jnp.bfloat16
    ones = ones_ref[...]

    @pl.when(pl.program_id(1) == 0)
    def _():
        for bb in range(nb):
            for p in range(N_PAIR):
                s_scr[bb, p] = jnp.concatenate([s0_ref[bb, 2 * p], s0_ref[bb, 2 * p + 1]], axis=1)

    flat = lambda ref: ref[...].reshape(nb * tt, D_MODEL)
    kraw = flat(k_ref)
    kkp = kraw * kkp_ref[...]
    kk_scr[...] = (kkp * lax.rsqrt(jnp.maximum(_segsum(kkp * kkp, ones), 1e-24))).reshape(nb, tt, D_MODEL)
    km_scr[...] = (kraw * (1.0 + (flat(a_ref) - 1.0) * kap_ref[...])).reshape(nb, tt, D_MODEL)

    row = lax.broadcasted_iota(jnp.int32, (CHUNK, D_MODEL), 0)
    head0 = (lax.broadcasted_iota(jnp.int32, (2 * CHUNK, D_MODEL), 1) // HEAD_DIM) % 2 == 0
    bcast = lambda x, s: jnp.broadcast_to(x[s:s + 1, :], (CHUNK, D_MODEL))

    def chunk(c, carry):
        rows = pl.ds(pl.multiple_of(c * CHUNK, CHUNK), CHUNK)
        st = []
        for bb in range(nb):
            r = r_ref[bb, rows, :]
            ld = ld_ref[bb, rows, :]
            v = v_ref[bb, rows, :]
            kk = kk_scr[bb, rows, :]
            cum = ld
            for sh in (1, 2, 4):
                cum = cum + jnp.where(row >= sh, pltpu.roll(cum, sh, axis=0), 0.0)
            w_cum = jnp.exp(cum)
            w_inv = jnp.exp(-cum)
            at = -kk * jnp.exp(cum - ld)
            rt = r * w_cum
            bt = kk * a_ref[bb, rows, :] * w_inv
            kt = km_scr[bb, rows, :] * w_inv
            wc = bcast(w_cum, CHUNK - 1)
            lhs = jnp.concatenate([at, rt], axis=0)
            lhs = jnp.concatenate([jnp.where(head0, lhs, 0.0), jnp.where(head0, 0.0, lhs)], axis=0).astype(bf16)
            st.append([v, at, rt, bt, kt, wc, lhs])
        for bb in range(nb):
            v, at, rt, bt, kt, wc, lhs = st[bb]
            xy = []
            for p in range(N_PAIR):
                sl = slice(p * PAIR, (p + 1) * PAIR)
                xh = lax.dot_general(lhs[:, sl], s_scr[bb, p].astype(bf16), (((1,), (1,)), ((), ())),
                                     preferred_element_type=f32)
                xy.append(xh[0:2 * CHUNK])
                xy.append(xh[2 * CHUNK:4 * CHUNK])
            st[bb][6] = jnp.concatenate(xy, axis=1)
            for s in range(CHUNK):
                bs, ks = bcast(bt, s), bcast(kt, s)
                rm = jnp.where(row >= s, rt, 0.0) if s else rt
                if s < CHUNK - 1:
                    am = jnp.where(row > s, at, 0.0)
                    p_scr[bb, 32 * s:32 * s + 16, :] = jnp.concatenate([bs * am, ks * am], axis=0).astype(bf16)
                p_scr[bb, R_OFF[s]:R_OFF[s] + 16, :] = jnp.concatenate([bs * rm, ks * rm], axis=0).astype(bf16)
            for q in range(D_MODEL // SEG):
                sl = slice(q * SEG, (q + 1) * SEG)
                e_scr[bb, :, sl] = jnp.dot(p_scr[bb, :, sl], ones, preferred_element_type=f32)
        accs = []
        for bb in range(nb):
            v, at, rt, bt, kt, wc, xy = st[bb]
            pre_u = pre_o = None
            for s in range(CHUNK):
                vs = bcast(v, s)
                to = e_scr[bb, R_OFF[s] + 8:R_OFF[s] + 16, :] * vs
                pre_o = to if pre_o is None else pre_o + to
                if s < CHUNK - 1:
                    tu = e_scr[bb, 32 * s + 8:32 * s + 16, :] * vs
                    pre_u = tu if pre_u is None else pre_u + tu
            acc_u = xy[0:CHUNK] + pre_u
            acc_o = xy[CHUNK:2 * CHUNK] + pre_o
            for s in range(CHUNK):
                us = bcast(acc_u, s)
                if s < CHUNK - 1:
                    acc_u = acc_u + e_scr[bb, 32 * s:32 * s + 8, :] * us
                acc_o = acc_o + e_scr[bb, R_OFF[s]:R_OFF[s] + 8, :] * us
            o_scr[bb, rows, :] = acc_o
            accs.append(acc_u)
        for bb in range(nb):
            v, at, rt, bt, kt, wc, xy = st[bb]
            uv = jnp.concatenate([accs[bb], v], axis=0)
            uv_swapped = pltpu.roll(uv, D_MODEL - HEAD_DIM, axis=1)
            uv = jnp.concatenate([jnp.where(head0, uv, 0.0), jnp.where(head0, uv_swapped, 0.0)], axis=0).astype(bf16)
            bk = jnp.concatenate([bt * wc, kt * wc], axis=0)
            bk = jnp.concatenate([jnp.where(head0, bk, 0.0), jnp.where(head0, 0.0, bk)], axis=0).astype(bf16)
            for p in range(N_PAIR):
                sl = slice(p * PAIR, (p + 1) * PAIR)
                upd = lax.dot_general(uv[:, p * PAIR:p * PAIR + HEAD_DIM], bk[:, sl], (((0,), (0,)), ((), ())),
                                      preferred_element_type=f32)
                s_scr[bb, p] = s_scr[bb, p] * wc[0:1, sl] + upd
        return carry

    lax.fori_loop(0, tt // CHUNK, chunk, 0)

    o = flat(o_scr)
    mu = _segsum(o, ones) * (1.0 / HEAD_DIM)
    od = o - mu
    var = _segsum(od * od, ones) * (1.0 / HEAD_DIM)
    y = od * lax.rsqrt(var + GN_EPS) * lng_ref[...] + lnb_ref[...]
    bonus = _segsum(flat(r_ref) * flat(km_scr) * rkp_ref[...], ones)
    y_ref[...] = (y + bonus * flat(v_ref)).reshape(nb, tt, D_MODEL)

    @pl.when(pl.program_id(1) == pl.num_programs(1) - 1)
    def _():
        for bb in range(nb):
            for p in range(N_PAIR):
                sp = s_scr[bb, p]
                sout_ref[bb, 2 * p] = sp[:, 0:HEAD_DIM]
                sout_ref[bb, 2 * p + 1] = sp[:, HEAD_DIM:PAIR]


def _wkv(r, ld, k, v, a, s0, P, li, nb, tt):
    bsz, tlen, _ = r.shape
    f32 = jnp.float32
    tok = pl.BlockSpec((nb, tt, D_MODEL), lambda b, t: (b, t, 0))
    stt = pl.BlockSpec((nb, HEADS, HEAD_DIM, HEAD_DIM), lambda b, t: (b, 0, 0, 0))
    lstt = pl.BlockSpec((None, nb, HEADS, HEAD_DIM, HEAD_DIM), lambda b, t: (li, b, 0, 0, 0))
    par = _lpar(P['kk'], li)
    return pl.pallas_call(
        functools.partial(_wkv_kernel, nb, tt),
        grid=(bsz // nb, tlen // tt),
        in_specs=[tok] * 5 + [lstt] + [par] * 5 + [pl.BlockSpec((SEG, SEG), lambda b, t: (0, 0))],
        out_specs=[tok, stt],
        out_shape=[jax.ShapeDtypeStruct((bsz, tlen, D_MODEL), f32),
                   jax.ShapeDtypeStruct((bsz, HEADS, HEAD_DIM, HEAD_DIM), f32)],
        scratch_shapes=[pltpu.VMEM((nb, N_PAIR, HEAD_DIM, PAIR), f32),
                        pltpu.VMEM((nb, tt, D_MODEL), f32), pltpu.VMEM((nb, tt, D_MODEL), f32),
                        pltpu.VMEM((nb, tt, D_MODEL), f32),
                        pltpu.VMEM((nb, P_ROWS, D_MODEL), jnp.bfloat16),
                        pltpu.VMEM((nb, P_ROWS, D_MODEL), f32)],
        compiler_params=_cparams(("arbitrary", "arbitrary")),
        name="wkv",
    )(r, ld, k, v, a, s0, P['kk'], P['ka'], P['rk'], P['lng'], P['lnb'], P['ones'])


LANES = 128


def _wkv_lanes_kernel(tlen, r_ref, ld_ref, k_ref, v_ref, a_ref, s0_ref, kkp_ref, kap_ref, rkp_ref, lng_ref, lnb_ref,
                      y_ref, s_ref, o_scr):
    s_ref[...] = s0_ref[...]
    for t in range(tlen):
        rt = r_ref[t]
        wt = jnp.exp(ld_ref[t])
        kraw = k_ref[t]
        at = a_ref[t]
        kkp = kraw * kkp_ref[...]
        kk = kkp * lax.rsqrt(jnp.maximum(jnp.sum(kkp * kkp, axis=0, keepdims=True), 1e-24))
        kt = kraw * (1.0 + (at - 1.0) * kap_ref[...])
        bt = kk * at
        nkk = -kk
        for i in range(HEAD_DIM):
            si = s_ref[i]
            sa = jnp.sum(si * nkk, axis=0, keepdims=True)
            sn = si * wt + sa * bt + v_ref[t, i:i + 1, :] * kt
            s_ref[i] = sn
            o_scr[i:i + 1, :] = jnp.sum(sn * rt, axis=0, keepdims=True)
        o = o_scr[...]
        mu = jnp.mean(o, axis=0, keepdims=True)
        od = o - mu
        var = jnp.mean(od * od, axis=0, keepdims=True)
        y = od * lax.rsqrt(var + GN_EPS) * lng_ref[...] + lnb_ref[...]
        bonus = jnp.sum(rt * kt * rkp_ref[...], axis=0, keepdims=True)
        y_ref[t] = y + bonus * v_ref[t]


def _wkv_lanes(r, ld, k, v, a, s0, P, li):
    tlen, _, _, bsz = r.shape
    f32 = jnp.float32
    tok = pl.BlockSpec((tlen, None, HEAD_DIM, bsz), lambda h: (0, h, 0, 0))
    lstt = pl.BlockSpec((None, None, HEAD_DIM, HEAD_DIM, bsz), lambda h: (li, h, 0, 0, 0))
    stt = pl.BlockSpec((None, HEAD_DIM, HEAD_DIM, bsz), lambda h: (h, 0, 0, 0))
    par = pl.BlockSpec((None, None, HEAD_DIM, bsz), lambda h: (li, h, 0, 0))
    hp = P['head_lanes']
    return pl.pallas_call(
        functools.partial(_wkv_lanes_kernel, tlen),
        grid=(HEADS,),
        in_specs=[tok] * 5 + [lstt] + [par] * 5,
        out_specs=[tok, stt],
        out_shape=[jax.ShapeDtypeStruct((tlen, HEADS, HEAD_DIM, bsz), f32),
                   jax.ShapeDtypeStruct((HEADS, HEAD_DIM, HEAD_DIM, bsz), f32)],
        scratch_shapes=[pltpu.VMEM((HEAD_DIM, bsz), f32)],
        compiler_params=_cparams(("arbitrary",)),
        name="wkv_lanes",
    )(r, ld, k, v, a, s0, hp['kk'], hp['ka'], hp['rk'], hp['lng'], hp['lnb'])


def _merge_kernel(h_ref, xa_ref, yw_ref, g_ref, xc_ref, zg_ref, wa_ref, wb_ref, wc_ref, bc_ref, wo_ref, o_ref):
    ya = _bdot(xa_ref[...], wa_ref[...])
    yb = _bdot(yw_ref[...] * g_ref[...], wb_ref[...])
    yc = _bdot(xc_ref[...], wc_ref[...]) + bc_ref[...]
    m = (_sigmoid(zg_ref[:, 0:D_MODEL]) * ya + _sigmoid(zg_ref[:, D_MODEL:2 * D_MODEL]) * yb
         + _sigmoid(zg_ref[:, 2 * D_MODEL:3 * D_MODEL]) * yc)
    o_ref[...] = h_ref[...] + _bdot(m, wo_ref[...])


def _merge(h, xa, yw, g, xc, z, P, li, tm):
    n = h.shape[0]
    tok = lambda width: pl.BlockSpec((tm, width), lambda i: (i, 0))
    par = lambda a: _lpar(a, li)
    ws = [P[n_] for n_ in ('sc_w_out', 'rk_w_out', 'cf_w_out', 'cf_b_out', 'w_o')]
    return pl.pallas_call(
        _merge_kernel,
        grid=(n // tm,),
        in_specs=[tok(D_MODEL), tok(D_SC), tok(D_MODEL), tok(D_MODEL), tok(D_CF), tok(Z_BLK)] + [par(a) for a in ws],
        out_specs=tok(D_MODEL),
        out_shape=jax.ShapeDtypeStruct((n, D_MODEL), jnp.float32),
        compiler_params=_cparams(("arbitrary",)),
        name="merge",
    )(h, xa, yw, g, xc, z, *ws)


def _mlp_ple_kernel(nk, final, h_ref, p_ref, g2_ref, w1_ref, w2_ref, gp_ref, wg_ref, wp_ref, *rest):
    gf_ref = rest[0] if final else None
    o_ref, hn_scr, acc_scr = rest[-3:]
    kstep = pl.program_id(1)

    @pl.when(kstep == 0)
    def _():
        hn_scr[...] = _rms(h_ref[...], g2_ref[...]).astype(jnp.bfloat16)
        acc_scr[...] = jnp.zeros_like(acc_scr)

    u = jnp.dot(hn_scr[...], w1_ref[...], preferred_element_type=jnp.float32)
    u = jnp.maximum(u, 0.0)
    acc_scr[...] += _bdot(u * u, w2_ref[...])

    @pl.when(kstep == nk - 1)
    def _():
        h1 = h_ref[...] + acc_scr[...]
        gate = _sigmoid(_bdot(_rms(h1, gp_ref[...]), wg_ref[...]))
        h2 = h1 + gate * _bdot(p_ref[...], wp_ref[...])
        o_ref[...] = h2 if gf_ref is None else _rms(h2, gf_ref[...])


def _mlp_ple(h, p, P, li, tm, tf, final_g=None):
    n = h.shape[0]
    nk = D_FF // tf
    par = lambda a: _lpar(a, li)
    extra = [] if final_g is None else [final_g]
    return pl.pallas_call(
        functools.partial(_mlp_ple_kernel, nk, final_g is not None),
        grid=(n // tm, nk),
        in_specs=[pl.BlockSpec((tm, D_MODEL), lambda i, k: (i, 0)),
                  pl.BlockSpec((None, tm, D_PLE), lambda i, k: (li, i, 0)),
                  par(P['norm2_g']),
                  pl.BlockSpec((None, D_MODEL, tf), lambda i, k: (li, 0, k)),
                  pl.BlockSpec((None, tf, D_MODEL), lambda i, k: (li, k, 0)),
                  par(P['ple_norm_g']), par(P['ple_gate_w']), par(P['ple_w'])]
                 + [pl.BlockSpec((1, D_MODEL), lambda i, k: (0, 0)) for _ in extra],
        out_specs=pl.BlockSpec((tm, D_MODEL), lambda i, k: (i, 0)),
        out_shape=jax.ShapeDtypeStruct((n, D_MODEL), jnp.float32),
        scratch_shapes=[pltpu.VMEM((tm, D_MODEL), jnp.bfloat16), pltpu.VMEM((tm, D_MODEL), jnp.float32)],
        compiler_params=_cparams(("arbitrary", "arbitrary")),
        name="mlp_ple",
    )(h, p, P['norm2_g'], P['mlp_w1'], P['mlp_w2'], P['ple_norm_g'], P['ple_gate_w'], P['ple_w'], *extra)


def _prep(W):
    bf = jnp.bfloat16
    nl = W['w_in'].shape[0]
    row = lambda a: a.reshape(nl, 1, -1)
    w_in = W['w_in']
    n_sc, n_rk, n_cf = 3 * D_SC, RK_PROJ, 2 * D_CF
    o_rk, o_cf, o_gate = n_sc, n_sc + n_rk, n_sc + n_rk + n_cf
    w_perm = jnp.concatenate([
        w_in[:, :, o_gate:o_gate + 3 * D_MODEL], w_in[:, :, o_rk:o_rk + 3 * D_MODEL], w_in[:, :, 0:n_sc],
        w_in[:, :, o_cf:o_cf + n_cf], w_in[:, :, o_rk + 3 * D_MODEL:o_rk + RK_PROJ],
        jnp.zeros((nl, D_MODEL, LORA_PAD - N_LORA), w_in.dtype)], axis=2).astype(bf)
    mu = W['rk_mu']
    zrows = lambda n: jnp.zeros((nl, n, D_MODEL), jnp.float32)
    o_a, o_g = DECAY_LORA, DECAY_LORA + AAA_LORA + MV_LORA
    seg_head = jnp.arange(SEG) // HEAD_DIM
    on_lanes = lambda a: jnp.broadcast_to(a.reshape(nl, HEADS, HEAD_DIM, 1), (nl, HEADS, HEAD_DIM, LANES))
    return {
        'head_lanes': {'kk': on_lanes(W['rk_k_k']), 'ka': on_lanes(W['rk_k_a']), 'rk': on_lanes(W['rk_r_k']),
                       'lng': on_lanes(W['rk_ln_g']), 'lnb': on_lanes(W['rk_ln_b'])},
        'ones': (seg_head[:, None] == seg_head[None, :]).astype(bf),
        'kk': row(W['rk_k_k']), 'ka': row(W['rk_k_a']), 'rk': row(W['rk_r_k']),
        'lng': row(W['rk_ln_g']), 'lnb': row(W['rk_ln_b']),
        'norm1_g': row(W['norm1_g']), 'w_in': w_perm,
        'mu1': row(mu[:, :3 * D_MODEL]),
        'mu2': row(jnp.pad(mu[:, 3 * D_MODEL:], ((0, 0), (0, LORA_PAD - N_LORA)))),
        'w0': row(W['rk_w0']), 'a0': row(W['rk_a0']), 'v0': row(W['rk_v0']),
        'w2p': jnp.concatenate([W['rk_w2'], zrows(128 - o_a)], axis=1).astype(bf),
        'a2p': jnp.concatenate([zrows(o_a), W['rk_a2']], axis=1).astype(bf),
        'v2p': jnp.concatenate([W['rk_v2'], zrows(128 - MV_LORA)], axis=1).astype(bf),
        'g2p': jnp.concatenate([zrows(o_g - 128), W['rk_g2'], zrows(384 - N_LORA)], axis=1).astype(bf),
        'scw': W['sc_conv_w'], 'cfbin': row(W['cf_b_in']), 'cfw': W['cf_dw_w'],
        'cfdb': row(W['cf_dw_b']), 'cflg': row(W['cf_ln_g']), 'cflb': row(W['cf_ln_b']),
        'sc_w_out': W['sc_w_out'].astype(bf), 'rk_w_out': W['rk_w_out'].astype(bf),
        'cf_w_out': W['cf_w_out'].astype(bf), 'cf_b_out': row(W['cf_b_out']), 'w_o': W['w_o'].astype(bf),
        'norm2_g': row(W['norm2_g']), 'mlp_w1': W['mlp_w1'].astype(bf), 'mlp_w2': W['mlp_w2'].astype(bf),
        'ple_norm_g': row(W['ple_norm_g']), 'ple_gate_w': W['ple_gate_w'].astype(bf),
        'ple_w': W['ple_w'].astype(bf),
    }


def _pick(n, pref):
    return pref if n % pref == 0 else n


def _trunk(x, p, sc0, shift0, wkv0, cf0, P, final_g, cfg):
    bsz, tlen, _ = x.shape
    n = bsz * tlen
    tm = _pick(n, cfg['tm'])
    tt = _pick(tlen, cfg['tt'])
    tpad = -tlen % CHUNK
    tw = _pick(tlen + tpad, cfg['tw'])
    pad_t = lambda a: jnp.pad(a, ((0, 0), (0, tpad), (0, 0))) if tpad else a
    h = x.reshape(n, D_MODEL)
    pf = p.reshape(DEPTH, n, D_PLE)
    sh1 = shift0[:, :, None, :3 * D_MODEL]
    sh2 = jnp.pad(shift0[:, :, 3 * D_MODEL:], ((0, 0), (0, 0), (0, LORA_PAD - N_LORA)))[:, :, None, :]
    lanes_form = bsz == LANES and tlen <= CHUNK
    to_lanes = lambda a: a.reshape(bsz, tlen, HEADS, HEAD_DIM).transpose(1, 2, 3, 0)
    wkv0_l = wkv0.transpose(0, 2, 3, 4, 1) if lanes_form else None
    v_first = None
    new_sc, new_shift, new_wkv, new_cf = [], [], [], []
    for i in range(DEPTH):
        z = _in_proj(h, P, i, tm)
        (r, w, k, v, a, g, xa, xc, nsh1, nsh2, nsc, ncf) = _seq(
            z.reshape(bsz, tlen, NZ), sh1, sh2, sc0, cf0, v_first, P, i, tt, cfg['nbs'])
        if i == 0:
            v_first = v
        if lanes_form:
            y, s_out = _wkv_lanes(to_lanes(r), to_lanes(w), to_lanes(k), to_lanes(v), to_lanes(a), wkv0_l, P, i)
            yw = y.transpose(3, 0, 1, 2).reshape(n, D_MODEL)
        else:
            y, s_out = _wkv(pad_t(r), pad_t(w), pad_t(k), pad_t(v), pad_t(a), wkv0, P, i, cfg['nb'], tw)
            yw = y[:, :tlen].reshape(n, D_MODEL)
        h = _merge(h, xa.reshape(n, D_SC), yw, g.reshape(n, D_MODEL), xc.reshape(n, D_CF), z, P, i, tm)
        h = _mlp_ple(h, pf, P, i, tm, cfg['tf'], final_g if i == DEPTH - 1 else None)
        new_sc.append(nsc)
        new_shift.append(jnp.concatenate([nsh1[:, 0, :], nsh2[:, 0, :N_LORA]], axis=-1))
        new_wkv.append(s_out)
        new_cf.append(ncf)
    y = h.reshape(bsz, tlen, D_MODEL)
    wkv_out = jnp.stack(new_wkv)
    if lanes_form:
        wkv_out = wkv_out.transpose(0, 4, 1, 2, 3)
    return y, jnp.stack(new_sc), jnp.stack(new_shift), wkv_out, jnp.stack(new_cf)


PROMPT_CFG = {'tm': 512, 'tt': 128, 'nbs': 1, 'tw': 64, 'nb': 8, 'tf': 2048}
SAMPLE_CFG = {'tm': 512, 'tt': 4, 'nbs': 8, 'tw': 8, 'nb': 8, 'tf': 1024}


def kernel(x_prompt, x_sample, p_prompt, p_sample, state_sconv, state_shift, state_wkv, state_cconv,
           norm1_g, w_in, sc_conv_w, sc_w_out, rk_mu, rk_w0, rk_w2, rk_a0, rk_a2, rk_v0, rk_v2, rk_g2,
           rk_k_k, rk_k_a, rk_r_k, rk_ln_g, rk_ln_b, rk_w_out, cf_b_in, cf_dw_w, cf_dw_b, cf_ln_g, cf_ln_b,
           cf_w_out, cf_b_out, w_o, norm2_g, mlp_w1, mlp_w2, ple_w, ple_gate_w, ple_norm_g, final_norm_g):
    W = dict(norm1_g=norm1_g, w_in=w_in, sc_conv_w=sc_conv_w, sc_w_out=sc_w_out, rk_mu=rk_mu, rk_w0=rk_w0,
             rk_w2=rk_w2, rk_a0=rk_a0, rk_a2=rk_a2, rk_v0=rk_v0, rk_v2=rk_v2, rk_g2=rk_g2, rk_k_k=rk_k_k,
             rk_k_a=rk_k_a, rk_r_k=rk_r_k, rk_ln_g=rk_ln_g, rk_ln_b=rk_ln_b, rk_w_out=rk_w_out, cf_b_in=cf_b_in,
             cf_dw_w=cf_dw_w, cf_dw_b=cf_dw_b, cf_ln_g=cf_ln_g, cf_ln_b=cf_ln_b, cf_w_out=cf_w_out,
             cf_b_out=cf_b_out, w_o=w_o, norm2_g=norm2_g, mlp_w1=mlp_w1, mlp_w2=mlp_w2, ple_w=ple_w,
             ple_gate_w=ple_gate_w, ple_norm_g=ple_norm_g)
    P = _prep(W)
    final_g = final_norm_g.reshape(1, D_MODEL)
    bp = x_prompt.shape[0]
    f32 = jnp.float32
    zsc = jnp.zeros((DEPTH, bp, SC_WIDTH - 1, D_SC), f32)
    zsh = jnp.zeros((DEPTH, bp, RK_PROJ), f32)
    zwkv = jnp.zeros((DEPTH, bp, HEADS, HEAD_DIM, HEAD_DIM), f32)
    zcf = jnp.zeros((DEPTH, bp, CF_WIDTH - 1, D_CF), f32)
    y_p, sc_p, sh_p, wkv_p, cf_p = _trunk(x_prompt, p_prompt, zsc, zsh, zwkv, zcf, P, final_g, PROMPT_CFG)
    y_s, sc_s, sh_s, wkv_s, cf_s = _trunk(x_sample, p_sample, state_sconv, state_shift, state_wkv, state_cconv,
                                          P, final_g, SAMPLE_CFG)
    return (y_p, y_s, sc_p, sh_p, wkv_p, cf_p, sc_s, sh_s, wkv_s, cf_s)
```

```python
import functools

import jax
import jax.numpy as jnp
from jax import lax
from jax.experimental import pallas as pl
from jax.experimental.pallas import tpu as pltpu

D_MODEL = 1024
DEPTH = 4
D_SC = 512
D_CF = 512
SC_WIDTH = 3
CF_WIDTH = 31
HEAD_DIM = 64
HEADS = 16
D_FF = 4096
D_PLE = 256
DECAY_LORA, AAA_LORA, MV_LORA, GATE_LORA = 64, 64, 32, 128
N_LORA = DECAY_LORA + AAA_LORA + MV_LORA + GATE_LORA
RK_PROJ = 3 * D_MODEL + N_LORA
RMS_EPS = 1e-6
LN_EPS = 1e-5
GN_EPS = 64e-5
EXP_M05 = 0.6065306597126334

Z_BLK = 3 * D_MODEL
NZ = 3 * Z_BLK
Z2_SC = 0
Z2_CF = 3 * D_SC
Z2_LORA = Z2_CF + 2 * D_CF
LORA_PAD = Z_BLK - Z2_LORA

VMEM_LIMIT_BYTES = 56 * 1024 * 1024


def _cparams(sem):
    return pltpu.CompilerParams(dimension_semantics=sem, vmem_limit_bytes=VMEM_LIMIT_BYTES)


def _sigmoid(x):
    return 1.0 / (1.0 + jnp.exp(-x))


def _rms(x, g):
    return x * lax.rsqrt(jnp.mean(x * x, axis=-1, keepdims=True) + RMS_EPS) * g


def _bdot(a, b):
    return jnp.dot(a.astype(jnp.bfloat16), b, preferred_element_type=jnp.float32)


def _in_proj_kernel(x_ref, g_ref, w_ref, o_ref):
    o_ref[...] = _bdot(_rms(x_ref[...], g_ref[...]), w_ref[...])


def _lpar(a, li):
    return pl.BlockSpec((None,) + a.shape[1:], lambda *_: (li,) + (0,) * (a.ndim - 1))


def _in_proj(x, P, li, tm):
    n = x.shape[0]
    g, w = P['norm1_g'], P['w_in']
    return pl.pallas_call(
        _in_proj_kernel,
        grid=(NZ // Z_BLK, n // tm),
        in_specs=[
            pl.BlockSpec((tm, D_MODEL), lambda j, i: (i, 0)),
            _lpar(g, li),
            pl.BlockSpec((None, D_MODEL, Z_BLK), lambda j, i: (li, 0, j)),
        ],
        out_specs=pl.BlockSpec((tm, Z_BLK), lambda j, i: (i, j)),
        out_shape=jax.ShapeDtypeStruct((n, NZ), jnp.float32),
        compiler_params=_cparams(("arbitrary", "arbitrary")),
        name="in_proj",
    )(x, g, w)


SEQ_N_PARAMS = 15


def _seq_kernel(first_layer, tt, nbs, *refs):
    n_in = 6 if first_layer else 7
    tok_refs = set(range(n_in + SEQ_N_PARAMS, n_in + SEQ_N_PARAMS + 8))
    par_refs = set(range(n_in, n_in + SEQ_N_PARAMS))
    for bb in range(nbs):
        views = []
        for i, r in enumerate(refs):
            if i in tok_refs:
                views.append(r.at[pl.ds(bb * tt, tt)])
            elif i in par_refs:
                views.append(r)
            else:
                views.append(r.at[bb])
        _seq_body(first_layer, tt, views)


def _seq_body(first_layer, tt, refs):
    it = iter(refs)
    z1_ref, z2_ref, sh1_ref, sh2_ref, sc0_ref, cf0_ref = (next(it) for _ in range(6))
    vf_ref = None if first_layer else next(it)
    (mu1_ref, mu2_ref, w0_ref, w2_ref, a0_ref, a2_ref, v0_ref, v2_ref, g2_ref,
     scw_ref, cfbin_ref, cfw_ref, cfdb_ref, cflg_ref, cflb_ref) = (next(it) for _ in range(15))
    (r_ref, w_ref, k_ref, v_ref, a_ref, g_ref, xa_ref, xc_ref,
     nsh1_ref, nsh2_ref, nsc_ref, ncf_ref) = (next(it) for _ in range(12))
    s1buf, s2buf, cbuf, gbuf, pbuf = (next(it) for _ in range(5))

    @pl.when(pl.program_id(1) == 0)
    def _():
        s1buf[7:8, :] = sh1_ref[...]
        s2buf[7:8, :] = sh2_ref[...]
        cbuf[6:8, :] = sc0_ref[...]
        gbuf[2:CF_WIDTH + 1, :] = cf0_ref[...]
        gbuf[0:2, :] = jnp.zeros((2, D_CF), jnp.float32)
        gbuf[CF_WIDTH + 1 + tt:CF_WIDTH + 1 + tt + 8, :] = jnp.zeros((8, D_CF), jnp.float32)

    z1 = z1_ref[...]
    if tt % 8 == 0:
        rowi = lax.broadcasted_iota(jnp.int32, (tt, Z_BLK), 0)
        zp1 = jnp.where(rowi == 0, jnp.broadcast_to(s1buf[7:8, :], (tt, Z_BLK)), pltpu.roll(z1, 1, axis=0))
        s1buf[7 + tt:8 + tt, :] = z1[tt - 1:tt, :]
    else:
        s1buf[8:8 + tt, :] = z1
        zp1 = s1buf[7:7 + tt, :]
    zm1 = z1 + (zp1 - z1) * mu1_ref[...]
    z2l = z2_ref[:, Z2_LORA:Z_BLK]
    if tt % 8 == 0:
        rowl = lax.broadcasted_iota(jnp.int32, (tt, LORA_PAD), 0)
        zp2 = jnp.where(rowl == 0, jnp.broadcast_to(s2buf[7:8, :], (tt, LORA_PAD)),
                        pltpu.roll(z2l, 1, axis=0))
        s2buf[7 + tt:8 + tt, :] = z2l[tt - 1:tt, :]
    else:
        s2buf[8:8 + tt, :] = z2l
        zp2 = s2buf[7:7 + tt, :]
    zm2 = z2l + (zp2 - z2l) * mu2_ref[...]
    last1 = s1buf[7 + tt:8 + tt, :]
    last2 = s2buf[7 + tt:8 + tt, :]
    nsh1_ref[...] = last1
    nsh2_ref[...] = last2
    s1buf[7:8, :] = last1
    s2buf[7:8, :] = last2

    x01 = zm2[:, 0:128]
    lw = _bdot(jnp.tanh(x01), w2_ref[...])
    la = _bdot(x01, a2_ref[...])
    w_ref[...] = -EXP_M05 * _sigmoid(w0_ref[...] + lw)
    a_ref[...] = _sigmoid(a0_ref[...] + la)
    g_ref[...] = _bdot(_sigmoid(zm2[:, 128:384]), g2_ref[...])
    r_ref[...] = zm1[:, 0:D_MODEL]
    k_ref[...] = zm1[:, D_MODEL:2 * D_MODEL]
    v = zm1[:, 2 * D_MODEL:3 * D_MODEL]
    if not first_layer:
        lv = _bdot(zm2[:, 128:256], v2_ref[...])
        v = v + (vf_ref[...] - v) * _sigmoid(v0_ref[...] + lv)
    v_ref[...] = v

    cu = z2_ref[:, Z2_SC + D_SC:Z2_SC + 2 * D_SC] * z2_ref[:, Z2_SC + 2 * D_SC:Z2_SC + 3 * D_SC]
    cbuf[8:8 + tt, :] = cu
    conv_a = (scw_ref[0:1, :] * cbuf[6:6 + tt, :] + scw_ref[1:2, :] * cbuf[7:7 + tt, :]
              + scw_ref[2:3, :] * cu)
    xa_ref[...] = z2_ref[:, Z2_SC:Z2_SC + D_SC] * conv_a
    tail_a = cbuf[6 + tt:8 + tt, :]
    nsc_ref[...] = tail_a
    cbuf[6:8, :] = tail_a

    zc = z2_ref[:, Z2_CF:Z2_CF + 2 * D_CF] + cfbin_ref[...]
    glu = zc[:, 0:D_CF] * _sigmoid(zc[:, D_CF:2 * D_CF])
    gbuf[CF_WIDTH + 1:CF_WIDTH + 1 + tt, :] = glu
    if tt % 8 == 0:
        acc = None
        for ph in range(8):
            part = None
            for j in range(ph, CF_WIDTH + 2, 8):
                if j < 2:
                    continue
                term = cfw_ref[j - 2:j - 1, :] * gbuf[j - ph:j - ph + tt + 8, :]
                part = term if part is None else part + term
            pbuf[...] = part
            sh = pbuf[ph:ph + tt, :]
            acc = sh if acc is None else acc + sh
    else:
        acc = cfw_ref[0:1, :] * gbuf[2:2 + tt, :]
        for kk in range(1, CF_WIDTH):
            acc = acc + cfw_ref[kk:kk + 1, :] * gbuf[2 + kk:2 + kk + tt, :]
    cc = acc + cfdb_ref[...]
    mu = jnp.mean(cc, axis=-1, keepdims=True)
    cd = cc - mu
    var = jnp.mean(cd * cd, axis=-1, keepdims=True)
    ln = cd * lax.rsqrt(var + LN_EPS) * cflg_ref[...] + cflb_ref[...]
    xc_ref[...] = ln * _sigmoid(ln)
    tail_c = gbuf[2 + tt:CF_WIDTH + 1 + tt, :]
    ncf_ref[...] = tail_c
    gbuf[2:CF_WIDTH + 1, :] = tail_c


def _seq(z, sh1, sh2, sc0, cf0, vfirst, P, li, tt, nbs):
    bsz, tlen, _ = z.shape
    first_layer = vfirst is None
    nt = tlen // tt
    assert nbs == 1 or nt == 1, "a batch block of several sequences must cover their whole length"
    tok = lambda width, blk: pl.BlockSpec((nbs, tt, width), lambda b, t, blk=blk: (b, t, blk))
    otok = lambda width: pl.BlockSpec((nbs * tt, width), lambda b, t: (b * nt + t, 0))
    st = lambda rows, width: pl.BlockSpec((nbs, rows, width), lambda b, t: (b, 0, 0))
    lst = lambda rows, width: pl.BlockSpec((None, nbs, rows, width), lambda b, t: (li, b, 0, 0))
    par = lambda a: _lpar(a, li)
    params = [P[n] for n in ('mu1', 'mu2', 'w0', 'w2p', 'a0', 'a2p', 'v0', 'v2p', 'g2p',
                              'scw', 'cfbin', 'cfw', 'cfdb', 'cflg', 'cflb')]
    args = [z, z, sh1, sh2, sc0, cf0]
    in_specs = [tok(Z_BLK, 1), tok(Z_BLK, 2), lst(1, Z_BLK), lst(1, LORA_PAD),
                lst(SC_WIDTH - 1, D_SC), lst(CF_WIDTH - 1, D_CF)]
    if not first_layer:
        args.append(vfirst)
        in_specs.append(tok(D_MODEL, 0))
    args += params
    in_specs += [par(a) for a in params]
    f32 = jnp.float32
    n = bsz * tlen
    out_shape = ([jax.ShapeDtypeStruct((n, D_MODEL), f32)] * 6
                 + [jax.ShapeDtypeStruct((n, D_SC), f32), jax.ShapeDtypeStruct((n, D_CF), f32),
                    jax.ShapeDtypeStruct((bsz, 1, Z_BLK), f32), jax.ShapeDtypeStruct((bsz, 1, LORA_PAD), f32),
                    jax.ShapeDtypeStruct((bsz, SC_WIDTH - 1, D_SC), f32),
                    jax.ShapeDtypeStruct((bsz, CF_WIDTH - 1, D_CF), f32)])
    out_specs = ([otok(D_MODEL)] * 6 + [otok(D_SC), otok(D_CF), st(1, Z_BLK), st(1, LORA_PAD),
                                           st(SC_WIDTH - 1, D_SC), st(CF_WIDTH - 1, D_CF)])
    scratch = [pltpu.VMEM((nbs, 8 + tt, Z_BLK), f32), pltpu.VMEM((nbs, 8 + tt, LORA_PAD), f32),
               pltpu.VMEM((nbs, 8 + tt, D_SC), f32), pltpu.VMEM((nbs, CF_WIDTH + 1 + tt + 8, D_CF), f32),
               pltpu.VMEM((nbs, tt + 8, D_CF), f32)]
    return pl.pallas_call(
        functools.partial(_seq_kernel, first_layer, tt, nbs),
        grid=(bsz // nbs, tlen // tt),
        in_specs=in_specs, out_specs=out_specs, out_shape=out_shape, scratch_shapes=scratch,
        compiler_params=_cparams(("arbitrary", "arbitrary")),
        name="seq",
    )(*args)


CHUNK = 8
PAIR = 2 * HEAD_DIM
N_PAIR = HEADS // 2
SEG = 256
P_ROWS = 4 * CHUNK * CHUNK - 2 * CHUNK
R_OFF = [32 * s + 16 if s < CHUNK - 1 else 32 * s for s in range(CHUNK)]


def _segsum(x, ones):
    xb = x.astype(jnp.bfloat16)
    return jnp.concatenate([jnp.dot(xb[:, q * SEG:(q + 1) * SEG], ones, preferred_element_type=jnp.float32)
                            for q in range(D_MODEL // SEG)], axis=1)


def _wkv_kernel(nb, tt, r_ref, ld_ref, k_ref, v_ref, a_ref, s0_ref, kkp_ref, kap_ref, rkp_ref, lng_ref, lnb_ref,
                ones_ref, y_ref, sout_ref, s_scr, kk_scr, km_scr, o_scr, p_scr, e_scr):
    f32, bf16 = jnp.float32, jnp.bfloat16
    ones = ones_ref[...]

    @pl.when(pl.program_id(1) == 0)
    def _():
        for bb in range(nb):
            for p in range(N_PAIR):
                s_scr[bb, p] = jnp.concatenate([s0_ref[bb, 2 * p], s0_ref[bb, 2 * p + 1]], axis=1)

    flat = lambda ref: ref[...].reshape(nb * tt, D_MODEL)
    kraw = flat(k_ref)
    kkp = kraw * kkp_ref[...]
    kk_scr[...] = (kkp * lax.rsqrt(jnp.maximum(_segsum(kkp * kkp, ones), 1e-24))).reshape(nb, tt, D_MODEL)
    km_scr[...] = (kraw * (1.0 + (flat(a_ref) - 1.0) * kap_ref[...])).reshape(nb, tt, D_MODEL)

    row = lax.broadcasted_iota(jnp.int32, (CHUNK, D_MODEL), 0)
    head0 = (lax.broadcasted_iota(jnp.int32, (2 * CHUNK, D_MODEL), 1) // HEAD_DIM) % 2 == 0
    bcast = lambda x, s: jnp.broadcast_to(x[s:s + 1, :], (CHUNK, D_MODEL))

    def chunk(c, carry):
        rows = pl.ds(pl.multiple_of(c * CHUNK, CHUNK), CHUNK)
        st = []
        for bb in range(nb):
            r = r_ref[bb, rows, :]
            ld = ld_ref[bb, rows, :]
            v = v_ref[bb, rows, :]
            kk = kk_scr[bb, rows, :]
            cum = ld
            for sh in (1, 2, 4):
                cum = cum + jnp.where(row >= sh, pltpu.roll(cum, sh, axis=0), 0.0)
            w_cum = jnp.exp(cum)
            w_inv = jnp.exp(-cum)
            at = -kk * jnp.exp(cum - ld)
            rt = r * w_cum
            bt = kk * a_ref[bb, rows, :] * w_inv
            kt = km_scr[bb, rows, :] * w_inv
            wc = bcast(w_cum, CHUNK - 1)
            lhs = jnp.concatenate([at, rt], axis=0)
            lhs = jnp.concatenate([jnp.where(head0, lhs, 0.0), jnp.where(head0, 0.0, lhs)], axis=0).astype(bf16)
            st.append([v, at, rt, bt, kt, wc, lhs])
        for bb in range(nb):
            v, at, rt, bt, kt, wc, lhs = st[bb]
            xy = []
            for p in range(N_PAIR):
                sl = slice(p * PAIR, (p + 1) * PAIR)
                xh = lax.dot_general(lhs[:, sl], s_scr[bb, p].astype(bf16), (((1,), (1,)), ((), ())),
                                     preferred_element_type=f32)
                xy.append(xh[0:2 * CHUNK])
                xy.append(xh[2 * CHUNK:4 * CHUNK])
            st[bb][6] = jnp.concatenate(xy, axis=1)
            for s in range(CHUNK):
                bs, ks = bcast(bt, s), bcast(kt, s)
                rm = jnp.where(row >= s, rt, 0.0) if s else rt
                if s < CHUNK - 1:
                    am = jnp.where(row > s, at, 0.0)
                    p_scr[bb, 32 * s:32 * s + 16, :] = jnp.concatenate([bs * am, ks * am], axis=0).astype(bf16)
                p_scr[bb, R_OFF[s]:R_OFF[s] + 16, :] = jnp.concatenate([bs * rm, ks * rm], axis=0).astype(bf16)
            for q in range(D_MODEL // SEG):
                sl = slice(q * SEG, (q + 1) * SEG)
                e_scr[bb, :, sl] = jnp.dot(p_scr[bb, :, sl], ones, preferred_element_type=f32)
        accs = []
        for bb in range(nb):
            v, at, rt, bt, kt, wc, xy = st[bb]
            pre_u = pre_o = None
            for s in range(CHUNK):
                vs = bcast(v, s)
                to = e_scr[bb, R_OFF[s] + 8:R_OFF[s] + 16, :] * vs
                pre_o = to if pre_o is None else pre_o + to
                if s < CHUNK - 1:
                    tu = e_scr[bb, 32 * s + 8:32 * s + 16, :] * vs
                    pre_u = tu if pre_u is None else pre_u + tu
            acc_u = xy[0:CHUNK] + pre_u
            acc_o = xy[CHUNK:2 * CHUNK] + pre_o
            for s in range(CHUNK):
                us = bcast(acc_u, s)
                if s < CHUNK - 1:
                    acc_u = acc_u + e_scr[bb, 32 * s:32 * s + 8, :] * us
                acc_o = acc_o + e_scr[bb, R_OFF[s]:R_OFF[s] + 8, :] * us
            o_scr[bb, rows, :] = acc_o
            accs.append(acc_u)
        for bb in range(nb):
            v, at, rt, bt, kt, wc, xy = st[bb]
            uv = jnp.concatenate([accs[bb], v], axis=0)
            uv_swapped = pltpu.roll(uv, D_MODEL - HEAD_DIM, axis=1)
            uv = jnp.concatenate([jnp.where(head0, uv, 0.0), jnp.where(head0, uv_swapped, 0.0)], axis=0).astype(bf16)
            bk = jnp.concatenate([bt * wc, kt * wc], axis=0)
            bk = jnp.concatenate([jnp.where(head0, bk, 0.0), jnp.where(head0, 0.0, bk)], axis=0).astype(bf16)
            for p in range(N_PAIR):
                sl = slice(p * PAIR, (p + 1) * PAIR)
                upd = lax.dot_general(uv[:, p * PAIR:p * PAIR + HEAD_DIM], bk[:, sl], (((0,), (0,)), ((), ())),
                                      preferred_element_type=f32)
                s_scr[bb, p] = s_scr[bb, p] * wc[0:1, sl] + upd
        return carry

    lax.fori_loop(0, tt // CHUNK, chunk, 0)

    o = flat(o_scr)
    mu = _segsum(o, ones) * (1.0 / HEAD_DIM)
    od = o - mu
    var = _segsum(od * od, ones) * (1.0 / HEAD_DIM)
    y = od * lax.rsqrt(var + GN_EPS) * lng_ref[...] + lnb_ref[...]
    bonus = _segsum(flat(r_ref) * flat(km_scr) * rkp_ref[...], ones)
    y_ref[...] = (y + bonus * flat(v_ref)).reshape(nb, tt, D_MODEL)

    @pl.when(pl.program_id(1) == pl.num_programs(1) - 1)
    def _():
        for bb in range(nb):
            for p in range(N_PAIR):
                sp = s_scr[bb, p]
                sout_ref[bb, 2 * p] = sp[:, 0:HEAD_DIM]
                sout_ref[bb, 2 * p + 1] = sp[:, HEAD_DIM:PAIR]


def _wkv(r, ld, k, v, a, s0, P, li, nb, tt):
    bsz, tlen, _ = r.shape
    f32 = jnp.float32
    tok = pl.BlockSpec((nb, tt, D_MODEL), lambda b, t: (b, t, 0))
    stt = pl.BlockSpec((nb, HEADS, HEAD_DIM, HEAD_DIM), lambda b, t: (b, 0, 0, 0))
    lstt = pl.BlockSpec((None, nb, HEADS, HEAD_DIM, HEAD_DIM), lambda b, t: (li, b, 0, 0, 0))
    par = _lpar(P['kk'], li)
    return pl.pallas_call(
        functools.partial(_wkv_kernel, nb, tt),
        grid=(bsz // nb, tlen // tt),
        in_specs=[tok] * 5 + [lstt] + [par] * 5 + [pl.BlockSpec((SEG, SEG), lambda b, t: (0, 0))],
        out_specs=[tok, stt],
        out_shape=[jax.ShapeDtypeStruct((bsz, tlen, D_MODEL), f32),
                   jax.ShapeDtypeStruct((bsz, HEADS, HEAD_DIM, HEAD_DIM), f32)],
        scratch_shapes=[pltpu.VMEM((nb, N_PAIR, HEAD_DIM, PAIR), f32),
                        pltpu.VMEM((nb, tt, D_MODEL), f32), pltpu.VMEM((nb, tt, D_MODEL), f32),
                        pltpu.VMEM((nb, tt, D_MODEL), f32),
                        pltpu.VMEM((nb, P_ROWS, D_MODEL), jnp.bfloat16),
                        pltpu.VMEM((nb, P_ROWS, D_MODEL), f32)],
        compiler_params=_cparams(("arbitrary", "arbitrary")),
        name="wkv",
    )(r, ld, k, v, a, s0, P['kk'], P['ka'], P['rk'], P['lng'], P['lnb'], P['ones'])


LANES = 128


def _wkv_lanes_kernel(tlen, r_ref, ld_ref, k_ref, v_ref, a_ref, s0_ref, kkp_ref, kap_ref, rkp_ref, lng_ref, lnb_ref,
                      y_ref, s_ref, o_scr):
    s_ref[...] = s0_ref[...]
    for t in range(tlen):
        rt = r_ref[t]
        wt = jnp.exp(ld_ref[t])
        kraw = k_ref[t]
        at = a_ref[t]
        kkp = kraw * kkp_ref[...]
        kk = kkp * lax.rsqrt(jnp.maximum(jnp.sum(kkp * kkp, axis=0, keepdims=True), 1e-24))
        kt = kraw * (1.0 + (at - 1.0) * kap_ref[...])
        bt = kk * at
        nkk = -kk
        for i in range(HEAD_DIM):
            si = s_ref[i]
            sa = jnp.sum(si * nkk, axis=0, keepdims=True)
            sn = si * wt + sa * bt + v_ref[t, i:i + 1, :] * kt
            s_ref[i] = sn
            o_scr[i:i + 1, :] = jnp.sum(sn * rt, axis=0, keepdims=True)
        o = o_scr[...]
        mu = jnp.mean(o, axis=0, keepdims=True)
        od = o - mu
        var = jnp.mean(od * od, axis=0, keepdims=True)
        y = od * lax.rsqrt(var + GN_EPS) * lng_ref[...] + lnb_ref[...]
        bonus = jnp.sum(rt * kt * rkp_ref[...], axis=0, keepdims=True)
        y_ref[t] = y + bonus * v_ref[t]


def _wkv_lanes(r, ld, k, v, a, s0, P, li):
    tlen, _, _, bsz = r.shape
    f32 = jnp.float32
    tok = pl.BlockSpec((tlen, None, HEAD_DIM, bsz), lambda h: (0, h, 0, 0))
    lstt = pl.BlockSpec((None, None, HEAD_DIM, HEAD_DIM, bsz), lambda h: (li, h, 0, 0, 0))
    stt = pl.BlockSpec((None, HEAD_DIM, HEAD_DIM, bsz), lambda h: (h, 0, 0, 0))
    par = pl.BlockSpec((None, None, HEAD_DIM, bsz), lambda h: (li, h, 0, 0))
    hp = P['head_lanes']
    return pl.pallas_call(
        functools.partial(_wkv_lanes_kernel, tlen),
        grid=(HEADS,),
        in_specs=[tok] * 5 + [lstt] + [par] * 5,
        out_specs=[tok, stt],
        out_shape=[jax.ShapeDtypeStruct((tlen, HEADS, HEAD_DIM, bsz), f32),
                   jax.ShapeDtypeStruct((HEADS, HEAD_DIM, HEAD_DIM, bsz), f32)],
        scratch_shapes=[pltpu.VMEM((HEAD_DIM, bsz), f32)],
        compiler_params=_cparams(("arbitrary",)),
        name="wkv_lanes",
    )(r, ld, k, v, a, s0, hp['kk'], hp['ka'], hp['rk'], hp['lng'], hp['lnb'])


def _merge_kernel(h_ref, xa_ref, yw_ref, g_ref, xc_ref, zg_ref, wa_ref, wb_ref, wc_ref, bc_ref, wo_ref, o_ref):
    ya = _bdot(xa_ref[...], wa_ref[...])
    yb = _bdot(yw_ref[...] * g_ref[...], wb_ref[...])
    yc = _bdot(xc_ref[...], wc_ref[...]) + bc_ref[...]
    m = (_sigmoid(zg_ref[:, 0:D_MODEL]) * ya + _sigmoid(zg_ref[:, D_MODEL:2 * D_MODEL]) * yb
         + _sigmoid(zg_ref[:, 2 * D_MODEL:3 * D_MODEL]) * yc)
    o_ref[...] = h_ref[...] + _bdot(m, wo_ref[...])


def _merge(h, xa, yw, g, xc, z, P, li, tm):
    n = h.shape[0]
    tok = lambda width: pl.BlockSpec((tm, width), lambda i: (i, 0))
    par = lambda a: _lpar(a, li)
    ws = [P[n_] for n_ in ('sc_w_out', 'rk_w_out', 'cf_w_out', 'cf_b_out', 'w_o')]
    return pl.pallas_call(
        _merge_kernel,
        grid=(n // tm,),
        in_specs=[tok(D_MODEL), tok(D_SC), tok(D_MODEL), tok(D_MODEL), tok(D_CF), tok(Z_BLK)] + [par(a) for a in ws],
        out_specs=tok(D_MODEL),
        out_shape=jax.ShapeDtypeStruct((n, D_MODEL), jnp.float32),
        compiler_params=_cparams(("arbitrary",)),
        name="merge",
    )(h, xa, yw, g, xc, z, *ws)


def _mlp_ple_kernel(nk, final, h_ref, p_ref, g2_ref, w1_ref, w2_ref, gp_ref, wg_ref, wp_ref, *rest):
    gf_ref = rest[0] if final else None
    o_ref, hn_scr, acc_scr = rest[-3:]
    kstep = pl.program_id(1)

    @pl.when(kstep == 0)
    def _():
        hn_scr[...] = _rms(h_ref[...], g2_ref[...]).astype(jnp.bfloat16)
        acc_scr[...] = jnp.zeros_like(acc_scr)

    u = jnp.dot(hn_scr[...], w1_ref[...], preferred_element_type=jnp.float32)
    u = jnp.maximum(u, 0.0)
    acc_scr[...] += _bdot(u * u, w2_ref[...])

    @pl.when(kstep == nk - 1)
    def _():
        h1 = h_ref[...] + acc_scr[...]
        gate = _sigmoid(_bdot(_rms(h1, gp_ref[...]), wg_ref[...]))
        h2 = h1 + gate * _bdot(p_ref[...], wp_ref[...])
        o_ref[...] = h2 if gf_ref is None else _rms(h2, gf_ref[...])


def _mlp_ple(h, p, P, li, tm, tf, final_g=None):
    n = h.shape[0]
    nk = D_FF // tf
    par = lambda a: _lpar(a, li)
    extra = [] if final_g is None else [final_g]
    return pl.pallas_call(
        functools.partial(_mlp_ple_kernel, nk, final_g is not None),
        grid=(n // tm, nk),
        in_specs=[pl.BlockSpec((tm, D_MODEL), lambda i, k: (i, 0)),
                  pl.BlockSpec((None, tm, D_PLE), lambda i, k: (li, i, 0)),
                  par(P['norm2_g']),
                  pl.BlockSpec((None, D_MODEL, tf), lambda i, k: (li, 0, k)),
                  pl.BlockSpec((None, tf, D_MODEL), lambda i, k: (li, k, 0)),
                  par(P['ple_norm_g']), par(P['ple_gate_w']), par(P['ple_w'])]
                 + [pl.BlockSpec((1, D_MODEL), lambda i, k: (0, 0)) for _ in extra],
        out_specs=pl.BlockSpec((tm, D_MODEL), lambda i, k: (i, 0)),
        out_shape=jax.ShapeDtypeStruct((n, D_MODEL), jnp.float32),
        scratch_shapes=[pltpu.VMEM((tm, D_MODEL), jnp.bfloat16), pltpu.VMEM((tm, D_MODEL), jnp.float32)],
        compiler_params=_cparams(("arbitrary", "arbitrary")),
        name="mlp_ple",
    )(h, p, P['norm2_g'], P['mlp_w1'], P['mlp_w2'], P['ple_norm_g'], P['ple_gate_w'], P['ple_w'], *extra)


def _prep(W):
    bf = jnp.bfloat16
    nl = W['w_in'].shape[0]
    row = lambda a: a.reshape(nl, 1, -1)
    w_in = W['w_in']
    n_sc, n_rk, n_cf = 3 * D_SC, RK_PROJ, 2 * D_CF
    o_rk, o_cf, o_gate = n_sc, n_sc + n_rk, n_sc + n_rk + n_cf
    w_perm = jnp.concatenate([
        w_in[:, :, o_gate:o_gate + 3 * D_MODEL], w_in[:, :, o_rk:o_rk + 3 * D_MODEL], w_in[:, :, 0:n_sc],
        w_in[:, :, o_cf:o_cf + n_cf], w_in[:, :, o_rk + 3 * D_MODEL:o_rk + RK_PROJ],
        jnp.zeros((nl, D_MODEL, LORA_PAD - N_LORA), w_in.dtype)], axis=2).astype(bf)
    mu = W['rk_mu']
    zrows = lambda n: jnp.zeros((nl, n, D_MODEL), jnp.float32)
    o_a, o_g = DECAY_LORA, DECAY_LORA + AAA_LORA + MV_LORA
    seg_head = jnp.arange(SEG) // HEAD_DIM
    on_lanes = lambda a: jnp.broadcast_to(a.reshape(nl, HEADS, HEAD_DIM, 1), (nl, HEADS, HEAD_DIM, LANES))
    return {
        'head_lanes': {'kk': on_lanes(W['rk_k_k']), 'ka': on_lanes(W['rk_k_a']), 'rk': on_lanes(W['rk_r_k']),
                       'lng': on_lanes(W['rk_ln_g']), 'lnb': on_lanes(W['rk_ln_b'])},
        'ones': (seg_head[:, None] == seg_head[None, :]).astype(bf),
        'kk': row(W['rk_k_k']), 'ka': row(W['rk_k_a']), 'rk': row(W['rk_r_k']),
        'lng': row(W['rk_ln_g']), 'lnb': row(W['rk_ln_b']),
        'norm1_g': row(W['norm1_g']), 'w_in': w_perm,
        'mu1': row(mu[:, :3 * D_MODEL]),
        'mu2': row(jnp.pad(mu[:, 3 * D_MODEL:], ((0, 0), (0, LORA_PAD - N_LORA)))),
        'w0': row(W['rk_w0']), 'a0': row(W['rk_a0']), 'v0': row(W['rk_v0']),
        'w2p': jnp.concatenate([W['rk_w2'], zrows(128 - o_a)], axis=1).astype(bf),
        'a2p': jnp.concatenate([zrows(o_a), W['rk_a2']], axis=1).astype(bf),
        'v2p': jnp.concatenate([W['rk_v2'], zrows(128 - MV_LORA)], axis=1).astype(bf),
        'g2p': jnp.concatenate([zrows(o_g - 128), W['rk_g2'], zrows(384 - N_LORA)], axis=1).astype(bf),
        'scw': W['sc_conv_w'], 'cfbin': row(W['cf_b_in']), 'cfw': W['cf_dw_w'],
        'cfdb': row(W['cf_dw_b']), 'cflg': row(W['cf_ln_g']), 'cflb': row(W['cf_ln_b']),
        'sc_w_out': W['sc_w_out'].astype(bf), 'rk_w_out': W['rk_w_out'].astype(bf),
        'cf_w_out': W['cf_w_out'].astype(bf), 'cf_b_out': row(W['cf_b_out']), 'w_o': W['w_o'].astype(bf),
        'norm2_g': row(W['norm2_g']), 'mlp_w1': W['mlp_w1'].astype(bf), 'mlp_w2': W['mlp_w2'].astype(bf),
        'ple_norm_g': row(W['ple_norm_g']), 'ple_gate_w': W['ple_gate_w'].astype(bf),
        'ple_w': W['ple_w'].astype(bf),
    }


def _pick(n, pref):
    return pref if n % pref == 0 else n


def _trunk(x, p, sc0, shift0, wkv0, cf0, P, final_g, cfg):
    bsz, tlen, _ = x.shape
    n = bsz * tlen
    tm = _pick(n, cfg['tm'])
    tt = _pick(tlen, cfg['tt'])
    tpad = -tlen % CHUNK
    tw = _pick(tlen + tpad, cfg['tw'])
    pad_t = lambda a: jnp.pad(a, ((0, 0), (0, tpad), (0, 0))) if tpad else a
    h = x.reshape(n, D_MODEL)
    pf = p.reshape(DEPTH, n, D_PLE)
    sh1 = shift0[:, :, None, :3 * D_MODEL]
    sh2 = jnp.pad(shift0[:, :, 3 * D_MODEL:], ((0, 0), (0, 0), (0, LORA_PAD - N_LORA)))[:, :, None, :]
    lanes_form = bsz == LANES and tlen <= CHUNK
    to_lanes = lambda a: a.reshape(bsz, tlen, HEADS, HEAD_DIM).transpose(1, 2, 3, 0)
    seqs = lambda a: a.reshape(bsz, tlen, D_MODEL)
    wkv0_l = wkv0.transpose(0, 2, 3, 4, 1) if lanes_form else None
    v_first = None
    new_sc, new_shift, new_wkv, new_cf = [], [], [], []
    for i in range(DEPTH):
        z = _in_proj(h, P, i, tm)
        (r, w, k, v, a, g, xa, xc, nsh1, nsh2, nsc, ncf) = _seq(
            z.reshape(bsz, tlen, NZ), sh1, sh2, sc0, cf0, v_first, P, i, tt, cfg['nbs'])
        if i == 0:
            v_first = seqs(v)
        if lanes_form:
            y, s_out = _wkv_lanes(to_lanes(r), to_lanes(w), to_lanes(k), to_lanes(v), to_lanes(a), wkv0_l, P, i)
            yw = y.transpose(3, 0, 1, 2).reshape(n, D_MODEL)
        else:
            y, s_out = _wkv(pad_t(seqs(r)), pad_t(seqs(w)), pad_t(seqs(k)), pad_t(seqs(v)), pad_t(seqs(a)),
                            wkv0, P, i, cfg['nb'], tw)
            yw = y[:, :tlen].reshape(n, D_MODEL)
        h = _merge(h, xa, yw, g, xc, z, P, i, tm)
        h = _mlp_ple(h, pf, P, i, tm, cfg['tf'], final_g if i == DEPTH - 1 else None)
        new_sc.append(nsc)
        new_shift.append(jnp.concatenate([nsh1[:, 0, :], nsh2[:, 0, :N_LORA]], axis=-1))
        new_wkv.append(s_out)
        new_cf.append(ncf)
    y = h.reshape(bsz, tlen, D_MODEL)
    wkv_out = jnp.stack(new_wkv)
    if lanes_form:
        wkv_out = wkv_out.transpose(0, 4, 1, 2, 3)
    return y, jnp.stack(new_sc), jnp.stack(new_shift), wkv_out, jnp.stack(new_cf)


PROMPT_CFG = {'tm': 512, 'tt': 128, 'nbs': 1, 'tw': 64, 'nb': 8, 'tf': 2048}
SAMPLE_CFG = {'tm': 512, 'tt': 4, 'nbs': 8, 'tw': 8, 'nb': 8, 'tf': 1024}


def kernel(x_prompt, x_sample, p_prompt, p_sample, state_sconv, state_shift, state_wkv, state_cconv,
           norm1_g, w_in, sc_conv_w, sc_w_out, rk_mu, rk_w0, rk_w2, rk_a0, rk_a2, rk_v0, rk_v2, rk_g2,
           rk_k_k, rk_k_a, rk_r_k, rk_ln_g, rk_ln_b, rk_w_out, cf_b_in, cf_dw_w, cf_dw_b, cf_ln_g, cf_ln_b,
           cf_w_out, cf_b_out, w_o, norm2_g, mlp_w1, mlp_w2, ple_w, ple_gate_w, ple_norm_g, final_norm_g):
    W = dict(norm1_g=norm1_g, w_in=w_in, sc_conv_w=sc_conv_w, sc_w_out=sc_w_out, rk_mu=rk_mu, rk_w0=rk_w0,
             rk_w2=rk_w2, rk_a0=rk_a0, rk_a2=rk_a2, rk_v0=rk_v0, rk_v2=rk_v2, rk_g2=rk_g2, rk_k_k=rk_k_k,
             rk_k_a=rk_k_a, rk_r_k=rk_r_k, rk_ln_g=rk_ln_g, rk_ln_b=rk_ln_b, rk_w_out=rk_w_out, cf_b_in=cf_b_in,
             cf_dw_w=cf_dw_w, cf_dw_b=cf_dw_b, cf_ln_g=cf_ln_g, cf_ln_b=cf_ln_b, cf_w_out=cf_w_out,
             cf_b_out=cf_b_out, w_o=w_o, norm2_g=norm2_g, mlp_w1=mlp_w1, mlp_w2=mlp_w2, ple_w=ple_w,
             ple_gate_w=ple_gate_w, ple_norm_g=ple_norm_g)
    P = _prep(W)
    final_g = final_norm_g.reshape(1, D_MODEL)
    bp = x_prompt.shape[0]
    f32 = jnp.float32
    zsc = jnp.zeros((DEPTH, bp, SC_WIDTH - 1, D_SC), f32)
    zsh = jnp.zeros((DEPTH, bp, RK_PROJ), f32)
    zwkv = jnp.zeros((DEPTH, bp, HEADS, HEAD_DIM, HEAD_DIM), f32)
    zcf = jnp.zeros((DEPTH, bp, CF_WIDTH - 1, D_CF), f32)
    y_p, sc_p, sh_p, wkv_p, cf_p = _trunk(x_prompt, p_prompt, zsc, zsh, zwkv, zcf, P, final_g, PROMPT_CFG)
    y_s, sc_s, sh_s, wkv_s, cf_s = _trunk(x_sample, p_sample, state_sconv, state_shift, state_wkv, state_cconv,
                                          P, final_g, SAMPLE_CFG)
    return (y_p, y_s, sc_p, sh_p, wkv_p, cf_p, sc_s, sh_s, wkv_s, cf_s)
```

```python
import functools

import jax
import jax.numpy as jnp
from jax import lax
from jax.experimental import pallas as pl
from jax.experimental.pallas import tpu as pltpu

D_MODEL = 1024
DEPTH = 4
D_SC = 512
D_CF = 512
SC_WIDTH = 3
CF_WIDTH = 31
HEAD_DIM = 64
HEADS = 16
D_FF = 4096
D_PLE = 256
DECAY_LORA, AAA_LORA, MV_LORA, GATE_LORA = 64, 64, 32, 128
N_LORA = DECAY_LORA + AAA_LORA + MV_LORA + GATE_LORA
RK_PROJ = 3 * D_MODEL + N_LORA
RMS_EPS = 1e-6
LN_EPS = 1e-5
GN_EPS = 64e-5
EXP_M05 = 0.6065306597126334

Z_BLK = 3 * D_MODEL
NZ = 3 * Z_BLK
Z2_SC = 0
Z2_CF = 3 * D_SC
Z2_LORA = Z2_CF + 2 * D_CF
LORA_PAD = Z_BLK - Z2_LORA

VMEM_LIMIT_BYTES = 56 * 1024 * 1024


def _cparams(sem):
    return pltpu.CompilerParams(dimension_semantics=sem, vmem_limit_bytes=VMEM_LIMIT_BYTES)


def _sigmoid(x):
    return 1.0 / (1.0 + jnp.exp(-x))


def _rms(x, g):
    return x * lax.rsqrt(jnp.mean(x * x, axis=-1, keepdims=True) + RMS_EPS) * g


def _bdot(a, b):
    return jnp.dot(a.astype(jnp.bfloat16), b, preferred_element_type=jnp.float32)


def _in_proj_kernel(x_ref, g_ref, w_ref, o_ref):
    o_ref[...] = _bdot(_rms(x_ref[...], g_ref[...]), w_ref[...])


def _lpar(a, li):
    return pl.BlockSpec((None,) + a.shape[1:], lambda *_: (li,) + (0,) * (a.ndim - 1))


def _in_proj(x, P, li, tm):
    n = x.shape[0]
    g, w = P['norm1_g'], P['w_in']
    return pl.pallas_call(
        _in_proj_kernel,
        grid=(NZ // Z_BLK, n // tm),
        in_specs=[
            pl.BlockSpec((tm, D_MODEL), lambda j, i: (i, 0)),
            _lpar(g, li),
            pl.BlockSpec((None, D_MODEL, Z_BLK), lambda j, i: (li, 0, j)),
        ],
        out_specs=pl.BlockSpec((tm, Z_BLK), lambda j, i: (i, j)),
        out_shape=jax.ShapeDtypeStruct((n, NZ), jnp.float32),
        compiler_params=_cparams(("arbitrary", "arbitrary")),
        name="in_proj",
    )(x, g, w)


def _seq_batched(first_layer):
    n_in = 6 if first_layer else 7
    n_par = 15
    return set(range(n_in)) | set(range(n_in + n_par, n_in + n_par + 12 + 5))


SEQ_BATCHED = {True: _seq_batched(True), False: _seq_batched(False)}


def _seq_kernel(first_layer, tt, nbs, *refs):
    for bb in range(nbs):
        _seq_body(first_layer, tt, [r.at[bb] if i in SEQ_BATCHED[first_layer] else r for i, r in enumerate(refs)])


def _seq_body(first_layer, tt, refs):
    it = iter(refs)
    z1_ref, z2_ref, sh1_ref, sh2_ref, sc0_ref, cf0_ref = (next(it) for _ in range(6))
    vf_ref = None if first_layer else next(it)
    (mu1_ref, mu2_ref, w0_ref, w2_ref, a0_ref, a2_ref, v0_ref, v2_ref, g2_ref,
     scw_ref, cfbin_ref, cfw_ref, cfdb_ref, cflg_ref, cflb_ref) = (next(it) for _ in range(15))
    (r_ref, w_ref, k_ref, v_ref, a_ref, g_ref, xa_ref, xc_ref,
     nsh1_ref, nsh2_ref, nsc_ref, ncf_ref) = (next(it) for _ in range(12))
    s1buf, s2buf, cbuf, gbuf, pbuf = (next(it) for _ in range(5))

    @pl.when(pl.program_id(1) == 0)
    def _():
        s1buf[7:8, :] = sh1_ref[...]
        s2buf[7:8, :] = sh2_ref[...]
        cbuf[6:8, :] = sc0_ref[...]
        gbuf[2:CF_WIDTH + 1, :] = cf0_ref[...]
        gbuf[0:2, :] = jnp.zeros((2, D_CF), jnp.float32)
        gbuf[CF_WIDTH + 1 + tt:CF_WIDTH + 1 + tt + 8, :] = jnp.zeros((8, D_CF), jnp.float32)

    z1 = z1_ref[...]
    if tt % 8 == 0:
        rowi = lax.broadcasted_iota(jnp.int32, (tt, Z_BLK), 0)
        zp1 = jnp.where(rowi == 0, jnp.broadcast_to(s1buf[7:8, :], (tt, Z_BLK)), pltpu.roll(z1, 1, axis=0))
        s1buf[7 + tt:8 + tt, :] = z1[tt - 1:tt, :]
    else:
        s1buf[8:8 + tt, :] = z1
        zp1 = s1buf[7:7 + tt, :]
    zm1 = z1 + (zp1 - z1) * mu1_ref[...]
    z2l = z2_ref[:, Z2_LORA:Z_BLK]
    if tt % 8 == 0:
        rowl = lax.broadcasted_iota(jnp.int32, (tt, LORA_PAD), 0)
        zp2 = jnp.where(rowl == 0, jnp.broadcast_to(s2buf[7:8, :], (tt, LORA_PAD)),
                        pltpu.roll(z2l, 1, axis=0))
        s2buf[7 + tt:8 + tt, :] = z2l[tt - 1:tt, :]
    else:
        s2buf[8:8 + tt, :] = z2l
        zp2 = s2buf[7:7 + tt, :]
    zm2 = z2l + (zp2 - z2l) * mu2_ref[...]
    last1 = s1buf[7 + tt:8 + tt, :]
    last2 = s2buf[7 + tt:8 + tt, :]
    nsh1_ref[...] = last1
    nsh2_ref[...] = last2
    s1buf[7:8, :] = last1
    s2buf[7:8, :] = last2

    x01 = zm2[:, 0:128]
    lw = _bdot(jnp.tanh(x01), w2_ref[...])
    la = _bdot(x01, a2_ref[...])
    w_ref[...] = -EXP_M05 * _sigmoid(w0_ref[...] + lw)
    a_ref[...] = _sigmoid(a0_ref[...] + la)
    g_ref[...] = _bdot(_sigmoid(zm2[:, 128:384]), g2_ref[...])
    r_ref[...] = zm1[:, 0:D_MODEL]
    k_ref[...] = zm1[:, D_MODEL:2 * D_MODEL]
    v = zm1[:, 2 * D_MODEL:3 * D_MODEL]
    if not first_layer:
        lv = _bdot(zm2[:, 128:256], v2_ref[...])
        v = v + (vf_ref[...] - v) * _sigmoid(v0_ref[...] + lv)
    v_ref[...] = v

    cu = z2_ref[:, Z2_SC + D_SC:Z2_SC + 2 * D_SC] * z2_ref[:, Z2_SC + 2 * D_SC:Z2_SC + 3 * D_SC]
    cbuf[8:8 + tt, :] = cu
    conv_a = (scw_ref[0:1, :] * cbuf[6:6 + tt, :] + scw_ref[1:2, :] * cbuf[7:7 + tt, :]
              + scw_ref[2:3, :] * cu)
    xa_ref[...] = z2_ref[:, Z2_SC:Z2_SC + D_SC] * conv_a
    tail_a = cbuf[6 + tt:8 + tt, :]
    nsc_ref[...] = tail_a
    cbuf[6:8, :] = tail_a

    zc = z2_ref[:, Z2_CF:Z2_CF + 2 * D_CF] + cfbin_ref[...]
    glu = zc[:, 0:D_CF] * _sigmoid(zc[:, D_CF:2 * D_CF])
    gbuf[CF_WIDTH + 1:CF_WIDTH + 1 + tt, :] = glu
    if tt % 8 == 0:
        acc = None
        for ph in range(8):
            part = None
            for j in range(ph, CF_WIDTH + 2, 8):
                if j < 2:
                    continue
                term = cfw_ref[j - 2:j - 1, :] * gbuf[j - ph:j - ph + tt + 8, :]
                part = term if part is None else part + term
            pbuf[...] = part
            sh = pbuf[ph:ph + tt, :]
            acc = sh if acc is None else acc + sh
    else:
        acc = cfw_ref[0:1, :] * gbuf[2:2 + tt, :]
        for kk in range(1, CF_WIDTH):
            acc = acc + cfw_ref[kk:kk + 1, :] * gbuf[2 + kk:2 + kk + tt, :]
    cc = acc + cfdb_ref[...]
    mu = jnp.mean(cc, axis=-1, keepdims=True)
    cd = cc - mu
    var = jnp.mean(cd * cd, axis=-1, keepdims=True)
    ln = cd * lax.rsqrt(var + LN_EPS) * cflg_ref[...] + cflb_ref[...]
    xc_ref[...] = ln * _sigmoid(ln)
    tail_c = gbuf[2 + tt:CF_WIDTH + 1 + tt, :]
    ncf_ref[...] = tail_c
    gbuf[2:CF_WIDTH + 1, :] = tail_c


def _seq(z, sh1, sh2, sc0, cf0, vfirst, P, li, tt, nbs):
    bsz, tlen, _ = z.shape
    first_layer = vfirst is None
    tok = lambda width, blk: pl.BlockSpec((nbs, tt, width), lambda b, t, blk=blk: (b, t, blk))
    st = lambda rows, width: pl.BlockSpec((nbs, rows, width), lambda b, t: (b, 0, 0))
    lst = lambda rows, width: pl.BlockSpec((None, nbs, rows, width), lambda b, t: (li, b, 0, 0))
    par = lambda a: _lpar(a, li)
    params = [P[n] for n in ('mu1', 'mu2', 'w0', 'w2p', 'a0', 'a2p', 'v0', 'v2p', 'g2p',
                              'scw', 'cfbin', 'cfw', 'cfdb', 'cflg', 'cflb')]
    args = [z, z, sh1, sh2, sc0, cf0]
    in_specs = [tok(Z_BLK, 1), tok(Z_BLK, 2), lst(1, Z_BLK), lst(1, LORA_PAD),
                lst(SC_WIDTH - 1, D_SC), lst(CF_WIDTH - 1, D_CF)]
    if not first_layer:
        args.append(vfirst)
        in_specs.append(tok(D_MODEL, 0))
    args += params
    in_specs += [par(a) for a in params]
    f32 = jnp.float32
    out_shape = ([jax.ShapeDtypeStruct((bsz, tlen, D_MODEL), f32)] * 6
                 + [jax.ShapeDtypeStruct((bsz, tlen, D_SC), f32), jax.ShapeDtypeStruct((bsz, tlen, D_CF), f32),
                    jax.ShapeDtypeStruct((bsz, 1, Z_BLK), f32), jax.ShapeDtypeStruct((bsz, 1, LORA_PAD), f32),
                    jax.ShapeDtypeStruct((bsz, SC_WIDTH - 1, D_SC), f32),
                    jax.ShapeDtypeStruct((bsz, CF_WIDTH - 1, D_CF), f32)])
    out_specs = ([tok(D_MODEL, 0)] * 6 + [tok(D_SC, 0), tok(D_CF, 0), st(1, Z_BLK), st(1, LORA_PAD),
                                           st(SC_WIDTH - 1, D_SC), st(CF_WIDTH - 1, D_CF)])
    scratch = [pltpu.VMEM((nbs, 8 + tt, Z_BLK), f32), pltpu.VMEM((nbs, 8 + tt, LORA_PAD), f32),
               pltpu.VMEM((nbs, 8 + tt, D_SC), f32), pltpu.VMEM((nbs, CF_WIDTH + 1 + tt + 8, D_CF), f32),
               pltpu.VMEM((nbs, tt + 8, D_CF), f32)]
    return pl.pallas_call(
        functools.partial(_seq_kernel, first_layer, tt, nbs),
        grid=(bsz // nbs, tlen // tt),
        in_specs=in_specs, out_specs=out_specs, out_shape=out_shape, scratch_shapes=scratch,
        compiler_params=_cparams(("arbitrary", "arbitrary")),
        name="seq",
    )(*args)


CHUNK = 8
PAIR = 2 * HEAD_DIM
N_PAIR = HEADS // 2
SEG = 256
P_ROWS = 4 * CHUNK * CHUNK - 2 * CHUNK
R_OFF = [32 * s + 16 if s < CHUNK - 1 else 32 * s for s in range(CHUNK)]


def _segsum(x, ones):
    xb = x.astype(jnp.bfloat16)
    return jnp.concatenate([jnp.dot(xb[:, q * SEG:(q + 1) * SEG], ones, preferred_element_type=jnp.float32)
                            for q in range(D_MODEL // SEG)], axis=1)


def _wkv_kernel(nb, tt, r_ref, ld_ref, k_ref, v_ref, a_ref, s0_ref, kkp_ref, kap_ref, rkp_ref, lng_ref, lnb_ref,
                ones_ref, y_ref, sout_ref, s_scr, kk_scr, km_scr, o_scr, p_scr, e_scr):
    f32, bf16 = jnp.float32, jnp.bfloat16
    ones = ones_ref[...]

    @pl.when(pl.program_id(1) == 0)
    def _():
        for bb in range(nb):
            for p in range(N_PAIR):
                s_scr[bb, p] = jnp.concatenate([s0_ref[bb, 2 * p], s0_ref[bb, 2 * p + 1]], axis=1)

    flat = lambda ref: ref[...].reshape(nb * tt, D_MODEL)
    kraw = flat(k_ref)
    kkp = kraw * kkp_ref[...]
    kk_scr[...] = (kkp * lax.rsqrt(jnp.maximum(_segsum(kkp * kkp, ones), 1e-24))).reshape(nb, tt, D_MODEL)
    km_scr[...] = (kraw * (1.0 + (flat(a_ref) - 1.0) * kap_ref[...])).reshape(nb, tt, D_MODEL)

    row = lax.broadcasted_iota(jnp.int32, (CHUNK, D_MODEL), 0)
    head0 = (lax.broadcasted_iota(jnp.int32, (2 * CHUNK, D_MODEL), 1) // HEAD_DIM) % 2 == 0
    bcast = lambda x, s: jnp.broadcast_to(x[s:s + 1, :], (CHUNK, D_MODEL))

    def chunk(c, carry):
        rows = pl.ds(pl.multiple_of(c * CHUNK, CHUNK), CHUNK)
        st = []
        for bb in range(nb):
            r = r_ref[bb, rows, :]
            ld = ld_ref[bb, rows, :]
            v = v_ref[bb, rows, :]
            kk = kk_scr[bb, rows, :]
            cum = ld
            for sh in (1, 2, 4):
                cum = cum + jnp.where(row >= sh, pltpu.roll(cum, sh, axis=0), 0.0)
            w_cum = jnp.exp(cum)
            w_inv = jnp.exp(-cum)
            at = -kk * jnp.exp(cum - ld)
            rt = r * w_cum
            bt = kk * a_ref[bb, rows, :] * w_inv
            kt = km_scr[bb, rows, :] * w_inv
            wc = bcast(w_cum, CHUNK - 1)
            lhs = jnp.concatenate([at, rt], axis=0)
            lhs = jnp.concatenate([jnp.where(head0, lhs, 0.0), jnp.where(head0, 0.0, lhs)], axis=0).astype(bf16)
            st.append([v, at, rt, bt, kt, wc, lhs])
        for bb in range(nb):
            v, at, rt, bt, kt, wc, lhs = st[bb]
            xy = []
            for p in range(N_PAIR):
                sl = slice(p * PAIR, (p + 1) * PAIR)
                xh = lax.dot_general(lhs[:, sl], s_scr[bb, p].astype(bf16), (((1,), (1,)), ((), ())),
                                     preferred_element_type=f32)
                xy.append(xh[0:2 * CHUNK])
                xy.append(xh[2 * CHUNK:4 * CHUNK])
            st[bb][6] = jnp.concatenate(xy, axis=1)
            for s in range(CHUNK):
                bs, ks = bcast(bt, s), bcast(kt, s)
                rm = jnp.where(row >= s, rt, 0.0) if s else rt
                if s < CHUNK - 1:
                    am = jnp.where(row > s, at, 0.0)
                    p_scr[bb, 32 * s:32 * s + 16, :] = jnp.concatenate([bs * am, ks * am], axis=0).astype(bf16)
                p_scr[bb, R_OFF[s]:R_OFF[s] + 16, :] = jnp.concatenate([bs * rm, ks * rm], axis=0).astype(bf16)
            for q in range(D_MODEL // SEG):
                sl = slice(q * SEG, (q + 1) * SEG)
                e_scr[bb, :, sl] = jnp.dot(p_scr[bb, :, sl], ones, preferred_element_type=f32)
        accs = []
        for bb in range(nb):
            v, at, rt, bt, kt, wc, xy = st[bb]
            pre_u = pre_o = None
            for s in range(CHUNK):
                vs = bcast(v, s)
                to = e_scr[bb, R_OFF[s] + 8:R_OFF[s] + 16, :] * vs
                pre_o = to if pre_o is None else pre_o + to
                if s < CHUNK - 1:
                    tu = e_scr[bb, 32 * s + 8:32 * s + 16, :] * vs
                    pre_u = tu if pre_u is None else pre_u + tu
            acc_u = xy[0:CHUNK] + pre_u
            acc_o = xy[CHUNK:2 * CHUNK] + pre_o
            for s in range(CHUNK):
                us = bcast(acc_u, s)
                if s < CHUNK - 1:
                    acc_u = acc_u + e_scr[bb, 32 * s:32 * s + 8, :] * us
                acc_o = acc_o + e_scr[bb, R_OFF[s]:R_OFF[s] + 8, :] * us
            o_scr[bb, rows, :] = acc_o
            accs.append(acc_u)
        for bb in range(nb):
            v, at, rt, bt, kt, wc, xy = st[bb]
            uv = jnp.concatenate([accs[bb], v], axis=0)
            uv_swapped = pltpu.roll(uv, D_MODEL - HEAD_DIM, axis=1)
            uv = jnp.concatenate([jnp.where(head0, uv, 0.0), jnp.where(head0, uv_swapped, 0.0)], axis=0).astype(bf16)
            bk = jnp.concatenate([bt * wc, kt * wc], axis=0)
            bk = jnp.concatenate([jnp.where(head0, bk, 0.0), jnp.where(head0, 0.0, bk)], axis=0).astype(bf16)
            for p in range(N_PAIR):
                sl = slice(p * PAIR, (p + 1) * PAIR)
                upd = lax.dot_general(uv[:, p * PAIR:p * PAIR + HEAD_DIM], bk[:, sl], (((0,), (0,)), ((), ())),
                                      preferred_element_type=f32)
                s_scr[bb, p] = s_scr[bb, p] * wc[0:1, sl] + upd
        return carry

    lax.fori_loop(0, tt // CHUNK, chunk, 0)

    o = flat(o_scr)
    mu = _segsum(o, ones) * (1.0 / HEAD_DIM)
    od = o - mu
    var = _segsum(od * od, ones) * (1.0 / HEAD_DIM)
    y = od * lax.rsqrt(var + GN_EPS) * lng_ref[...] + lnb_ref[...]
    bonus = _segsum(flat(r_ref) * flat(km_scr) * rkp_ref[...], ones)
    y_ref[...] = (y + bonus * flat(v_ref)).reshape(nb, tt, D_MODEL)

    @pl.when(pl.program_id(1) == pl.num_programs(1) - 1)
    def _():
        for bb in range(nb):
            for p in range(N_PAIR):
                sp = s_scr[bb, p]
                sout_ref[bb, 2 * p] = sp[:, 0:HEAD_DIM]
                sout_ref[bb, 2 * p + 1] = sp[:, HEAD_DIM:PAIR]


def _wkv(r, ld, k, v, a, s0, P, li, nb, tt):
    bsz, tlen, _ = r.shape
    f32 = jnp.float32
    tok = pl.BlockSpec((nb, tt, D_MODEL), lambda b, t: (b, t, 0))
    stt = pl.BlockSpec((nb, HEADS, HEAD_DIM, HEAD_DIM), lambda b, t: (b, 0, 0, 0))
    lstt = pl.BlockSpec((None, nb, HEADS, HEAD_DIM, HEAD_DIM), lambda b, t: (li, b, 0, 0, 0))
    par = _lpar(P['kk'], li)
    return pl.pallas_call(
        functools.partial(_wkv_kernel, nb, tt),
        grid=(bsz // nb, tlen // tt),
        in_specs=[tok] * 5 + [lstt] + [par] * 5 + [pl.BlockSpec((SEG, SEG), lambda b, t: (0, 0))],
        out_specs=[tok, stt],
        out_shape=[jax.ShapeDtypeStruct((bsz, tlen, D_MODEL), f32),
                   jax.ShapeDtypeStruct((bsz, HEADS, HEAD_DIM, HEAD_DIM), f32)],
        scratch_shapes=[pltpu.VMEM((nb, N_PAIR, HEAD_DIM, PAIR), f32),
                        pltpu.VMEM((nb, tt, D_MODEL), f32), pltpu.VMEM((nb, tt, D_MODEL), f32),
                        pltpu.VMEM((nb, tt, D_MODEL), f32),
                        pltpu.VMEM((nb, P_ROWS, D_MODEL), jnp.bfloat16),
                        pltpu.VMEM((nb, P_ROWS, D_MODEL), f32)],
        compiler_params=_cparams(("arbitrary", "arbitrary")),
        name="wkv",
    )(r, ld, k, v, a, s0, P['kk'], P['ka'], P['rk'], P['lng'], P['lnb'], P['ones'])


LANES = 128


def _wkv_lanes_kernel(tlen, r_ref, ld_ref, k_ref, v_ref, a_ref, s0_ref, kkp_ref, kap_ref, rkp_ref, lng_ref, lnb_ref,
                      y_ref, s_ref, o_scr):
    s_ref[...] = s0_ref[...]
    for t in range(tlen):
        rt = r_ref[t]
        wt = jnp.exp(ld_ref[t])
        kraw = k_ref[t]
        at = a_ref[t]
        kkp = kraw * kkp_ref[...]
        kk = kkp * lax.rsqrt(jnp.maximum(jnp.sum(kkp * kkp, axis=0, keepdims=True), 1e-24))
        kt = kraw * (1.0 + (at - 1.0) * kap_ref[...])
        bt = kk * at
        nkk = -kk
        for i in range(HEAD_DIM):
            si = s_ref[i]
            sa = jnp.sum(si * nkk, axis=0, keepdims=True)
            sn = si * wt + sa * bt + v_ref[t, i:i + 1, :] * kt
            s_ref[i] = sn
            o_scr[i:i + 1, :] = jnp.sum(sn * rt, axis=0, keepdims=True)
        o = o_scr[...]
        mu = jnp.mean(o, axis=0, keepdims=True)
        od = o - mu
        var = jnp.mean(od * od, axis=0, keepdims=True)
        y = od * lax.rsqrt(var + GN_EPS) * lng_ref[...] + lnb_ref[...]
        bonus = jnp.sum(rt * kt * rkp_ref[...], axis=0, keepdims=True)
        y_ref[t] = y + bonus * v_ref[t]


def _wkv_lanes(r, ld, k, v, a, s0, P, li):
    tlen, _, _, bsz = r.shape
    f32 = jnp.float32
    tok = pl.BlockSpec((tlen, None, HEAD_DIM, bsz), lambda h: (0, h, 0, 0))
    lstt = pl.BlockSpec((None, None, HEAD_DIM, HEAD_DIM, bsz), lambda h: (li, h, 0, 0, 0))
    stt = pl.BlockSpec((None, HEAD_DIM, HEAD_DIM, bsz), lambda h: (h, 0, 0, 0))
    par = pl.BlockSpec((None, None, HEAD_DIM, bsz), lambda h: (li, h, 0, 0))
    hp = P['head_lanes']
    return pl.pallas_call(
        functools.partial(_wkv_lanes_kernel, tlen),
        grid=(HEADS,),
        in_specs=[tok] * 5 + [lstt] + [par] * 5,
        out_specs=[tok, stt],
        out_shape=[jax.ShapeDtypeStruct((tlen, HEADS, HEAD_DIM, bsz), f32),
                   jax.ShapeDtypeStruct((HEADS, HEAD_DIM, HEAD_DIM, bsz), f32)],
        scratch_shapes=[pltpu.VMEM((HEAD_DIM, bsz), f32)],
        compiler_params=_cparams(("arbitrary",)),
        name="wkv_lanes",
    )(r, ld, k, v, a, s0, hp['kk'], hp['ka'], hp['rk'], hp['lng'], hp['lnb'])


def _merge_kernel(h_ref, xa_ref, yw_ref, g_ref, xc_ref, zg_ref, wa_ref, wb_ref, wc_ref, bc_ref, wo_ref, o_ref):
    ya = _bdot(xa_ref[...], wa_ref[...])
    yb = _bdot(yw_ref[...] * g_ref[...], wb_ref[...])
    yc = _bdot(xc_ref[...], wc_ref[...]) + bc_ref[...]
    m = (_sigmoid(zg_ref[:, 0:D_MODEL]) * ya + _sigmoid(zg_ref[:, D_MODEL:2 * D_MODEL]) * yb
         + _sigmoid(zg_ref[:, 2 * D_MODEL:3 * D_MODEL]) * yc)
    o_ref[...] = h_ref[...] + _bdot(m, wo_ref[...])


def _merge(h, xa, yw, g, xc, z, P, li, tm):
    n = h.shape[0]
    tok = lambda width: pl.BlockSpec((tm, width), lambda i: (i, 0))
    par = lambda a: _lpar(a, li)
    ws = [P[n_] for n_ in ('sc_w_out', 'rk_w_out', 'cf_w_out', 'cf_b_out', 'w_o')]
    return pl.pallas_call(
        _merge_kernel,
        grid=(n // tm,),
        in_specs=[tok(D_MODEL), tok(D_SC), tok(D_MODEL), tok(D_MODEL), tok(D_CF), tok(Z_BLK)] + [par(a) for a in ws],
        out_specs=tok(D_MODEL),
        out_shape=jax.ShapeDtypeStruct((n, D_MODEL), jnp.float32),
        compiler_params=_cparams(("arbitrary",)),
        name="merge",
    )(h, xa, yw, g, xc, z, *ws)


def _mlp_ple_kernel(nk, final, h_ref, p_ref, g2_ref, w1_ref, w2_ref, gp_ref, wg_ref, wp_ref, *rest):
    gf_ref = rest[0] if final else None
    o_ref, hn_scr, acc_scr = rest[-3:]
    kstep = pl.program_id(1)

    @pl.when(kstep == 0)
    def _():
        hn_scr[...] = _rms(h_ref[...], g2_ref[...]).astype(jnp.bfloat16)
        acc_scr[...] = jnp.zeros_like(acc_scr)

    u = jnp.dot(hn_scr[...], w1_ref[...], preferred_element_type=jnp.float32)
    u = jnp.maximum(u, 0.0)
    acc_scr[...] += _bdot(u * u, w2_ref[...])

    @pl.when(kstep == nk - 1)
    def _():
        h1 = h_ref[...] + acc_scr[...]
        gate = _sigmoid(_bdot(_rms(h1, gp_ref[...]), wg_ref[...]))
        h2 = h1 + gate * _bdot(p_ref[...], wp_ref[...])
        o_ref[...] = h2 if gf_ref is None else _rms(h2, gf_ref[...])


def _mlp_ple(h, p, P, li, tm, tf, final_g=None):
    n = h.shape[0]
    nk = D_FF // tf
    par = lambda a: _lpar(a, li)
    extra = [] if final_g is None else [final_g]
    return pl.pallas_call(
        functools.partial(_mlp_ple_kernel, nk, final_g is not None),
        grid=(n // tm, nk),
        in_specs=[pl.BlockSpec((tm, D_MODEL), lambda i, k: (i, 0)),
                  pl.BlockSpec((None, tm, D_PLE), lambda i, k: (li, i, 0)),
                  par(P['norm2_g']),
                  pl.BlockSpec((None, D_MODEL, tf), lambda i, k: (li, 0, k)),
                  pl.BlockSpec((None, tf, D_MODEL), lambda i, k: (li, k, 0)),
                  par(P['ple_norm_g']), par(P['ple_gate_w']), par(P['ple_w'])]
                 + [pl.BlockSpec((1, D_MODEL), lambda i, k: (0, 0)) for _ in extra],
        out_specs=pl.BlockSpec((tm, D_MODEL), lambda i, k: (i, 0)),
        out_shape=jax.ShapeDtypeStruct((n, D_MODEL), jnp.float32),
        scratch_shapes=[pltpu.VMEM((tm, D_MODEL), jnp.bfloat16), pltpu.VMEM((tm, D_MODEL), jnp.float32)],
        compiler_params=_cparams(("arbitrary", "arbitrary")),
        name="mlp_ple",
    )(h, p, P['norm2_g'], P['mlp_w1'], P['mlp_w2'], P['ple_norm_g'], P['ple_gate_w'], P['ple_w'], *extra)


def _prep(W):
    bf = jnp.bfloat16
    nl = W['w_in'].shape[0]
    row = lambda a: a.reshape(nl, 1, -1)
    w_in = W['w_in']
    n_sc, n_rk, n_cf = 3 * D_SC, RK_PROJ, 2 * D_CF
    o_rk, o_cf, o_gate = n_sc, n_sc + n_rk, n_sc + n_rk + n_cf
    w_perm = jnp.concatenate([
        w_in[:, :, o_gate:o_gate + 3 * D_MODEL], w_in[:, :, o_rk:o_rk + 3 * D_MODEL], w_in[:, :, 0:n_sc],
        w_in[:, :, o_cf:o_cf + n_cf], w_in[:, :, o_rk + 3 * D_MODEL:o_rk + RK_PROJ],
        jnp.zeros((nl, D_MODEL, LORA_PAD - N_LORA), w_in.dtype)], axis=2).astype(bf)
    mu = W['rk_mu']
    zrows = lambda n: jnp.zeros((nl, n, D_MODEL), jnp.float32)
    o_a, o_g = DECAY_LORA, DECAY_LORA + AAA_LORA + MV_LORA
    seg_head = jnp.arange(SEG) // HEAD_DIM
    on_lanes = lambda a: jnp.broadcast_to(a.reshape(nl, HEADS, HEAD_DIM, 1), (nl, HEADS, HEAD_DIM, LANES))
    return {
        'head_lanes': {'kk': on_lanes(W['rk_k_k']), 'ka': on_lanes(W['rk_k_a']), 'rk': on_lanes(W['rk_r_k']),
                       'lng': on_lanes(W['rk_ln_g']), 'lnb': on_lanes(W['rk_ln_b'])},
        'ones': (seg_head[:, None] == seg_head[None, :]).astype(bf),
        'kk': row(W['rk_k_k']), 'ka': row(W['rk_k_a']), 'rk': row(W['rk_r_k']),
        'lng': row(W['rk_ln_g']), 'lnb': row(W['rk_ln_b']),
        'norm1_g': row(W['norm1_g']), 'w_in': w_perm,
        'mu1': row(mu[:, :3 * D_MODEL]),
        'mu2': row(jnp.pad(mu[:, 3 * D_MODEL:], ((0, 0), (0, LORA_PAD - N_LORA)))),
        'w0': row(W['rk_w0']), 'a0': row(W['rk_a0']), 'v0': row(W['rk_v0']),
        'w2p': jnp.concatenate([W['rk_w2'], zrows(128 - o_a)], axis=1).astype(bf),
        'a2p': jnp.concatenate([zrows(o_a), W['rk_a2']], axis=1).astype(bf),
        'v2p': jnp.concatenate([W['rk_v2'], zrows(128 - MV_LORA)], axis=1).astype(bf),
        'g2p': jnp.concatenate([zrows(o_g - 128), W['rk_g2'], zrows(384 - N_LORA)], axis=1).astype(bf),
        'scw': W['sc_conv_w'], 'cfbin': row(W['cf_b_in']), 'cfw': W['cf_dw_w'],
        'cfdb': row(W['cf_dw_b']), 'cflg': row(W['cf_ln_g']), 'cflb': row(W['cf_ln_b']),
        'sc_w_out': W['sc_w_out'].astype(bf), 'rk_w_out': W['rk_w_out'].astype(bf),
        'cf_w_out': W['cf_w_out'].astype(bf), 'cf_b_out': row(W['cf_b_out']), 'w_o': W['w_o'].astype(bf),
        'norm2_g': row(W['norm2_g']), 'mlp_w1': W['mlp_w1'].astype(bf), 'mlp_w2': W['mlp_w2'].astype(bf),
        'ple_norm_g': row(W['ple_norm_g']), 'ple_gate_w': W['ple_gate_w'].astype(bf),
        'ple_w': W['ple_w'].astype(bf),
    }


def _pick(n, pref):
    return pref if n % pref == 0 else n


def _trunk(x, p, sc0, shift0, wkv0, cf0, P, final_g, cfg):
    bsz, tlen, _ = x.shape
    n = bsz * tlen
    tm = _pick(n, cfg['tm'])
    tt = _pick(tlen, cfg['tt'])
    tpad = -tlen % CHUNK
    tw = _pick(tlen + tpad, cfg['tw'])
    pad_t = lambda a: jnp.pad(a, ((0, 0), (0, tpad), (0, 0))) if tpad else a
    h = x.reshape(n, D_MODEL)
    pf = p.reshape(DEPTH, n, D_PLE)
    sh1 = shift0[:, :, None, :3 * D_MODEL]
    sh2 = jnp.pad(shift0[:, :, 3 * D_MODEL:], ((0, 0), (0, 0), (0, LORA_PAD - N_LORA)))[:, :, None, :]
    lanes_form = bsz == LANES and tlen <= CHUNK
    to_lanes = lambda a: a.reshape(bsz, tlen, HEADS, HEAD_DIM).transpose(1, 2, 3, 0)
    wkv0_l = wkv0.transpose(0, 2, 3, 4, 1) if lanes_form else None
    v_first = None
    new_sc, new_shift, new_wkv, new_cf = [], [], [], []
    for i in range(DEPTH):
        z = _in_proj(h, P, i, tm)
        (r, w, k, v, a, g, xa, xc, nsh1, nsh2, nsc, ncf) = _seq(
            z.reshape(bsz, tlen, NZ), sh1, sh2, sc0, cf0, v_first, P, i, tt, cfg['nbs'])
        if i == 0:
            v_first = v
        if lanes_form:
            y, s_out = _wkv_lanes(to_lanes(r), to_lanes(w), to_lanes(k), to_lanes(v), to_lanes(a), wkv0_l, P, i)
            yw = y.transpose(3, 0, 1, 2).reshape(n, D_MODEL)
        else:
            y, s_out = _wkv(pad_t(r), pad_t(w), pad_t(k), pad_t(v), pad_t(a), wkv0, P, i, cfg['nb'], tw)
            yw = y[:, :tlen].reshape(n, D_MODEL)
        h = _merge(h, xa.reshape(n, D_SC), yw, g.reshape(n, D_MODEL), xc.reshape(n, D_CF), z, P, i, tm)
        h = _mlp_ple(h, pf, P, i, tm, cfg['tf'], final_g if i == DEPTH - 1 else None)
        new_sc.append(nsc)
        new_shift.append(jnp.concatenate([nsh1[:, 0, :], nsh2[:, 0, :N_LORA]], axis=-1))
        new_wkv.append(s_out)
        new_cf.append(ncf)
    y = h.reshape(bsz, tlen, D_MODEL)
    wkv_out = jnp.stack(new_wkv)
    if lanes_form:
        wkv_out = wkv_out.transpose(0, 4, 1, 2, 3)
    return y, jnp.stack(new_sc), jnp.stack(new_shift), wkv_out, jnp.stack(new_cf)


PROMPT_CFG = {'tm': 512, 'tt': 256, 'nbs': 1, 'tw': 64, 'nb': 8, 'tf': 2048}
SAMPLE_CFG = {'tm': 512, 'tt': 4, 'nbs': 8, 'tw': 8, 'nb': 8, 'tf': 1024}


def kernel(x_prompt, x_sample, p_prompt, p_sample, state_sconv, state_shift, state_wkv, state_cconv,
           norm1_g, w_in, sc_conv_w, sc_w_out, rk_mu, rk_w0, rk_w2, rk_a0, rk_a2, rk_v0, rk_v2, rk_g2,
           rk_k_k, rk_k_a, rk_r_k, rk_ln_g, rk_ln_b, rk_w_out, cf_b_in, cf_dw_w, cf_dw_b, cf_ln_g, cf_ln_b,
           cf_w_out, cf_b_out, w_o, norm2_g, mlp_w1, mlp_w2, ple_w, ple_gate_w, ple_norm_g, final_norm_g):
    W = dict(norm1_g=norm1_g, w_in=w_in, sc_conv_w=sc_conv_w, sc_w_out=sc_w_out, rk_mu=rk_mu, rk_w0=rk_w0,
             rk_w2=rk_w2, rk_a0=rk_a0, rk_a2=rk_a2, rk_v0=rk_v0, rk_v2=rk_v2, rk_g2=rk_g2, rk_k_k=rk_k_k,
             rk_k_a=rk_k_a, rk_r_k=rk_r_k, rk_ln_g=rk_ln_g, rk_ln_b=rk_ln_b, rk_w_out=rk_w_out, cf_b_in=cf_b_in,
             cf_dw_w=cf_dw_w, cf_dw_b=cf_dw_b, cf_ln_g=cf_ln_g, cf_ln_b=cf_ln_b, cf_w_out=cf_w_out,
             cf_b_out=cf_b_out, w_o=w_o, norm2_g=norm2_g, mlp_w1=mlp_w1, mlp_w2=mlp_w2, ple_w=ple_w,
             ple_gate_w=ple_gate_w, ple_norm_g=ple_norm_g)
    P = _prep(W)
    final_g = final_norm_g.reshape(1, D_MODEL)
    bp = x_prompt.shape[0]
    f32 = jnp.float32
    zsc = jnp.zeros((DEPTH, bp, SC_WIDTH - 1, D_SC), f32)
    zsh = jnp.zeros((DEPTH, bp, RK_PROJ), f32)
    zwkv = jnp.zeros((DEPTH, bp, HEADS, HEAD_DIM, HEAD_DIM), f32)
    zcf = jnp.zeros((DEPTH, bp, CF_WIDTH - 1, D_CF), f32)
    y_p, sc_p, sh_p, wkv_p, cf_p = _trunk(x_prompt, p_prompt, zsc, zsh, zwkv, zcf, P, final_g, PROMPT_CFG)
    y_s, sc_s, sh_s, wkv_s, cf_s = _trunk(x_sample, p_sample, state_sconv, state_shift, state_wkv, state_cconv,
                                          P, final_g, SAMPLE_CFG)
    return (y_p, y_s, sc_p, sh_p, wkv_p, cf_p, sc_s, sh_s, wkv_s, cf_s)
```

```python
import functools

import jax
import jax.numpy as jnp
from jax import lax
from jax.experimental import pallas as pl
from jax.experimental.pallas import tpu as pltpu

D_MODEL = 1024
DEPTH = 4
D_SC = 512
D_CF = 512
SC_WIDTH = 3
CF_WIDTH = 31
HEAD_DIM = 64
HEADS = 16
D_FF = 4096
D_PLE = 256
DECAY_LORA, AAA_LORA, MV_LORA, GATE_LORA = 64, 64, 32, 128
N_LORA = DECAY_LORA + AAA_LORA + MV_LORA + GATE_LORA
RK_PROJ = 3 * D_MODEL + N_LORA
RMS_EPS = 1e-6
LN_EPS = 1e-5
GN_EPS = 64e-5
EXP_M05 = 0.6065306597126334

Z_BLK = 3 * D_MODEL
NZ = 3 * Z_BLK
Z2_SC = 0
Z2_CF = 3 * D_SC
Z2_LORA = Z2_CF + 2 * D_CF
LORA_PAD = Z_BLK - Z2_LORA

VMEM_LIMIT_BYTES = 56 * 1024 * 1024


def _cparams(sem):
    return pltpu.CompilerParams(dimension_semantics=sem, vmem_limit_bytes=VMEM_LIMIT_BYTES)


def _sigmoid(x):
    return 1.0 / (1.0 + jnp.exp(-x))


def _rms(x, g):
    return x * lax.rsqrt(jnp.mean(x * x, axis=-1, keepdims=True) + RMS_EPS) * g


def _bdot(a, b):
    return jnp.dot(a.astype(jnp.bfloat16), b, preferred_element_type=jnp.float32)


def _in_proj_kernel(x_ref, g_ref, w_ref, o_ref):
    o_ref[...] = _bdot(_rms(x_ref[...], g_ref[...]), w_ref[...])


def _lpar(a, li):
    return pl.BlockSpec((None,) + a.shape[1:], lambda *_: (li,) + (0,) * (a.ndim - 1))


def _in_proj(x, P, li, tm):
    n = x.shape[0]
    g, w = P['norm1_g'], P['w_in']
    return pl.pallas_call(
        _in_proj_kernel,
        grid=(NZ // Z_BLK, n // tm),
        in_specs=[
            pl.BlockSpec((tm, D_MODEL), lambda j, i: (i, 0)),
            _lpar(g, li),
            pl.BlockSpec((None, D_MODEL, Z_BLK), lambda j, i: (li, 0, j)),
        ],
        out_specs=pl.BlockSpec((tm, Z_BLK), lambda j, i: (i, j)),
        out_shape=jax.ShapeDtypeStruct((n, NZ), jnp.float32),
        compiler_params=_cparams(("arbitrary", "arbitrary")),
        name="in_proj",
    )(x, g, w)


def _seq_batched(first_layer):
    n_in = 6 if first_layer else 7
    n_par = 15
    return set(range(n_in)) | set(range(n_in + n_par, n_in + n_par + 12 + 5))


SEQ_BATCHED = {True: _seq_batched(True), False: _seq_batched(False)}


def _seq_kernel(first_layer, tt, nbs, *refs):
    for bb in range(nbs):
        _seq_body(first_layer, tt, [r.at[bb] if i in SEQ_BATCHED[first_layer] else r for i, r in enumerate(refs)])


def _seq_body(first_layer, tt, refs):
    it = iter(refs)
    z1_ref, z2_ref, sh1_ref, sh2_ref, sc0_ref, cf0_ref = (next(it) for _ in range(6))
    vf_ref = None if first_layer else next(it)
    (mu1_ref, mu2_ref, w0_ref, w2_ref, a0_ref, a2_ref, v0_ref, v2_ref, g2_ref,
     scw_ref, cfbin_ref, cfw_ref, cfdb_ref, cflg_ref, cflb_ref) = (next(it) for _ in range(15))
    (r_ref, w_ref, k_ref, v_ref, a_ref, g_ref, xa_ref, xc_ref,
     nsh1_ref, nsh2_ref, nsc_ref, ncf_ref) = (next(it) for _ in range(12))
    s1buf, s2buf, cbuf, gbuf, pbuf = (next(it) for _ in range(5))

    @pl.when(pl.program_id(1) == 0)
    def _():
        s1buf[7:8, :] = sh1_ref[...]
        s2buf[7:8, :] = sh2_ref[...]
        cbuf[6:8, :] = sc0_ref[...]
        gbuf[2:CF_WIDTH + 1, :] = cf0_ref[...]
        gbuf[0:2, :] = jnp.zeros((2, D_CF), jnp.float32)
        gbuf[CF_WIDTH + 1 + tt:CF_WIDTH + 1 + tt + 8, :] = jnp.zeros((8, D_CF), jnp.float32)

    z1 = z1_ref[...]
    if tt % 8 == 0:
        rowi = lax.broadcasted_iota(jnp.int32, (tt, Z_BLK), 0)
        zp1 = jnp.where(rowi == 0, jnp.broadcast_to(s1buf[7:8, :], (tt, Z_BLK)), pltpu.roll(z1, 1, axis=0))
        s1buf[7 + tt:8 + tt, :] = z1[tt - 1:tt, :]
    else:
        s1buf[8:8 + tt, :] = z1
        zp1 = s1buf[7:7 + tt, :]
    zm1 = z1 + (zp1 - z1) * mu1_ref[...]
    z2l = z2_ref[:, Z2_LORA:Z_BLK]
    if tt % 8 == 0:
        rowl = lax.broadcasted_iota(jnp.int32, (tt, LORA_PAD), 0)
        zp2 = jnp.where(rowl == 0, jnp.broadcast_to(s2buf[7:8, :], (tt, LORA_PAD)),
                        pltpu.roll(z2l, 1, axis=0))
        s2buf[7 + tt:8 + tt, :] = z2l[tt - 1:tt, :]
    else:
        s2buf[8:8 + tt, :] = z2l
        zp2 = s2buf[7:7 + tt, :]
    zm2 = z2l + (zp2 - z2l) * mu2_ref[...]
    last1 = s1buf[7 + tt:8 + tt, :]
    last2 = s2buf[7 + tt:8 + tt, :]
    nsh1_ref[...] = last1
    nsh2_ref[...] = last2
    s1buf[7:8, :] = last1
    s2buf[7:8, :] = last2

    x01 = zm2[:, 0:128]
    lw = _bdot(jnp.tanh(x01), w2_ref[...])
    la = _bdot(x01, a2_ref[...])
    w_ref[...] = -EXP_M05 * _sigmoid(w0_ref[...] + lw)
    a_ref[...] = _sigmoid(a0_ref[...] + la)
    g_ref[...] = _bdot(_sigmoid(zm2[:, 128:384]), g2_ref[...])
    r_ref[...] = zm1[:, 0:D_MODEL]
    k_ref[...] = zm1[:, D_MODEL:2 * D_MODEL]
    v = zm1[:, 2 * D_MODEL:3 * D_MODEL]
    if not first_layer:
        lv = _bdot(zm2[:, 128:256], v2_ref[...])
        v = v + (vf_ref[...] - v) * _sigmoid(v0_ref[...] + lv)
    v_ref[...] = v

    cu = z2_ref[:, Z2_SC + D_SC:Z2_SC + 2 * D_SC] * z2_ref[:, Z2_SC + 2 * D_SC:Z2_SC + 3 * D_SC]
    cbuf[8:8 + tt, :] = cu
    conv_a = (scw_ref[0:1, :] * cbuf[6:6 + tt, :] + scw_ref[1:2, :] * cbuf[7:7 + tt, :]
              + scw_ref[2:3, :] * cu)
    xa_ref[...] = z2_ref[:, Z2_SC:Z2_SC + D_SC] * conv_a
    tail_a = cbuf[6 + tt:8 + tt, :]
    nsc_ref[...] = tail_a
    cbuf[6:8, :] = tail_a

    zc = z2_ref[:, Z2_CF:Z2_CF + 2 * D_CF] + cfbin_ref[...]
    glu = zc[:, 0:D_CF] * _sigmoid(zc[:, D_CF:2 * D_CF])
    gbuf[CF_WIDTH + 1:CF_WIDTH + 1 + tt, :] = glu
    if tt % 8 == 0:
        acc = None
        for ph in range(8):
            part = None
            for j in range(ph, CF_WIDTH + 2, 8):
                if j < 2:
                    continue
                term = cfw_ref[j - 2:j - 1, :] * gbuf[j - ph:j - ph + tt + 8, :]
                part = term if part is None else part + term
            pbuf[...] = part
            sh = pbuf[ph:ph + tt, :]
            acc = sh if acc is None else acc + sh
    else:
        acc = cfw_ref[0:1, :] * gbuf[2:2 + tt, :]
        for kk in range(1, CF_WIDTH):
            acc = acc + cfw_ref[kk:kk + 1, :] * gbuf[2 + kk:2 + kk + tt, :]
    cc = acc + cfdb_ref[...]
    mu = jnp.mean(cc, axis=-1, keepdims=True)
    cd = cc - mu
    var = jnp.mean(cd * cd, axis=-1, keepdims=True)
    ln = cd * lax.rsqrt(var + LN_EPS) * cflg_ref[...] + cflb_ref[...]
    xc_ref[...] = ln * _sigmoid(ln)
    tail_c = gbuf[2 + tt:CF_WIDTH + 1 + tt, :]
    ncf_ref[...] = tail_c
    gbuf[2:CF_WIDTH + 1, :] = tail_c


def _seq(z, sh1, sh2, sc0, cf0, vfirst, P, li, tt, nbs):
    bsz, tlen, _ = z.shape
    first_layer = vfirst is None
    tok = lambda width, blk: pl.BlockSpec((nbs, tt, width), lambda b, t, blk=blk: (b, t, blk))
    st = lambda rows, width: pl.BlockSpec((nbs, rows, width), lambda b, t: (b, 0, 0))
    lst = lambda rows, width: pl.BlockSpec((None, nbs, rows, width), lambda b, t: (li, b, 0, 0))
    par = lambda a: _lpar(a, li)
    params = [P[n] for n in ('mu1', 'mu2', 'w0', 'w2p', 'a0', 'a2p', 'v0', 'v2p', 'g2p',
                              'scw', 'cfbin', 'cfw', 'cfdb', 'cflg', 'cflb')]
    args = [z, z, sh1, sh2, sc0, cf0]
    in_specs = [tok(Z_BLK, 1), tok(Z_BLK, 2), lst(1, Z_BLK), lst(1, LORA_PAD),
                lst(SC_WIDTH - 1, D_SC), lst(CF_WIDTH - 1, D_CF)]
    if not first_layer:
        args.append(vfirst)
        in_specs.append(tok(D_MODEL, 0))
    args += params
    in_specs += [par(a) for a in params]
    f32 = jnp.float32
    out_shape = ([jax.ShapeDtypeStruct((bsz, tlen, D_MODEL), f32)] * 6
                 + [jax.ShapeDtypeStruct((bsz, tlen, D_SC), f32), jax.ShapeDtypeStruct((bsz, tlen, D_CF), f32),
                    jax.ShapeDtypeStruct((bsz, 1, Z_BLK), f32), jax.ShapeDtypeStruct((bsz, 1, LORA_PAD), f32),
                    jax.ShapeDtypeStruct((bsz, SC_WIDTH - 1, D_SC), f32),
                    jax.ShapeDtypeStruct((bsz, CF_WIDTH - 1, D_CF), f32)])
    out_specs = ([tok(D_MODEL, 0)] * 6 + [tok(D_SC, 0), tok(D_CF, 0), st(1, Z_BLK), st(1, LORA_PAD),
                                           st(SC_WIDTH - 1, D_SC), st(CF_WIDTH - 1, D_CF)])
    scratch = [pltpu.VMEM((nbs, 8 + tt, Z_BLK), f32), pltpu.VMEM((nbs, 8 + tt, LORA_PAD), f32),
               pltpu.VMEM((nbs, 8 + tt, D_SC), f32), pltpu.VMEM((nbs, CF_WIDTH + 1 + tt + 8, D_CF), f32),
               pltpu.VMEM((nbs, tt + 8, D_CF), f32)]
    return pl.pallas_call(
        functools.partial(_seq_kernel, first_layer, tt, nbs),
        grid=(bsz // nbs, tlen // tt),
        in_specs=in_specs, out_specs=out_specs, out_shape=out_shape, scratch_shapes=scratch,
        compiler_params=_cparams(("arbitrary", "arbitrary")),
        name="seq",
    )(*args)


CHUNK = 8
PAIR = 2 * HEAD_DIM
N_PAIR = HEADS // 2
SEG = 256
P_ROWS = 4 * CHUNK * CHUNK - 2 * CHUNK
R_OFF = [32 * s + 16 if s < CHUNK - 1 else 32 * s for s in range(CHUNK)]


def _segsum(x, ones):
    xb = x.astype(jnp.bfloat16)
    return jnp.concatenate([jnp.dot(xb[:, q * SEG:(q + 1) * SEG], ones, preferred_element_type=jnp.float32)
                            for q in range(D_MODEL // SEG)], axis=1)


def _wkv_kernel(nb, tt, r_ref, ld_ref, k_ref, v_ref, a_ref, s0_ref, kkp_ref, kap_ref, rkp_ref, lng_ref, lnb_ref,
                ones_ref, y_ref, sout_ref, s_scr, kk_scr, km_scr, o_scr, p_scr, e_scr):
    f32, bf16 = jnp.float32, jnp.bfloat16
    ones = ones_ref[...]

    @pl.when(pl.program_id(1) == 0)
    def _():
        for bb in range(nb):
            for p in range(N_PAIR):
                s_scr[bb, p] = jnp.concatenate([s0_ref[bb, 2 * p], s0_ref[bb, 2 * p + 1]], axis=1)

    flat = lambda ref: ref[...].reshape(nb * tt, D_MODEL)
    kraw = flat(k_ref)
    kkp = kraw * kkp_ref[...]
    kk_scr[...] = (kkp * lax.rsqrt(jnp.maximum(_segsum(kkp * kkp, ones), 1e-24))).reshape(nb, tt, D_MODEL)
    km_scr[...] = (kraw * (1.0 + (flat(a_ref) - 1.0) * kap_ref[...])).reshape(nb, tt, D_MODEL)

    row = lax.broadcasted_iota(jnp.int32, (CHUNK, D_MODEL), 0)
    head0 = (lax.broadcasted_iota(jnp.int32, (2 * CHUNK, D_MODEL), 1) // HEAD_DIM) % 2 == 0
    bcast = lambda x, s: jnp.broadcast_to(x[s:s + 1, :], (CHUNK, D_MODEL))

    def chunk(c, carry):
        rows = pl.ds(pl.multiple_of(c * CHUNK, CHUNK), CHUNK)
        st = []
        for bb in range(nb):
            r = r_ref[bb, rows, :]
            ld = ld_ref[bb, rows, :]
            v = v_ref[bb, rows, :]
            kk = kk_scr[bb, rows, :]
            cum = ld
            for sh in (1, 2, 4):
                cum = cum + jnp.where(row >= sh, pltpu.roll(cum, sh, axis=0), 0.0)
            w_cum = jnp.exp(cum)
            w_inv = jnp.exp(-cum)
            at = -kk * jnp.exp(cum - ld)
            rt = r * w_cum
            bt = kk * a_ref[bb, rows, :] * w_inv
            kt = km_scr[bb, rows, :] * w_inv
            wc = bcast(w_cum, CHUNK - 1)
            lhs = jnp.concatenate([at, rt], axis=0)
            lhs = jnp.concatenate([jnp.where(head0, lhs, 0.0), jnp.where(head0, 0.0, lhs)], axis=0).astype(bf16)
            st.append([v, at, rt, bt, kt, wc, lhs])
        for bb in range(nb):
            v, at, rt, bt, kt, wc, lhs = st[bb]
            xy = []
            for p in range(N_PAIR):
                sl = slice(p * PAIR, (p + 1) * PAIR)
                xh = lax.dot_general(lhs[:, sl], s_scr[bb, p].astype(bf16), (((1,), (1,)), ((), ())),
                                     preferred_element_type=f32)
                xy.append(xh[0:2 * CHUNK])
                xy.append(xh[2 * CHUNK:4 * CHUNK])
            st[bb][6] = jnp.concatenate(xy, axis=1)
            for s in range(CHUNK):
                bs, ks = bcast(bt, s), bcast(kt, s)
                rm = jnp.where(row >= s, rt, 0.0) if s else rt
                if s < CHUNK - 1:
                    am = jnp.where(row > s, at, 0.0)
                    p_scr[bb, 32 * s:32 * s + 16, :] = jnp.concatenate([bs * am, ks * am], axis=0).astype(bf16)
                p_scr[bb, R_OFF[s]:R_OFF[s] + 16, :] = jnp.concatenate([bs * rm, ks * rm], axis=0).astype(bf16)
            for q in range(D_MODEL // SEG):
                sl = slice(q * SEG, (q + 1) * SEG)
                e_scr[bb, :, sl] = jnp.dot(p_scr[bb, :, sl], ones, preferred_element_type=f32)
        accs = []
        for bb in range(nb):
            v, at, rt, bt, kt, wc, xy = st[bb]
            pre_u = pre_o = None
            for s in range(CHUNK):
                vs = bcast(v, s)
                to = e_scr[bb, R_OFF[s] + 8:R_OFF[s] + 16, :] * vs
                pre_o = to if pre_o is None else pre_o + to
                if s < CHUNK - 1:
                    tu = e_scr[bb, 32 * s + 8:32 * s + 16, :] * vs
                    pre_u = tu if pre_u is None else pre_u + tu
            acc_u = xy[0:CHUNK] + pre_u
            acc_o = xy[CHUNK:2 * CHUNK] + pre_o
            for s in range(CHUNK):
                us = bcast(acc_u, s)
                if s < CHUNK - 1:
                    acc_u = acc_u + e_scr[bb, 32 * s:32 * s + 8, :] * us
                acc_o = acc_o + e_scr[bb, R_OFF[s]:R_OFF[s] + 8, :] * us
            o_scr[bb, rows, :] = acc_o
            accs.append(acc_u)
        for bb in range(nb):
            v, at, rt, bt, kt, wc, xy = st[bb]
            uv = jnp.concatenate([accs[bb], v], axis=0)
            uv_swapped = pltpu.roll(uv, D_MODEL - HEAD_DIM, axis=1)
            uv = jnp.concatenate([jnp.where(head0, uv, 0.0), jnp.where(head0, uv_swapped, 0.0)], axis=0).astype(bf16)
            bk = jnp.concatenate([bt * wc, kt * wc], axis=0)
            bk = jnp.concatenate([jnp.where(head0, bk, 0.0), jnp.where(head0, 0.0, bk)], axis=0).astype(bf16)
            for p in range(N_PAIR):
                sl = slice(p * PAIR, (p + 1) * PAIR)
                upd = lax.dot_general(uv[:, p * PAIR:p * PAIR + HEAD_DIM], bk[:, sl], (((0,), (0,)), ((), ())),
                                      preferred_element_type=f32)
                s_scr[bb, p] = s_scr[bb, p] * wc[0:1, sl] + upd
        return carry

    lax.fori_loop(0, tt // CHUNK, chunk, 0)

    o = flat(o_scr)
    mu = _segsum(o, ones) * (1.0 / HEAD_DIM)
    od = o - mu
    var = _segsum(od * od, ones) * (1.0 / HEAD_DIM)
    y = od * lax.rsqrt(var + GN_EPS) * lng_ref[...] + lnb_ref[...]
    bonus = _segsum(flat(r_ref) * flat(km_scr) * rkp_ref[...], ones)
    y_ref[...] = (y + bonus * flat(v_ref)).reshape(nb, tt, D_MODEL)

    @pl.when(pl.program_id(1) == pl.num_programs(1) - 1)
    def _():
        for bb in range(nb):
            for p in range(N_PAIR):
                sp = s_scr[bb, p]
                sout_ref[bb, 2 * p] = sp[:, 0:HEAD_DIM]
                sout_ref[bb, 2 * p + 1] = sp[:, HEAD_DIM:PAIR]


def _wkv(r, ld, k, v, a, s0, P, li, nb, tt):
    bsz, tlen, _ = r.shape
    f32 = jnp.float32
    tok = pl.BlockSpec((nb, tt, D_MODEL), lambda b, t: (b, t, 0))
    stt = pl.BlockSpec((nb, HEADS, HEAD_DIM, HEAD_DIM), lambda b, t: (b, 0, 0, 0))
    lstt = pl.BlockSpec((None, nb, HEADS, HEAD_DIM, HEAD_DIM), lambda b, t: (li, b, 0, 0, 0))
    par = _lpar(P['kk'], li)
    return pl.pallas_call(
        functools.partial(_wkv_kernel, nb, tt),
        grid=(bsz // nb, tlen // tt),
        in_specs=[tok] * 5 + [lstt] + [par] * 5 + [pl.BlockSpec((SEG, SEG), lambda b, t: (0, 0))],
        out_specs=[tok, stt],
        out_shape=[jax.ShapeDtypeStruct((bsz, tlen, D_MODEL), f32),
                   jax.ShapeDtypeStruct((bsz, HEADS, HEAD_DIM, HEAD_DIM), f32)],
        scratch_shapes=[pltpu.VMEM((nb, N_PAIR, HEAD_DIM, PAIR), f32),
                        pltpu.VMEM((nb, tt, D_MODEL), f32), pltpu.VMEM((nb, tt, D_MODEL), f32),
                        pltpu.VMEM((nb, tt, D_MODEL), f32),
                        pltpu.VMEM((nb, P_ROWS, D_MODEL), jnp.bfloat16),
                        pltpu.VMEM((nb, P_ROWS, D_MODEL), f32)],
        compiler_params=_cparams(("arbitrary", "arbitrary")),
        name="wkv",
    )(r, ld, k, v, a, s0, P['kk'], P['ka'], P['rk'], P['lng'], P['lnb'], P['ones'])


LANES = 128


def _wkv_lanes_kernel(tlen, r_ref, ld_ref, k_ref, v_ref, a_ref, s0_ref, kkp_ref, kap_ref, rkp_ref, lng_ref, lnb_ref,
                      y_ref, s_ref, o_scr):
    s_ref[...] = s0_ref[...]
    for t in range(tlen):
        rt = r_ref[t]
        wt = jnp.exp(ld_ref[t])
        kraw = k_ref[t]
        at = a_ref[t]
        kkp = kraw * kkp_ref[...]
        kk = kkp * lax.rsqrt(jnp.maximum(jnp.sum(kkp * kkp, axis=0, keepdims=True), 1e-24))
        kt = kraw * (1.0 + (at - 1.0) * kap_ref[...])
        bt = kk * at
        nkk = -kk
        for i in range(HEAD_DIM):
            si = s_ref[i]
            sa = jnp.sum(si * nkk, axis=0, keepdims=True)
            sn = si * wt + sa * bt + v_ref[t, i:i + 1, :] * kt
            s_ref[i] = sn
            o_scr[i:i + 1, :] = jnp.sum(sn * rt, axis=0, keepdims=True)
        o = o_scr[...]
        mu = jnp.mean(o, axis=0, keepdims=True)
        od = o - mu
        var = jnp.mean(od * od, axis=0, keepdims=True)
        y = od * lax.rsqrt(var + GN_EPS) * lng_ref[...] + lnb_ref[...]
        bonus = jnp.sum(rt * kt * rkp_ref[...], axis=0, keepdims=True)
        y_ref[t] = y + bonus * v_ref[t]


def _wkv_lanes(r, ld, k, v, a, s0, P, li):
    tlen, _, _, bsz = r.shape
    f32 = jnp.float32
    tok = pl.BlockSpec((tlen, None, HEAD_DIM, bsz), lambda h: (0, h, 0, 0))
    lstt = pl.BlockSpec((None, None, HEAD_DIM, HEAD_DIM, bsz), lambda h: (li, h, 0, 0, 0))
    stt = pl.BlockSpec((None, HEAD_DIM, HEAD_DIM, bsz), lambda h: (h, 0, 0, 0))
    par = pl.BlockSpec((None, None, HEAD_DIM, bsz), lambda h: (li, h, 0, 0))
    hp = P['head_lanes']
    return pl.pallas_call(
        functools.partial(_wkv_lanes_kernel, tlen),
        grid=(HEADS,),
        in_specs=[tok] * 5 + [lstt] + [par] * 5,
        out_specs=[tok, stt],
        out_shape=[jax.ShapeDtypeStruct((tlen, HEADS, HEAD_DIM, bsz), f32),
                   jax.ShapeDtypeStruct((HEADS, HEAD_DIM, HEAD_DIM, bsz), f32)],
        scratch_shapes=[pltpu.VMEM((HEAD_DIM, bsz), f32)],
        compiler_params=_cparams(("arbitrary",)),
        name="wkv_lanes",
    )(r, ld, k, v, a, s0, hp['kk'], hp['ka'], hp['rk'], hp['lng'], hp['lnb'])


def _merge_kernel(h_ref, xa_ref, yw_ref, g_ref, xc_ref, zg_ref, wa_ref, wb_ref, wc_ref, bc_ref, wo_ref, o_ref):
    ya = _bdot(xa_ref[...], wa_ref[...])
    yb = _bdot(yw_ref[...] * g_ref[...], wb_ref[...])
    yc = _bdot(xc_ref[...], wc_ref[...]) + bc_ref[...]
    m = (_sigmoid(zg_ref[:, 0:D_MODEL]) * ya + _sigmoid(zg_ref[:, D_MODEL:2 * D_MODEL]) * yb
         + _sigmoid(zg_ref[:, 2 * D_MODEL:3 * D_MODEL]) * yc)
    o_ref[...] = h_ref[...] + _bdot(m, wo_ref[...])


def _merge(h, xa, yw, g, xc, z, P, li, tm):
    n = h.shape[0]
    tok = lambda width: pl.BlockSpec((tm, width), lambda i: (i, 0))
    par = lambda a: _lpar(a, li)
    ws = [P[n_] for n_ in ('sc_w_out', 'rk_w_out', 'cf_w_out', 'cf_b_out', 'w_o')]
    return pl.pallas_call(
        _merge_kernel,
        grid=(n // tm,),
        in_specs=[tok(D_MODEL), tok(D_SC), tok(D_MODEL), tok(D_MODEL), tok(D_CF), tok(Z_BLK)] + [par(a) for a in ws],
        out_specs=tok(D_MODEL),
        out_shape=jax.ShapeDtypeStruct((n, D_MODEL), jnp.float32),
        compiler_params=_cparams(("arbitrary",)),
        name="merge",
    )(h, xa, yw, g, xc, z, *ws)


def _mlp_ple_kernel(nk, final, h_ref, p_ref, g2_ref, w1_ref, w2_ref, gp_ref, wg_ref, wp_ref, *rest):
    gf_ref = rest[0] if final else None
    o_ref, hn_scr, acc_scr = rest[-3:]
    kstep = pl.program_id(1)

    @pl.when(kstep == 0)
    def _():
        hn_scr[...] = _rms(h_ref[...], g2_ref[...]).astype(jnp.bfloat16)
        acc_scr[...] = jnp.zeros_like(acc_scr)

    u = jnp.dot(hn_scr[...], w1_ref[...], preferred_element_type=jnp.float32)
    u = jnp.maximum(u, 0.0)
    acc_scr[...] += _bdot(u * u, w2_ref[...])

    @pl.when(kstep == nk - 1)
    def _():
        h1 = h_ref[...] + acc_scr[...]
        gate = _sigmoid(_bdot(_rms(h1, gp_ref[...]), wg_ref[...]))
        h2 = h1 + gate * _bdot(p_ref[...], wp_ref[...])
        o_ref[...] = h2 if gf_ref is None else _rms(h2, gf_ref[...])


def _mlp_ple(h, p, P, li, tm, tf, final_g=None):
    n = h.shape[0]
    nk = D_FF // tf
    par = lambda a: pl.BlockSpec((None,) + a.shape[1:], lambda *_: (li,) + (0,) * (a.ndim - 1),
                                 pipeline_mode=pl.Buffered(1))
    extra = [] if final_g is None else [final_g]
    return pl.pallas_call(
        functools.partial(_mlp_ple_kernel, nk, final_g is not None),
        grid=(n // tm, nk),
        in_specs=[pl.BlockSpec((tm, D_MODEL), lambda i, k: (i, 0)),
                  pl.BlockSpec((None, tm, D_PLE), lambda i, k: (li, i, 0)),
                  par(P['norm2_g']),
                  pl.BlockSpec((None, D_MODEL, tf), lambda i, k: (li, 0, k)),
                  pl.BlockSpec((None, tf, D_MODEL), lambda i, k: (li, k, 0)),
                  par(P['ple_norm_g']), par(P['ple_gate_w']), par(P['ple_w'])]
                 + [pl.BlockSpec((1, D_MODEL), lambda i, k: (0, 0)) for _ in extra],
        out_specs=pl.BlockSpec((tm, D_MODEL), lambda i, k: (i, 0)),
        out_shape=jax.ShapeDtypeStruct((n, D_MODEL), jnp.float32),
        scratch_shapes=[pltpu.VMEM((tm, D_MODEL), jnp.bfloat16), pltpu.VMEM((tm, D_MODEL), jnp.float32)],
        compiler_params=_cparams(("arbitrary", "arbitrary")),
        name="mlp_ple",
    )(h, p, P['norm2_g'], P['mlp_w1'], P['mlp_w2'], P['ple_norm_g'], P['ple_gate_w'], P['ple_w'], *extra)


def _prep(W):
    bf = jnp.bfloat16
    nl = W['w_in'].shape[0]
    row = lambda a: a.reshape(nl, 1, -1)
    w_in = W['w_in']
    n_sc, n_rk, n_cf = 3 * D_SC, RK_PROJ, 2 * D_CF
    o_rk, o_cf, o_gate = n_sc, n_sc + n_rk, n_sc + n_rk + n_cf
    w_perm = jnp.concatenate([
        w_in[:, :, o_gate:o_gate + 3 * D_MODEL], w_in[:, :, o_rk:o_rk + 3 * D_MODEL], w_in[:, :, 0:n_sc],
        w_in[:, :, o_cf:o_cf + n_cf], w_in[:, :, o_rk + 3 * D_MODEL:o_rk + RK_PROJ],
        jnp.zeros((nl, D_MODEL, LORA_PAD - N_LORA), w_in.dtype)], axis=2).astype(bf)
    mu = W['rk_mu']
    zrows = lambda n: jnp.zeros((nl, n, D_MODEL), jnp.float32)
    o_a, o_g = DECAY_LORA, DECAY_LORA + AAA_LORA + MV_LORA
    seg_head = jnp.arange(SEG) // HEAD_DIM
    on_lanes = lambda a: jnp.broadcast_to(a.reshape(nl, HEADS, HEAD_DIM, 1), (nl, HEADS, HEAD_DIM, LANES))
    return {
        'head_lanes': {'kk': on_lanes(W['rk_k_k']), 'ka': on_lanes(W['rk_k_a']), 'rk': on_lanes(W['rk_r_k']),
                       'lng': on_lanes(W['rk_ln_g']), 'lnb': on_lanes(W['rk_ln_b'])},
        'ones': (seg_head[:, None] == seg_head[None, :]).astype(bf),
        'kk': row(W['rk_k_k']), 'ka': row(W['rk_k_a']), 'rk': row(W['rk_r_k']),
        'lng': row(W['rk_ln_g']), 'lnb': row(W['rk_ln_b']),
        'norm1_g': row(W['norm1_g']), 'w_in': w_perm,
        'mu1': row(mu[:, :3 * D_MODEL]),
        'mu2': row(jnp.pad(mu[:, 3 * D_MODEL:], ((0, 0), (0, LORA_PAD - N_LORA)))),
        'w0': row(W['rk_w0']), 'a0': row(W['rk_a0']), 'v0': row(W['rk_v0']),
        'w2p': jnp.concatenate([W['rk_w2'], zrows(128 - o_a)], axis=1).astype(bf),
        'a2p': jnp.concatenate([zrows(o_a), W['rk_a2']], axis=1).astype(bf),
        'v2p': jnp.concatenate([W['rk_v2'], zrows(128 - MV_LORA)], axis=1).astype(bf),
        'g2p': jnp.concatenate([zrows(o_g - 128), W['rk_g2'], zrows(384 - N_LORA)], axis=1).astype(bf),
        'scw': W['sc_conv_w'], 'cfbin': row(W['cf_b_in']), 'cfw': W['cf_dw_w'],
        'cfdb': row(W['cf_dw_b']), 'cflg': row(W['cf_ln_g']), 'cflb': row(W['cf_ln_b']),
        'sc_w_out': W['sc_w_out'].astype(bf), 'rk_w_out': W['rk_w_out'].astype(bf),
        'cf_w_out': W['cf_w_out'].astype(bf), 'cf_b_out': row(W['cf_b_out']), 'w_o': W['w_o'].astype(bf),
        'norm2_g': row(W['norm2_g']), 'mlp_w1': W['mlp_w1'].astype(bf), 'mlp_w2': W['mlp_w2'].astype(bf),
        'ple_norm_g': row(W['ple_norm_g']), 'ple_gate_w': W['ple_gate_w'].astype(bf),
        'ple_w': W['ple_w'].astype(bf),
    }


def _pick(n, pref):
    return pref if n % pref == 0 else n


def _trunk(x, p, sc0, shift0, wkv0, cf0, P, final_g, cfg):
    bsz, tlen, _ = x.shape
    n = bsz * tlen
    tm = _pick(n, cfg['tm'])
    tt = _pick(tlen, cfg['tt'])
    tpad = -tlen % CHUNK
    tw = _pick(tlen + tpad, cfg['tw'])
    pad_t = lambda a: jnp.pad(a, ((0, 0), (0, tpad), (0, 0))) if tpad else a
    h = x.reshape(n, D_MODEL)
    pf = p.reshape(DEPTH, n, D_PLE)
    sh1 = shift0[:, :, None, :3 * D_MODEL]
    sh2 = jnp.pad(shift0[:, :, 3 * D_MODEL:], ((0, 0), (0, 0), (0, LORA_PAD - N_LORA)))[:, :, None, :]
    lanes_form = bsz == LANES and tlen <= CHUNK
    to_lanes = lambda a: a.reshape(bsz, tlen, HEADS, HEAD_DIM).transpose(1, 2, 3, 0)
    wkv0_l = wkv0.transpose(0, 2, 3, 4, 1) if lanes_form else None
    v_first = None
    new_sc, new_shift, new_wkv, new_cf = [], [], [], []
    for i in range(DEPTH):
        z = _in_proj(h, P, i, tm)
        (r, w, k, v, a, g, xa, xc, nsh1, nsh2, nsc, ncf) = _seq(
            z.reshape(bsz, tlen, NZ), sh1, sh2, sc0, cf0, v_first, P, i, tt, cfg['nbs'])
        if i == 0:
            v_first = v
        if lanes_form:
            y, s_out = _wkv_lanes(to_lanes(r), to_lanes(w), to_lanes(k), to_lanes(v), to_lanes(a), wkv0_l, P, i)
            yw = y.transpose(3, 0, 1, 2).reshape(n, D_MODEL)
        else:
            y, s_out = _wkv(pad_t(r), pad_t(w), pad_t(k), pad_t(v), pad_t(a), wkv0, P, i, cfg['nb'], tw)
            yw = y[:, :tlen].reshape(n, D_MODEL)
        h = _merge(h, xa.reshape(n, D_SC), yw, g.reshape(n, D_MODEL), xc.reshape(n, D_CF), z, P, i, tm)
        h = _mlp_ple(h, pf, P, i, _pick(n, cfg['tmm']), cfg['tf'], final_g if i == DEPTH - 1 else None)
        new_sc.append(nsc)
        new_shift.append(jnp.concatenate([nsh1[:, 0, :], nsh2[:, 0, :N_LORA]], axis=-1))
        new_wkv.append(s_out)
        new_cf.append(ncf)
    y = h.reshape(bsz, tlen, D_MODEL)
    wkv_out = jnp.stack(new_wkv)
    if lanes_form:
        wkv_out = wkv_out.transpose(0, 4, 1, 2, 3)
    return y, jnp.stack(new_sc), jnp.stack(new_shift), wkv_out, jnp.stack(new_cf)


PROMPT_CFG = {'tm': 512, 'tmm': 1024, 'tt': 256, 'nbs': 1, 'tw': 64, 'nb': 8, 'tf': 1024}
SAMPLE_CFG = {'tm': 512, 'tmm': 512, 'tt': 4, 'nbs': 8, 'tw': 8, 'nb': 8, 'tf': 1024}


def kernel(x_prompt, x_sample, p_prompt, p_sample, state_sconv, state_shift, state_wkv, state_cconv,
           norm1_g, w_in, sc_conv_w, sc_w_out, rk_mu, rk_w0, rk_w2, rk_a0, rk_a2, rk_v0, rk_v2, rk_g2,
           rk_k_k, rk_k_a, rk_r_k, rk_ln_g, rk_ln_b, rk_w_out, cf_b_in, cf_dw_w, cf_dw_b, cf_ln_g, cf_ln_b,
           cf_w_out, cf_b_out, w_o, norm2_g, mlp_w1, mlp_w2, ple_w, ple_gate_w, ple_norm_g, final_norm_g):
    W = dict(norm1_g=norm1_g, w_in=w_in, sc_conv_w=sc_conv_w, sc_w_out=sc_w_out, rk_mu=rk_mu, rk_w0=rk_w0,
             rk_w2=rk_w2, rk_a0=rk_a0, rk_a2=rk_a2, rk_v0=rk_v0, rk_v2=rk_v2, rk_g2=rk_g2, rk_k_k=rk_k_k,
             rk_k_a=rk_k_a, rk_r_k=rk_r_k, rk_ln_g=rk_ln_g, rk_ln_b=rk_ln_b, rk_w_out=rk_w_out, cf_b_in=cf_b_in,
             cf_dw_w=cf_dw_w, cf_dw_b=cf_dw_b, cf_ln_g=cf_ln_g, cf_ln_b=cf_ln_b, cf_w_out=cf_w_out,
             cf_b_out=cf_b_out, w_o=w_o, norm2_g=norm2_g, mlp_w1=mlp_w1, mlp_w2=mlp_w2, ple_w=ple_w,
             ple_gate_w=ple_gate_w, ple_norm_g=ple_norm_g)
    P = _prep(W)
    final_g = final_norm_g.reshape(1, D_MODEL)
    bp = x_prompt.shape[0]
    f32 = jnp.float32
    zsc = jnp.zeros((DEPTH, bp, SC_WIDTH - 1, D_SC), f32)
    zsh = jnp.zeros((DEPTH, bp, RK_PROJ), f32)
    zwkv = jnp.zeros((DEPTH, bp, HEADS, HEAD_DIM, HEAD_DIM), f32)
    zcf = jnp.zeros((DEPTH, bp, CF_WIDTH - 1, D_CF), f32)
    y_p, sc_p, sh_p, wkv_p, cf_p = _trunk(x_prompt, p_prompt, zsc, zsh, zwkv, zcf, P, final_g, PROMPT_CFG)
    y_s, sc_s, sh_s, wkv_s, cf_s = _trunk(x_sample, p_sample, state_sconv, state_shift, state_wkv, state_cconv,
                                          P, final_g, SAMPLE_CFG)
    return (y_p, y_s, sc_p, sh_p, wkv_p, cf_p, sc_s, sh_s, wkv_s, cf_s)
```
